```python
import math
import jax, jax.numpy as jnp
from jax import lax
import numpy as np

D_MODEL = 2048
BATCH = 4
SEQ = 4096
DEPTH = 4
DEC_BATCH = 32
DEC_SEQ = 16
PAST_LEN = 2048

CHUNK = 64
Q_BLOCK = 128
MIX_WIDTH = D_MODEL
POOL_WIDTH = MIX_WIDTH // 2
N_POOL_GROUPS = 4
POOL_GROUP = POOL_WIDTH // N_POOL_GROUPS
POOL_WINDOWS = (2, 4, 8, 16)
POOL_HIST = max(POOL_WINDOWS) - 1
ATTN_WIDTH = MIX_WIDTH - POOL_WIDTH
HEAD_DIM = 64
N_HEADS = ATTN_WIDTH // (2 * HEAD_DIM)
V_DIM = 2 * HEAD_DIM
IN_COLS = POOL_WIDTH + 3 * ATTN_WIDTH
D_FF = -(-8 * D_MODEL // (3 * 256)) * 256
ROPE_THETA = 10000.0
NORM_EPS = 1e-6
SUBLN_EPS = 1e-5
N_MOD = 6

kernel_name = "hymba_pool_diffattn_stream_step"

F32 = jnp.float32


def rms_norm(x, g, eps=NORM_EPS):
    x32 = x.astype(F32)
    y = x32 * lax.rsqrt(jnp.mean(x32 * x32, axis=-1, keepdims=True) + eps)
    return (y * g).astype(x.dtype)


def rope(x, pos):
    half = HEAD_DIM // 2
    inv = 1.0 / (ROPE_THETA ** (jnp.arange(half, dtype=F32) * (2.0 / HEAD_DIM)))
    ang = pos.astype(F32)[:, None] * inv[None, :]
    cos = jnp.cos(ang)[None, :, None, :]
    sin = jnp.sin(ang)[None, :, None, :]
    x32 = x.astype(F32)
    x1, x2 = x32[..., :half], x32[..., half:]
    return jnp.concatenate([x1 * cos - x2 * sin, x2 * cos + x1 * sin], axis=-1).astype(x.dtype)


def diff_attn(q, k, v, q_pos, k_pos, lam):
    s = jnp.einsum('bqhd,bkhd->bhqk', q, k, preferred_element_type=F32) * (HEAD_DIM ** -0.5)
    visible = (k_pos // CHUNK)[None, :] <= (q_pos // CHUNK)[:, None]
    s = jnp.where(visible[None, None], s, -jnp.inf)
    p = jax.nn.softmax(s, axis=-1)
    b, _, sq, sk = p.shape
    p = p.reshape(b, N_HEADS, 2, sq, sk)
    a = p[:, :, 0] - lam * p[:, :, 1]
    return jnp.einsum('bhqk,bkhe->bqhe', a.astype(v.dtype), v)


def prompt_attn(q, k, v, lam):
    b, s = q.shape[:2]
    nb = s // Q_BLOCK
    qb = q.reshape(b, nb, Q_BLOCK, 2 * N_HEADS, HEAD_DIM).transpose(1, 0, 2, 3, 4)
    k_pos = jnp.arange(s)

    def one(args):
        qblk, i = args
        q_pos = i * Q_BLOCK + jnp.arange(Q_BLOCK)
        return diff_attn(qblk, k, v, q_pos, k_pos, lam)

    out = lax.map(one, (qb, jnp.arange(nb)))
    return out.transpose(1, 0, 2, 3, 4).reshape(b, s, N_HEADS, V_DIM)


def pool_mix(u, hist, pos0, w_pool, pool_scale):
    b, s, _ = u.shape
    ext = jnp.concatenate([hist, u], axis=1)
    csum = jnp.concatenate([jnp.zeros((b, 1, POOL_WIDTH), F32), jnp.cumsum(ext.astype(F32), axis=1)], axis=1)
    end = csum[:, POOL_HIST + 1:]
    pos = pos0 + jnp.arange(s)
    u32 = u.astype(F32)
    groups = []
    for gi, w in enumerate(POOL_WINDOWS):
        c0, c1 = gi * POOL_GROUP, (gi + 1) * POOL_GROUP
        start = csum[:, POOL_HIST + 1 - w: POOL_HIST + 1 - w + s, c0:c1]
        count = jnp.minimum(w, pos + 1).astype(F32)[None, :, None]
        groups.append((end[..., c0:c1] - start) / count - u32[..., c0:c1])
    d = jnp.stack(groups, axis=2).astype(u.dtype)
    y = jnp.einsum('bsgc,gcd->bsgd', d, w_pool) * pool_scale
    return y.reshape(b, s, POOL_WIDTH), ext[:, -POOL_HIST:]


def token_mixer(h, pool_hist, k_past, v_past, pos0, lam_init, w_in, w_pool, pool_scale,
                lam_q1, lam_k1, lam_q2, lam_k2, subln_g, w_out):
    b, s, _ = h.shape
    proj = h @ w_in
    u = proj[..., :POOL_WIDTH]
    q = proj[..., POOL_WIDTH:POOL_WIDTH + ATTN_WIDTH].reshape(b, s, 2 * N_HEADS, HEAD_DIM)
    k = proj[..., POOL_WIDTH + ATTN_WIDTH:POOL_WIDTH + 2 * ATTN_WIDTH].reshape(b, s, 2 * N_HEADS, HEAD_DIM)
    v = proj[..., POOL_WIDTH + 2 * ATTN_WIDTH:].reshape(b, s, N_HEADS, V_DIM)
    pos = pos0 + jnp.arange(s)
    q = rope(q, pos)
    k = rope(k, pos)
    lam = (jnp.exp(jnp.sum(lam_q1.astype(F32) * lam_k1.astype(F32)))
           - jnp.exp(jnp.sum(lam_q2.astype(F32) * lam_k2.astype(F32))) + lam_init)
    if k_past is None:
        att = prompt_attn(q, k, v, lam)
    else:
        k_all = jnp.concatenate([k_past, k], axis=1)
        v_all = jnp.concatenate([v_past, v], axis=1)
        att = diff_attn(q, k_all, v_all, pos, jnp.arange(k_all.shape[1]), lam)
    att = rms_norm(att, subln_g, SUBLN_EPS) * (1.0 - lam_init)
    pool_out, new_hist = pool_mix(u, pool_hist, pos0, w_pool, pool_scale)
    mixed = jnp.concatenate([pool_out, att.reshape(b, s, ATTN_WIDTH).astype(pool_out.dtype)], axis=-1)
    return mixed @ w_out, k, v, new_hist


def run_trunk(x, c, pos0, pool_state, k_cache, v_cache, w_mod, b_mod, norm_mix_g, w_in, w_pool,
              pool_scale, lam_q1, lam_k1, lam_q2, lam_k2, subln_g, w_out, norm_ffn_g,
              w_gate, w_up, w_down, final_g):
    b = x.shape[0]
    cs = jax.nn.silu(c)
    ks, vs, hs = [], [], []
    for l in range(DEPTH):
        mod = cs @ w_mod[l] + b_mod[l]
        sh_a, sc_a, g_a, sh_f, sc_f, g_f = jnp.split(mod[:, None, :], N_MOD, axis=-1)
        lam_init = 0.8 - 0.6 * math.exp(-0.3 * l)
        hist = jnp.zeros((b, POOL_HIST, POOL_WIDTH), x.dtype) if pool_state is None else pool_state[l]
        kp = None if k_cache is None else k_cache[l]
        vp = None if v_cache is None else v_cache[l]
        h = rms_norm(x, norm_mix_g[l]) * (1 + sc_a) + sh_a
        out, k, v, nh = token_mixer(h, hist, kp, vp, pos0, lam_init, w_in[l], w_pool[l], pool_scale[l],
                                    lam_q1[l], lam_k1[l], lam_q2[l], lam_k2[l], subln_g[l], w_out[l])
        x = x + g_a * out
        h = rms_norm(x, norm_ffn_g[l]) * (1 + sc_f) + sh_f
        x = x + g_f * ((jax.nn.silu(h @ w_gate[l]) * (h @ w_up[l])) @ w_down[l])
        ks.append(k)
        vs.append(v)
        hs.append(nh)
    return rms_norm(x, final_g), jnp.stack(ks), jnp.stack(vs), jnp.stack(hs)


def setup_inputs(seed: int = 0) -> dict:
    key = jax.random.key(seed)
    ks = jax.random.split(key, 26)
    n = lambda k, shp, sc: jax.random.normal(k, shp, F32) * sc
    return {
        "x_prompt": n(ks[0], (BATCH, SEQ, D_MODEL), 1.0),
        "x_sample": n(ks[1], (DEC_BATCH, DEC_SEQ, D_MODEL), 1.0),
        "c_prompt": n(ks[2], (BATCH, D_MODEL), 1.0),
        "c_sample": n(ks[3], (DEC_BATCH, D_MODEL), 1.0),
        "cache_k": n(ks[4], (DEPTH, DEC_BATCH, PAST_LEN, 2 * N_HEADS, HEAD_DIM), 1.0),
        "cache_v": n(ks[5], (DEPTH, DEC_BATCH, PAST_LEN, N_HEADS, V_DIM), 1.0),
        "state_pool": n(ks[6], (DEPTH, DEC_BATCH, POOL_HIST, POOL_WIDTH), 1.0),
        "w_mod": n(ks[7], (DEPTH, D_MODEL, N_MOD * D_MODEL), 0.5 * D_MODEL ** -0.5),
        "b_mod": n(ks[8], (DEPTH, N_MOD * D_MODEL), 0.01),
        "norm_mix_g": 1.0 + n(ks[9], (DEPTH, D_MODEL), 0.02),
        "w_in": n(ks[10], (DEPTH, D_MODEL, IN_COLS), D_MODEL ** -0.5),
        "w_pool": n(ks[11], (DEPTH, N_POOL_GROUPS, POOL_GROUP, POOL_GROUP), POOL_GROUP ** -0.5),
        "pool_scale": 1.0 + n(ks[12], (DEPTH, N_POOL_GROUPS, POOL_GROUP), 0.1),
        "lam_q1": n(ks[13], (DEPTH, HEAD_DIM), 0.1),
        "lam_k1": n(ks[14], (DEPTH, HEAD_DIM), 0.1),
        "lam_q2": n(ks[15], (DEPTH, HEAD_DIM), 0.1),
        "lam_k2": n(ks[16], (DEPTH, HEAD_DIM), 0.1),
        "subln_g": 1.0 + n(ks[17], (DEPTH, V_DIM), 0.02),
        "w_out": n(ks[18], (DEPTH, MIX_WIDTH, D_MODEL), MIX_WIDTH ** -0.5),
        "norm_ffn_g": 1.0 + n(ks[19], (DEPTH, D_MODEL), 0.02),
        "w_gate": n(ks[20], (DEPTH, D_MODEL, D_FF), D_MODEL ** -0.5),
        "w_up": n(ks[21], (DEPTH, D_MODEL, D_FF), D_MODEL ** -0.5),
        "w_down": n(ks[22], (DEPTH, D_FF, D_MODEL), D_FF ** -0.5),
        "final_g": 1.0 + n(ks[23], (D_MODEL,), 0.02),
    }


def reference(x_prompt, x_sample, c_prompt, c_sample, cache_k, cache_v, state_pool, w_mod, b_mod,
              norm_mix_g, w_in, w_pool, pool_scale, lam_q1, lam_k1, lam_q2, lam_k2, subln_g, w_out,
              norm_ffn_g, w_gate, w_up, w_down, final_g):
    y_prompt, k_prompt, v_prompt, pool_prompt = run_trunk(
        x_prompt, c_prompt, 0, None, None, None, w_mod, b_mod, norm_mix_g, w_in, w_pool, pool_scale,
        lam_q1, lam_k1, lam_q2, lam_k2, subln_g, w_out, norm_ffn_g, w_gate, w_up, w_down, final_g)
    y_sample, k_sample, v_sample, pool_sample = run_trunk(
        x_sample, c_sample, PAST_LEN, state_pool, cache_k, cache_v, w_mod, b_mod, norm_mix_g, w_in,
        w_pool, pool_scale, lam_q1, lam_k1, lam_q2, lam_k2, subln_g, w_out, norm_ffn_g, w_gate, w_up,
        w_down, final_g)
    return (y_prompt, y_sample, k_prompt, v_prompt, pool_prompt, k_sample, v_sample, pool_sample)
```

```python
import functools
import math
from typing import NamedTuple

import jax
import jax.numpy as jnp
from jax import lax
from jax.experimental import pallas as pl
from jax.experimental.pallas import tpu as pltpu

F32 = jnp.float32
BF16 = jnp.bfloat16

D_MODEL = 2048
BATCH = 4
SEQ = 4096
DEPTH = 4
DEC_BATCH = 32
DEC_SEQ = 16
PAST_LEN = 2048
CHUNK = 64
POOL_WIDTH = D_MODEL // 2
N_POOL_GROUPS = 4
POOL_GROUP = POOL_WIDTH // N_POOL_GROUPS
POOL_WINDOWS = (2, 4, 8, 16)
POOL_HIST = max(POOL_WINDOWS) - 1
HALO = POOL_HIST + 1
ATTN_WIDTH = D_MODEL - POOL_WIDTH
HEAD_DIM = 64
N_HEADS = ATTN_WIDTH // (2 * HEAD_DIM)
V_DIM = 2 * HEAD_DIM
IN_COLS = POOL_WIDTH + 3 * ATTN_WIDTH
D_FF = -(-8 * D_MODEL // (3 * 256)) * 256
ROPE_THETA = 10000.0
NORM_EPS = 1e-6
SUBLN_EPS = 1e-5
N_MOD = 6
MOD_SHIFT_A, MOD_SCALE_A, MOD_GATE_A, MOD_SHIFT_F, MOD_SCALE_F, MOD_GATE_F = range(N_MOD)

V7X_VMEM_BYTES = 64 * 1024 * 1024
V7X_LANES = 128
VMEM_LIMIT = V7X_VMEM_BYTES * 7 // 8

TM = 512
TM_FFN = 1024
TN_MOD = 1024
TF = 512
TN_DOWN = 512
TQ = 512
TK_DEC = 1024


class Stream(NamedTuple):
    batch: int
    seq: int
    pos0: int
    mod_rows: int

    @property
    def rows(self):
        return self.batch * self.seq


PROMPT = Stream(BATCH, SEQ, 0, 1)
SAMPLE = Stream(DEC_BATCH, DEC_SEQ, PAST_LEN, DEC_BATCH * DEC_SEQ)


def _params(*semantics):
    return pltpu.CompilerParams(dimension_semantics=semantics, vmem_limit_bytes=VMEM_LIMIT)


def _rms(x, eps):
    return x * lax.rsqrt(jnp.mean(x * x, axis=-1, keepdims=True) + eps)


def _mod_spec(st, layer, which, tm, tn=D_MODEL):
    per_chunk = D_MODEL // tn
    if st.mod_rows == 1:
        tiles_per_batch = st.seq // tm
        return pl.BlockSpec(
            (None, 1, tn),
            lambda i, j: (layer * st.batch + i // tiles_per_batch, 0, which * per_chunk + j % per_chunk))
    return pl.BlockSpec((None, tm, tn), lambda i, j: (layer, i, which * per_chunk + j % per_chunk))


def _mod_kernel(c_ref, w_ref, b_ref, o_ref):
    c = c_ref[...]
    cs = (c * jax.nn.sigmoid(c)).astype(BF16)
    o_ref[...] = jnp.dot(cs, w_ref[...].astype(BF16), preferred_element_type=F32) + b_ref[...]


def _modulation(c_all, w_mod, b_mod):
    nb = c_all.shape[0]
    n = N_MOD * D_MODEL
    return pl.pallas_call(
        _mod_kernel,
        grid=(DEPTH, n // TN_MOD),
        in_specs=[
            pl.BlockSpec((nb, D_MODEL), lambda l, j: (0, 0)),
            pl.BlockSpec((None, D_MODEL, TN_MOD), lambda l, j: (l, 0, j)),
            pl.BlockSpec((None, 1, TN_MOD), lambda l, j: (l, 0, j)),
        ],
        out_specs=pl.BlockSpec((None, nb, TN_MOD), lambda l, j: (l, 0, j)),
        out_shape=jax.ShapeDtypeStruct((DEPTH, nb, n), F32),
        compiler_params=_params("arbitrary", "arbitrary"),
        name="modulation",
    )(c_all, w_mod, b_mod.reshape(DEPTH, 1, n))


def _rope_store(acc, cos_ref, sin_ref, scale, out_refs):
    tm = acc.shape[0]
    lane = lax.broadcasted_iota(jnp.int32, (tm, V7X_LANES), 1)
    first_half = (lane % HEAD_DIM) < (HEAD_DIM // 2)
    cos = cos_ref[...]
    sin = sin_ref[...]
    for c in range(ATTN_WIDTH // V7X_LANES):
        cols = slice(c * V7X_LANES, (c + 1) * V7X_LANES)
        xs = acc[:, cols]
        partner = jnp.where(first_half,
                            pltpu.roll(xs, V7X_LANES - HEAD_DIM // 2, 1),
                            pltpu.roll(xs, HEAD_DIM // 2, 1))
        y = xs * cos + partner * sin
        if scale != 1.0:
            y = y * scale
        for r in out_refs:
            r[:, cols] = y.astype(r.dtype)


def _in_proj_kernel(x_ref, g_ref, sc_ref, sh_ref, w_ref, cos_ref, sin_ref,
                    u_ref, q_ref, k_ref, kb_ref, v_ref, vb_ref, h_scr):
    j = pl.program_id(1)

    @pl.when(j == 0)
    def _():
        y = _rms(x_ref[...], NORM_EPS) * g_ref[...]
        h_scr[...] = (y * (1.0 + sc_ref[...]) + sh_ref[...]).astype(BF16)

    acc = jnp.dot(h_scr[...], w_ref[...], preferred_element_type=F32)

    @pl.when(j == 0)
    def _():
        u_ref[...] = acc

    @pl.when(j == 1)
    def _():
        _rope_store(acc, cos_ref, sin_ref, HEAD_DIM ** -0.5, (q_ref,))

    @pl.when(j == 2)
    def _():
        _rope_store(acc, cos_ref, sin_ref, 1.0, (k_ref, kb_ref))

    @pl.when(j == 3)
    def _():
        v_ref[...] = acc
        vb_ref[...] = acc.astype(BF16)


def _in_proj(st, layer, x, mod, norm_g, w_in, cos_tab, sin_tab):
    m = st.rows
    tm = TM
    tab_tiles = cos_tab.shape[0] // tm
    row = lambda i, j: (i, 0)
    out_spec = pl.BlockSpec((tm, ATTN_WIDTH), row)
    sds = lambda dt: jax.ShapeDtypeStruct((m, ATTN_WIDTH), dt)
    return pl.pallas_call(
        _in_proj_kernel,
        grid=(m // tm, IN_COLS // ATTN_WIDTH),
        in_specs=[
            pl.BlockSpec((tm, D_MODEL), row),
            pl.BlockSpec((None, 1, D_MODEL), lambda i, j: (layer, 0, 0)),
            _mod_spec(st, layer, MOD_SCALE_A, tm),
            _mod_spec(st, layer, MOD_SHIFT_A, tm),
            pl.BlockSpec((None, D_MODEL, ATTN_WIDTH), lambda i, j: (layer, 0, j)),
            pl.BlockSpec((tm, V7X_LANES), lambda i, j: (i % tab_tiles, 0)),
            pl.BlockSpec((tm, V7X_LANES), lambda i, j: (i % tab_tiles, 0)),
        ],
        out_specs=[out_spec] * 6,
        out_shape=[sds(F32), sds(BF16), sds(F32), sds(BF16), sds(F32), sds(BF16)],
        scratch_shapes=[pltpu.VMEM((tm, D_MODEL), BF16)],
        compiler_params=_params("arbitrary", "arbitrary"),
        name="in_proj",
    )(x, norm_g, mod, mod, w_in, cos_tab, sin_tab)


def _pool_kernel(u_ref, hist_ref, wp_ref, ps_ref, o_ref, ext_ref, *, tm, pos0):
    i = pl.program_id(1)

    @pl.when(i == 0)
    def _():
        ext_ref[0:HALO, :] = hist_ref[...]

    @pl.when(i > 0)
    def _():
        ext_ref[0:HALO, :] = ext_ref[tm:tm + HALO, :]

    ext_ref[HALO:, :] = u_ref[...]
    pos = pos0 + i * tm + lax.broadcasted_iota(jnp.int32, (tm, 1), 0)
    for g, w in enumerate(POOL_WINDOWS):
        cols = slice(g * POOL_GROUP, (g + 1) * POOL_GROUP)
        cur = u_ref[:, cols]
        s = cur
        for back in range(1, w):
            s = s + ext_ref[HALO - back:HALO - back + tm, cols]
        count = jnp.minimum(w, pos + 1).astype(F32)
        d = (s / count - cur).astype(BF16)
        y = jnp.dot(d, wp_ref[g], preferred_element_type=F32) * ps_ref[:, cols]
        o_ref[:, cols] = y.astype(BF16)


def _pool(st, layer, u, hist, w_pool, pool_scale):
    tm = min(TM, st.seq)
    tiles = st.seq // tm
    return pl.pallas_call(
        functools.partial(_pool_kernel, tm=tm, pos0=st.pos0),
        grid=(st.batch, tiles),
        in_specs=[
            pl.BlockSpec((tm, POOL_WIDTH), lambda b, i: (b * tiles + i, 0)),
            pl.BlockSpec((None, HALO, POOL_WIDTH), lambda b, i: (b, 0, 0)),
            pl.BlockSpec((None, N_POOL_GROUPS, POOL_GROUP, POOL_GROUP), lambda b, i: (layer, 0, 0, 0)),
            pl.BlockSpec((None, 1, POOL_WIDTH), lambda b, i: (layer, 0, 0)),
        ],
        out_specs=pl.BlockSpec((tm, POOL_WIDTH), lambda b, i: (b * tiles + i, 0)),
        out_shape=jax.ShapeDtypeStruct((st.rows, POOL_WIDTH), BF16),
        scratch_shapes=[pltpu.VMEM((tm + HALO, POOL_WIDTH), F32)],
        compiler_params=_params("arbitrary", "arbitrary"),
        name="pool_mix",
    )(u, hist, w_pool, pool_scale)


def _lambda(lam_ref, lam_init):
    lam = lam_ref[...]
    s1 = jnp.sum(lam[0:1] * lam[1:2], axis=-1, keepdims=True)
    s2 = jnp.sum(lam[2:3] * lam[3:4], axis=-1, keepdims=True)
    return jnp.exp(s1) - jnp.exp(s2) + lam_init


def _online_softmax_step(s, v, m_ref, l_ref, acc_ref):
    m_prev = m_ref[...]
    m_new = jnp.maximum(m_prev, jnp.max(s, axis=-1, keepdims=True))
    alpha = jnp.exp(m_prev - m_new)
    p = jnp.exp(s - m_new)
    l_ref[...] = alpha * l_ref[...] + jnp.sum(p, axis=-1, keepdims=True)
    acc_ref[...] = alpha * acc_ref[...] + jnp.dot(p.astype(BF16), v, preferred_element_type=F32)
    m_ref[...] = m_new


def _sub_ln(o1, o2, lam, g, lam_init):
    a = o1 - lam * o2
    return (_rms(a, SUBLN_EPS) * g) * (1.0 - lam_init)


_NT = (((1,), (1,)), ((), ()))


def _prompt_attn_kernel(q_ref, k_ref, v_ref, lam_ref, g_ref, o_ref,
                        qz_ref, m_ref, l_ref, acc_ref, *, lam_init):
    i = pl.program_id(2)
    q = q_ref[...]
    lane = lax.broadcasted_iota(jnp.int32, (TQ, V_DIM), 1)
    zero = jnp.zeros_like(q)
    qz_ref[0:TQ, :] = jnp.where(lane < HEAD_DIM, q, zero)
    qz_ref[TQ:, :] = jnp.where(lane >= HEAD_DIM, q, zero)
    m_ref[...] = jnp.full_like(m_ref, -jnp.inf)
    l_ref[...] = jnp.zeros_like(l_ref)
    acc_ref[...] = jnp.zeros_like(acc_ref)

    def full_tile(t, carry):
        start = pl.multiple_of(t * TQ, TQ)
        s = lax.dot_general(qz_ref[...], k_ref[pl.ds(start, TQ), :], _NT, preferred_element_type=F32)
        _online_softmax_step(s, v_ref[pl.ds(start, TQ), :], m_ref, l_ref, acc_ref)
        return carry

    lax.fori_loop(0, i, full_tile, 0)

    start = pl.multiple_of(i * TQ, TQ)
    s = lax.dot_general(qz_ref[...], k_ref[pl.ds(start, TQ), :], _NT, preferred_element_type=F32)
    row = lax.broadcasted_iota(jnp.int32, (2 * TQ, TQ), 0)
    col = lax.broadcasted_iota(jnp.int32, (2 * TQ, TQ), 1)
    q_local = jnp.where(row >= TQ, row - TQ, row)
    s = jnp.where((col // CHUNK) <= (q_local // CHUNK), s, -jnp.inf)
    _online_softmax_step(s, v_ref[pl.ds(start, TQ), :], m_ref, l_ref, acc_ref)

    o = acc_ref[...] / l_ref[...]
    y = _sub_ln(o[0:TQ], o[TQ:], _lambda(lam_ref, lam_init), g_ref[...], lam_init)
    o_ref[...] = y.astype(BF16)


def _prompt_attn(st, layer, q, k, v, lam, subln_g, lam_init):
    nq = st.seq // TQ
    return pl.pallas_call(
        functools.partial(_prompt_attn_kernel, lam_init=lam_init),
        grid=(st.batch, N_HEADS, nq),
        in_specs=[
            pl.BlockSpec((TQ, V_DIM), lambda b, h, i: (b * nq + i, h)),
            pl.BlockSpec((st.seq, V_DIM), lambda b, h, i: (b, h)),
            pl.BlockSpec((st.seq, V_DIM), lambda b, h, i: (b, h)),
            pl.BlockSpec((None, 4, HEAD_DIM), lambda b, h, i: (layer, 0, 0)),
            pl.BlockSpec((None, 1, V_DIM), lambda b, h, i: (layer, 0, 0)),
        ],
        out_specs=pl.BlockSpec((TQ, V_DIM), lambda b, h, i: (b * nq + i, h)),
        out_shape=jax.ShapeDtypeStruct((st.rows, ATTN_WIDTH), BF16),
        scratch_shapes=[
            pltpu.VMEM((2 * TQ, V_DIM), BF16),
            pltpu.VMEM((2 * TQ, 1), F32),
            pltpu.VMEM((2 * TQ, 1), F32),
            pltpu.VMEM((2 * TQ, V_DIM), F32),
        ],
        compiler_params=_params("arbitrary", "arbitrary", "arbitrary"),
        name="prompt_attn",
    )(q, k, v, lam, subln_g)


def _decode_attn_kernel(q_ref, kc_ref, vc_ref, kn_ref, vn_ref, lam_ref, g_ref, o_ref,
                        qbd_ref, m_ref, l_ref, acc_ref, *, lam_init):
    t = pl.program_id(1)
    n_score_heads = 2 * N_HEADS

    @pl.when(t == 0)
    def _():
        q = q_ref[...]
        col = lax.broadcasted_iota(jnp.int32, (DEC_SEQ, ATTN_WIDTH), 1)
        zero = jnp.zeros_like(q)
        for h in range(n_score_heads):
            qbd_ref[h * DEC_SEQ:(h + 1) * DEC_SEQ, :] = jnp.where(col // HEAD_DIM == h, q, zero)
        m_ref[...] = jnp.full_like(m_ref, -jnp.inf)
        l_ref[...] = jnp.zeros_like(l_ref)
        acc_ref[...] = jnp.zeros_like(acc_ref)

    def step(k, v):
        s = lax.dot_general(qbd_ref[...], k, _NT, preferred_element_type=F32)
        _online_softmax_step(s, v, m_ref, l_ref, acc_ref)

    step(kc_ref[...].astype(BF16), vc_ref[...].astype(BF16))

    @pl.when(t == pl.num_programs(1) - 1)
    def _():
        step(kn_ref[...], vn_ref[...])
        o = acc_ref[...] / l_ref[...]
        lam = _lambda(lam_ref, lam_init)
        g = g_ref[...]
        for h in range(N_HEADS):
            cols = slice(h * V_DIM, (h + 1) * V_DIM)
            r0 = 2 * h * DEC_SEQ
            y = _sub_ln(o[r0:r0 + DEC_SEQ, cols], o[r0 + DEC_SEQ:r0 + 2 * DEC_SEQ, cols], lam, g, lam_init)
            o_ref[:, cols] = y.astype(BF16)


def _decode_attn(st, layer, q, k_new, v_new, cache_k, cache_v, lam, subln_g, lam_init):
    rows_bd = 2 * N_HEADS * DEC_SEQ
    tok = pl.BlockSpec((DEC_SEQ, ATTN_WIDTH), lambda b, t: (b, 0))
    cache = pl.BlockSpec((None, TK_DEC, ATTN_WIDTH), lambda b, t: (layer * st.batch + b, t, 0))
    return pl.pallas_call(
        functools.partial(_decode_attn_kernel, lam_init=lam_init),
        grid=(st.batch, PAST_LEN // TK_DEC),
        in_specs=[
            tok, cache, cache, tok, tok,
            pl.BlockSpec((None, 4, HEAD_DIM), lambda b, t: (layer, 0, 0)),
            pl.BlockSpec((None, 1, V_DIM), lambda b, t: (layer, 0, 0)),
        ],
        out_specs=tok,
        out_shape=jax.ShapeDtypeStruct((st.rows, ATTN_WIDTH), BF16),
        scratch_shapes=[
            pltpu.VMEM((rows_bd, ATTN_WIDTH), BF16),
            pltpu.VMEM((rows_bd, 1), F32),
            pltpu.VMEM((rows_bd, 1), F32),
            pltpu.VMEM((rows_bd, ATTN_WIDTH), F32),
        ],
        compiler_params=_params("arbitrary", "arbitrary"),
        name="decode_attn",
    )(q, cache_k, cache_v, k_new, v_new, lam, subln_g)


def _out_proj_kernel(x_ref, pool_ref, att_ref, w_ref, gate_ref, g_ref, sc_ref, sh_ref,
                     xo_ref, h_ref):
    mixed = (jnp.dot(pool_ref[...], w_ref[0:POOL_WIDTH, :], preferred_element_type=F32)
             + jnp.dot(att_ref[...], w_ref[POOL_WIDTH:, :], preferred_element_type=F32))
    x = x_ref[...] + gate_ref[...] * mixed
    xo_ref[...] = x
    y = _rms(x, NORM_EPS) * g_ref[...]
    h_ref[...] = (y * (1.0 + sc_ref[...]) + sh_ref[...]).astype(BF16)


def _out_proj(st, layer, x, pool_out, att, mod, w_out, norm_g):
    m = st.rows
    tm = TM
    row = lambda i, j: (i, 0)
    return pl.pallas_call(
        _out_proj_kernel,
        grid=(m // tm, 1),
        in_specs=[
            pl.BlockSpec((tm, D_MODEL), row),
            pl.BlockSpec((tm, POOL_WIDTH), row),
            pl.BlockSpec((tm, ATTN_WIDTH), row),
            pl.BlockSpec((None, D_MODEL, D_MODEL), lambda i, j: (layer, 0, 0),
                         pipeline_mode=pl.Buffered(1)),
            _mod_spec(st, layer, MOD_GATE_A, tm),
            pl.BlockSpec((None, 1, D_MODEL), lambda i, j: (layer, 0, 0)),
            _mod_spec(st, layer, MOD_SCALE_F, tm),
            _mod_spec(st, layer, MOD_SHIFT_F, tm),
        ],
        out_specs=[pl.BlockSpec((tm, D_MODEL), row), pl.BlockSpec((tm, D_MODEL), row)],
        out_shape=[jax.ShapeDtypeStruct((m, D_MODEL), F32), jax.ShapeDtypeStruct((m, D_MODEL), BF16)],
        compiler_params=_params("arbitrary", "arbitrary"),
        name="out_proj",
    )(x, pool_out, att, w_out, mod, norm_g, mod, mod)


def _gate_up_kernel(h_ref, wg_ref, wu_ref, o_ref):
    h = h_ref[...]
    g = jnp.dot(h, wg_ref[...], preferred_element_type=F32)
    u = jnp.dot(h, wu_ref[...], preferred_element_type=F32)
    o_ref[...] = (g * jax.nn.sigmoid(g) * u).astype(BF16)


def _gate_up(st, layer, h, w_gate, w_up):
    m = st.rows
    tm = min(TM_FFN, m)
    w_spec = pl.BlockSpec((None, D_MODEL, TF), lambda i, j: (layer, 0, j))
    return pl.pallas_call(
        _gate_up_kernel,
        grid=(m // tm, D_FF // TF),
        in_specs=[pl.BlockSpec((tm, D_MODEL), lambda i, j: (i, 0)), w_spec, w_spec],
        out_specs=pl.BlockSpec((tm, TF), lambda i, j: (i, j)),
        out_shape=jax.ShapeDtypeStruct((m, D_FF), BF16),
        compiler_params=_params("arbitrary", "arbitrary"),
        name="ffn_gate_up",
    )(h, w_gate, w_up)


def _down_kernel(a_ref, w_ref, x_ref, gate_ref, o_ref):
    y = jnp.dot(a_ref[...], w_ref[...], preferred_element_type=F32)
    o_ref[...] = x_ref[...] + gate_ref[...] * y


def _down(st, layer, act, x, mod, w_down):
    m = st.rows
    tm = TM
    return pl.pallas_call(
        _down_kernel,
        grid=(m // tm, D_MODEL // TN_DOWN),
        in_specs=[
            pl.BlockSpec((tm, D_FF), lambda i, j: (i, 0)),
            pl.BlockSpec((None, D_FF, TN_DOWN), lambda i, j: (layer, 0, j)),
            pl.BlockSpec((tm, TN_DOWN), lambda i, j: (i, j)),
            _mod_spec(st, layer, MOD_GATE_F, tm, TN_DOWN),
        ],
        out_specs=pl.BlockSpec((tm, TN_DOWN), lambda i, j: (i, j)),
        out_shape=jax.ShapeDtypeStruct((m, D_MODEL), F32),
        compiler_params=_params("arbitrary", "arbitrary"),
        name="ffn_down",
    )(act, w_down, x, mod)


def _final_norm_kernel(x_ref, g_ref, o_ref):
    o_ref[...] = _rms(x_ref[...], NORM_EPS) * g_ref[...]


def _final_norm(x, g):
    m = x.shape[0]
    return pl.pallas_call(
        _final_norm_kernel,
        grid=(m // TM,),
        in_specs=[pl.BlockSpec((TM, D_MODEL), lambda i: (i, 0)), pl.BlockSpec((1, D_MODEL), lambda i: (0, 0))],
        out_specs=pl.BlockSpec((TM, D_MODEL), lambda i: (i, 0)),
        out_shape=jax.ShapeDtypeStruct((m, D_MODEL), F32),
        compiler_params=_params("arbitrary"),
        name="final_norm",
    )(x, g)


def _rope_tables(st):
    half = HEAD_DIM // 2
    inv = 1.0 / (ROPE_THETA ** (jnp.arange(half, dtype=F32) * (2.0 / HEAD_DIM)))
    pos = st.pos0 + jnp.arange(st.seq)
    ang = pos.astype(F32)[:, None] * inv[None, :]
    cos, sin = jnp.cos(ang), jnp.sin(ang)
    reps = V7X_LANES // HEAD_DIM
    cos_t = jnp.tile(jnp.concatenate([cos, cos], axis=-1), (1, reps))
    sin_t = jnp.tile(jnp.concatenate([-sin, sin], axis=-1), (1, reps))
    if st.seq < TM:
        cos_t = jnp.tile(cos_t, (TM // st.seq, 1))
        sin_t = jnp.tile(sin_t, (TM // st.seq, 1))
    return cos_t, sin_t


def _run_stream(st, x, mod, hist, caches, p):
    m = st.rows
    x = x.reshape(m, D_MODEL)
    cos_tab, sin_tab = _rope_tables(st)
    ks, vs, us = [], [], []
    for l in range(DEPTH):
        lam_init = 0.8 - 0.6 * math.exp(-0.3 * l)
        u, q, k, kb, v, vb = _in_proj(st, l, x, mod, p["norm_mix_g"], p["w_in"], cos_tab, sin_tab)
        pool_out = _pool(st, l, u, hist[l], p["w_pool"], p["pool_scale"])
        if caches is None:
            att = _prompt_attn(st, l, q, kb, vb, p["lam"], p["subln_g"], lam_init)
        else:
            att = _decode_attn(st, l, q, kb, vb, caches[0], caches[1], p["lam"], p["subln_g"], lam_init)
        x, h = _out_proj(st, l, x, pool_out, att, mod, p["w_out"], p["norm_ffn_g"])
        act = _gate_up(st, l, h, p["w_gate"], p["w_up"])
        x = _down(st, l, act, x, mod, p["w_down"])
        ks.append(k)
        vs.append(v)
        us.append(u)
    y = _final_norm(x, p["final_g"]).reshape(st.batch, st.seq, D_MODEL)
    k_out = jnp.stack(ks).reshape(DEPTH, st.batch, st.seq, 2 * N_HEADS, HEAD_DIM)
    v_out = jnp.stack(vs).reshape(DEPTH, st.batch, st.seq, N_HEADS, V_DIM)
    u_all = jnp.stack(us).reshape(DEPTH, st.batch, st.seq, POOL_WIDTH)
    return y, k_out, v_out, u_all[:, :, st.seq - POOL_HIST:]


def kernel(x_prompt, x_sample, c_prompt, c_sample, cache_k, cache_v, state_pool, w_mod, b_mod,
           norm_mix_g, w_in, w_pool, pool_scale, lam_q1, lam_k1, lam_q2, lam_k2, subln_g, w_out,
           norm_ffn_g, w_gate, w_up, w_down, final_g):
    p = {
        "norm_mix_g": norm_mix_g.reshape(DEPTH, 1, D_MODEL),
        "norm_ffn_g": norm_ffn_g.reshape(DEPTH, 1, D_MODEL),
        "final_g": final_g.reshape(1, D_MODEL),
        "w_in": w_in.astype(BF16),
        "w_pool": w_pool.astype(BF16),
        "pool_scale": pool_scale.reshape(DEPTH, 1, POOL_WIDTH),
        "lam": jnp.stack([lam_q1, lam_k1, lam_q2, lam_k2], axis=1),
        "subln_g": subln_g.reshape(DEPTH, 1, V_DIM),
        "w_out": w_out.astype(BF16),
        "w_gate": w_gate.astype(BF16),
        "w_up": w_up.astype(BF16),
        "w_down": w_down.astype(BF16),
    }
    c_all = jnp.concatenate([c_prompt, c_sample], axis=0)
    c_all = jnp.pad(c_all, ((0, -c_all.shape[0] % 8), (0, 0)))
    mod = _modulation(c_all, w_mod, b_mod)[:, :BATCH + DEC_BATCH]
    mod_prompt = mod[:, :BATCH].reshape(DEPTH * BATCH, 1, N_MOD * D_MODEL)
    mod_sample = jnp.repeat(mod[:, BATCH:], DEC_SEQ, axis=1)

    hist_prompt = jnp.zeros((DEPTH, BATCH, HALO, POOL_WIDTH), F32)
    hist_sample = jnp.pad(state_pool, ((0, 0), (0, 0), (HALO - POOL_HIST, 0), (0, 0)))
    caches = (cache_k.reshape(DEPTH * DEC_BATCH, PAST_LEN, ATTN_WIDTH),
              cache_v.reshape(DEPTH * DEC_BATCH, PAST_LEN, ATTN_WIDTH))

    y_p, k_p, v_p, pool_p = _run_stream(PROMPT, x_prompt, mod_prompt, hist_prompt, None, p)
    y_s, k_s, v_s, pool_s = _run_stream(SAMPLE, x_sample, mod_sample, hist_sample, caches, p)
    return (y_p, y_s, k_p, v_p, pool_p, k_s, v_s, pool_s)
```

```python
import functools
import math
from typing import NamedTuple

import jax
import jax.numpy as jnp
from jax import lax
from jax.experimental import pallas as pl
from jax.experimental.pallas import tpu as pltpu

F32 = jnp.float32
BF16 = jnp.bfloat16

D_MODEL = 2048
BATCH = 4
SEQ = 4096
DEPTH = 4
DEC_BATCH = 32
DEC_SEQ = 16
PAST_LEN = 2048
CHUNK = 64
POOL_WIDTH = D_MODEL // 2
N_POOL_GROUPS = 4
POOL_GROUP = POOL_WIDTH // N_POOL_GROUPS
POOL_WINDOWS = (2, 4, 8, 16)
POOL_HIST = max(POOL_WINDOWS) - 1
HALO = POOL_HIST + 1
ATTN_WIDTH = D_MODEL - POOL_WIDTH
HEAD_DIM = 64
N_HEADS = ATTN_WIDTH // (2 * HEAD_DIM)
V_DIM = 2 * HEAD_DIM
IN_COLS = POOL_WIDTH + 3 * ATTN_WIDTH
D_FF = -(-8 * D_MODEL // (3 * 256)) * 256
ROPE_THETA = 10000.0
NORM_EPS = 1e-6
SUBLN_EPS = 1e-5
N_MOD = 6
MOD_SHIFT_A, MOD_SCALE_A, MOD_GATE_A, MOD_SHIFT_F, MOD_SCALE_F, MOD_GATE_F = range(N_MOD)

V7X_VMEM_BYTES = 64 * 1024 * 1024
V7X_LANES = 128
VMEM_LIMIT = V7X_VMEM_BYTES * 7 // 8

TM = 512
TM_FFN = 1024
TN_MOD = 1024
TF = 512
TN_DOWN = 512
TQ = 512
TK_DEC = 1024


class Stream(NamedTuple):
    batch: int
    seq: int
    pos0: int
    mod_rows: int

    @property
    def rows(self):
        return self.batch * self.seq


PROMPT = Stream(BATCH, SEQ, 0, 1)
SAMPLE = Stream(DEC_BATCH, DEC_SEQ, PAST_LEN, DEC_BATCH * DEC_SEQ)


def _params(*semantics):
    return pltpu.CompilerParams(dimension_semantics=semantics, vmem_limit_bytes=VMEM_LIMIT)


def _rms(x, eps, axis=-1):
    return x * lax.rsqrt(jnp.mean(x * x, axis=axis, keepdims=True) + eps)


def _mod_spec(st, layer, which, tm, tn=D_MODEL):
    per_chunk = D_MODEL // tn
    if st.mod_rows == 1:
        tiles_per_batch = st.seq // tm
        return pl.BlockSpec(
            (None, 1, tn),
            lambda i, j: (layer * st.batch + i // tiles_per_batch, 0, which * per_chunk + j % per_chunk))
    return pl.BlockSpec((None, tm, tn), lambda i, j: (layer, i, which * per_chunk + j % per_chunk))


def _mod_kernel(c_ref, w_ref, b_ref, o_ref):
    c = c_ref[...]
    cs = (c * jax.nn.sigmoid(c)).astype(BF16)
    o_ref[...] = jnp.dot(cs, w_ref[...].astype(BF16), preferred_element_type=F32) + b_ref[...]


def _modulation(c_all, w_mod, b_mod):
    nb = c_all.shape[0]
    n = N_MOD * D_MODEL
    return pl.pallas_call(
        _mod_kernel,
        grid=(DEPTH, n // TN_MOD),
        in_specs=[
            pl.BlockSpec((nb, D_MODEL), lambda l, j: (0, 0)),
            pl.BlockSpec((None, D_MODEL, TN_MOD), lambda l, j: (l, 0, j)),
            pl.BlockSpec((None, 1, TN_MOD), lambda l, j: (l, 0, j)),
        ],
        out_specs=pl.BlockSpec((None, nb, TN_MOD), lambda l, j: (l, 0, j)),
        out_shape=jax.ShapeDtypeStruct((DEPTH, nb, n), F32),
        compiler_params=_params("arbitrary", "arbitrary"),
        name="modulation",
    )(c_all, w_mod, b_mod.reshape(DEPTH, 1, n))


def _rope_store(acc, cos_ref, sin_ref, scale, out_refs):
    tm = acc.shape[0]
    lane = lax.broadcasted_iota(jnp.int32, (tm, V7X_LANES), 1)
    first_half = (lane % HEAD_DIM) < (HEAD_DIM // 2)
    cos = cos_ref[...]
    sin = sin_ref[...]
    for c in range(ATTN_WIDTH // V7X_LANES):
        cols = slice(c * V7X_LANES, (c + 1) * V7X_LANES)
        xs = acc[:, cols]
        partner = jnp.where(first_half,
                            pltpu.roll(xs, V7X_LANES - HEAD_DIM // 2, 1),
                            pltpu.roll(xs, HEAD_DIM // 2, 1))
        y = xs * cos + partner * sin
        if scale != 1.0:
            y = y * scale
        for r in out_refs:
            r[:, cols] = y.astype(r.dtype)


def _in_proj_kernel(x_ref, g_ref, sc_ref, sh_ref, w_ref, cos_ref, sin_ref,
                    u_ref, q_ref, k_ref, kb_ref, v_ref, vb_ref, h_scr):
    j = pl.program_id(1)

    @pl.when(j == 0)
    def _():
        y = _rms(x_ref[...], NORM_EPS) * g_ref[...]
        h_scr[...] = (y * (1.0 + sc_ref[...]) + sh_ref[...]).astype(BF16)

    acc = jnp.dot(h_scr[...], w_ref[...], preferred_element_type=F32)

    @pl.when(j == 0)
    def _():
        u_ref[...] = acc

    @pl.when(j == 1)
    def _():
        _rope_store(acc, cos_ref, sin_ref, HEAD_DIM ** -0.5, (q_ref,))

    @pl.when(j == 2)
    def _():
        _rope_store(acc, cos_ref, sin_ref, 1.0, (k_ref, kb_ref))

    @pl.when(j == 3)
    def _():
        v_ref[...] = acc
        vb_ref[...] = acc.astype(BF16)


def _in_proj(st, layer, x, mod, norm_g, w_in, cos_tab, sin_tab):
    m = st.rows
    tm = TM
    tab_tiles = cos_tab.shape[0] // tm
    row = lambda i, j: (i, 0)
    out_spec = pl.BlockSpec((tm, ATTN_WIDTH), row)
    sds = lambda dt: jax.ShapeDtypeStruct((m, ATTN_WIDTH), dt)
    return pl.pallas_call(
        _in_proj_kernel,
        grid=(m // tm, IN_COLS // ATTN_WIDTH),
        in_specs=[
            pl.BlockSpec((tm, D_MODEL), row),
            pl.BlockSpec((None, 1, D_MODEL), lambda i, j: (layer, 0, 0)),
            _mod_spec(st, layer, MOD_SCALE_A, tm),
            _mod_spec(st, layer, MOD_SHIFT_A, tm),
            pl.BlockSpec((None, D_MODEL, ATTN_WIDTH), lambda i, j: (layer, 0, j)),
            pl.BlockSpec((tm, V7X_LANES), lambda i, j: (i % tab_tiles, 0)),
            pl.BlockSpec((tm, V7X_LANES), lambda i, j: (i % tab_tiles, 0)),
        ],
        out_specs=[out_spec] * 6,
        out_shape=[sds(F32), sds(BF16), sds(F32), sds(BF16), sds(F32), sds(BF16)],
        scratch_shapes=[pltpu.VMEM((tm, D_MODEL), BF16)],
        compiler_params=_params("arbitrary", "arbitrary"),
        name="in_proj",
    )(x, norm_g, mod, mod, w_in, cos_tab, sin_tab)


def _pool_kernel(u_ref, hist_ref, wp_ref, ps_ref, o_ref, ext_ref, *, tm, pos0):
    i = pl.program_id(1)

    @pl.when(i == 0)
    def _():
        ext_ref[0:HALO, :] = hist_ref[...]

    @pl.when(i > 0)
    def _():
        ext_ref[0:HALO, :] = ext_ref[tm:tm + HALO, :]

    ext_ref[HALO:, :] = u_ref[...]
    pos = pos0 + i * tm + lax.broadcasted_iota(jnp.int32, (tm, 1), 0)
    for g, w in enumerate(POOL_WINDOWS):
        cols = slice(g * POOL_GROUP, (g + 1) * POOL_GROUP)
        cur = u_ref[:, cols]
        s = cur
        for back in range(1, w):
            s = s + ext_ref[HALO - back:HALO - back + tm, cols]
        count = jnp.minimum(w, pos + 1).astype(F32)
        d = (s / count - cur).astype(BF16)
        y = jnp.dot(d, wp_ref[g], preferred_element_type=F32) * ps_ref[:, cols]
        o_ref[:, cols] = y.astype(BF16)


def _pool(st, layer, u, hist, w_pool, pool_scale):
    tm = min(TM, st.seq)
    tiles = st.seq // tm
    return pl.pallas_call(
        functools.partial(_pool_kernel, tm=tm, pos0=st.pos0),
        grid=(st.batch, tiles),
        in_specs=[
            pl.BlockSpec((tm, POOL_WIDTH), lambda b, i: (b * tiles + i, 0)),
            pl.BlockSpec((None, HALO, POOL_WIDTH), lambda b, i: (b, 0, 0)),
            pl.BlockSpec((None, N_POOL_GROUPS, POOL_GROUP, POOL_GROUP), lambda b, i: (layer, 0, 0, 0)),
            pl.BlockSpec((None, 1, POOL_WIDTH), lambda b, i: (layer, 0, 0)),
        ],
        out_specs=pl.BlockSpec((tm, POOL_WIDTH), lambda b, i: (b * tiles + i, 0)),
        out_shape=jax.ShapeDtypeStruct((st.rows, POOL_WIDTH), BF16),
        scratch_shapes=[pltpu.VMEM((tm + HALO, POOL_WIDTH), F32)],
        compiler_params=_params("arbitrary", "arbitrary"),
        name="pool_mix",
    )(u, hist, w_pool, pool_scale)


def _lambda(lam_ref, lam_init):
    lam = lam_ref[...]
    s1 = jnp.sum(lam[0:1] * lam[1:2], axis=-1, keepdims=True)
    s2 = jnp.sum(lam[2:3] * lam[3:4], axis=-1, keepdims=True)
    return jnp.exp(s1) - jnp.exp(s2) + lam_init


def _online_softmax_step(s, v, m_ref, l_ref, acc_ref):
    m_prev = m_ref[...]
    m_new = jnp.maximum(m_prev, jnp.max(s, axis=-1, keepdims=True))
    alpha = jnp.exp(m_prev - m_new)
    p = jnp.exp(s - m_new)
    l_ref[...] = alpha * l_ref[...] + jnp.sum(p, axis=-1, keepdims=True)
    acc_ref[...] = alpha * acc_ref[...] + jnp.dot(p.astype(BF16), v, preferred_element_type=F32)
    m_ref[...] = m_new


def _sub_ln(o1, o2, lam, g, lam_init, axis=-1):
    a = o1 - lam * o2
    return (_rms(a, SUBLN_EPS, axis) * g) * (1.0 - lam_init)


_NT = (((1,), (1,)), ((), ()))


def _prompt_attn_kernel(q_ref, k_ref, v_ref, lam_ref, g_ref, o_ref,
                        qz_ref, vt_ref, m_ref, l_ref, acc_ref, *, lam_init):
    i = pl.program_id(2)

    @pl.when(i == 0)
    def _():
        for t in range(vt_ref.shape[0]):
            vt_ref[t] = v_ref[t * TQ:(t + 1) * TQ, :].astype(F32).T.astype(BF16)

    q = q_ref[...]
    lane = lax.broadcasted_iota(jnp.int32, (TQ, V_DIM), 1)
    zero = jnp.zeros_like(q)
    qz_ref[0:TQ, :] = jnp.where(lane < HEAD_DIM, q, zero)
    qz_ref[TQ:, :] = jnp.where(lane >= HEAD_DIM, q, zero)
    m_ref[...] = jnp.full_like(m_ref, -jnp.inf)
    l_ref[...] = jnp.zeros_like(l_ref)
    acc_ref[...] = jnp.zeros_like(acc_ref)

    def tile(t, masked):
        start = pl.multiple_of(t * TQ, TQ)
        s = lax.dot_general(k_ref[pl.ds(start, TQ), :], qz_ref[...], _NT, preferred_element_type=F32)
        if masked:
            key = lax.broadcasted_iota(jnp.int32, (TQ, 2 * TQ), 0)
            col = lax.broadcasted_iota(jnp.int32, (TQ, 2 * TQ), 1)
            query = jnp.where(col >= TQ, col - TQ, col)
            s = jnp.where((key // CHUNK) <= (query // CHUNK), s, -jnp.inf)
        m_prev = m_ref[...]
        m_new = jnp.maximum(m_prev, jnp.max(s, axis=0, keepdims=True))
        alpha = jnp.exp(m_prev - m_new)
        p = jnp.exp(s - m_new)
        l_ref[...] = alpha * l_ref[...] + jnp.sum(p, axis=0, keepdims=True)
        acc_ref[...] = alpha * acc_ref[...] + jnp.dot(vt_ref[t], p.astype(BF16),
                                                      preferred_element_type=F32)
        m_ref[...] = m_new

    def full_tile(t, carry):
        tile(t, False)
        return carry

    lax.fori_loop(0, i, full_tile, 0)
    tile(i, True)

    o = acc_ref[...] / l_ref[...]
    y = _sub_ln(o[:, 0:TQ], o[:, TQ:], _lambda(lam_ref, lam_init), g_ref[...], lam_init, axis=0)
    o_ref[...] = y.T.astype(BF16)


def _prompt_attn(st, layer, q, k, v, lam, subln_g, lam_init):
    nq = st.seq // TQ
    return pl.pallas_call(
        functools.partial(_prompt_attn_kernel, lam_init=lam_init),
        grid=(st.batch, N_HEADS, nq),
        in_specs=[
            pl.BlockSpec((TQ, V_DIM), lambda b, h, i: (b * nq + i, h)),
            pl.BlockSpec((st.seq, V_DIM), lambda b, h, i: (b, h)),
            pl.BlockSpec((st.seq, V_DIM), lambda b, h, i: (b, h)),
            pl.BlockSpec((None, 4, HEAD_DIM), lambda b, h, i: (layer, 0, 0)),
            pl.BlockSpec((None, V_DIM, 1), lambda b, h, i: (layer, 0, 0)),
        ],
        out_specs=pl.BlockSpec((TQ, V_DIM), lambda b, h, i: (b * nq + i, h)),
        out_shape=jax.ShapeDtypeStruct((st.rows, ATTN_WIDTH), BF16),
        scratch_shapes=[
            pltpu.VMEM((2 * TQ, V_DIM), BF16),
            pltpu.VMEM((nq, V_DIM, TQ), BF16),
            pltpu.VMEM((1, 2 * TQ), F32),
            pltpu.VMEM((1, 2 * TQ), F32),
            pltpu.VMEM((V_DIM, 2 * TQ), F32),
        ],
        compiler_params=_params("arbitrary", "arbitrary", "arbitrary"),
        name="prompt_attn",
    )(q, k, v, lam, subln_g.reshape(DEPTH, V_DIM, 1))


def _decode_attn_kernel(q_ref, kc_ref, vc_ref, kn_ref, vn_ref, lam_ref, g_ref, o_ref,
                        qbd_ref, m_ref, l_ref, acc_ref, *, lam_init):
    t = pl.program_id(1)
    n_score_heads = 2 * N_HEADS

    @pl.when(t == 0)
    def _():
        q = q_ref[...]
        col = lax.broadcasted_iota(jnp.int32, (DEC_SEQ, ATTN_WIDTH), 1)
        zero = jnp.zeros_like(q)
        for h in range(n_score_heads):
            qbd_ref[h * DEC_SEQ:(h + 1) * DEC_SEQ, :] = jnp.where(col // HEAD_DIM == h, q, zero)
        m_ref[...] = jnp.full_like(m_ref, -jnp.inf)
        l_ref[...] = jnp.zeros_like(l_ref)
        acc_ref[...] = jnp.zeros_like(acc_ref)

    def step(k, v):
        s = lax.dot_general(qbd_ref[...], k, _NT, preferred_element_type=F32)
        _online_softmax_step(s, v, m_ref, l_ref, acc_ref)

    step(kc_ref[...].astype(BF16), vc_ref[...].astype(BF16))

    @pl.when(t == pl.num_programs(1) - 1)
    def _():
        step(kn_ref[...], vn_ref[...])
        o = acc_ref[...] / l_ref[...]
        lam = _lambda(lam_ref, lam_init)
        g = g_ref[...]
        for h in range(N_HEADS):
            cols = slice(h * V_DIM, (h + 1) * V_DIM)
            r0 = 2 * h * DEC_SEQ
            y = _sub_ln(o[r0:r0 + DEC_SEQ, cols], o[r0 + DEC_SEQ:r0 + 2 * DEC_SEQ, cols], lam, g, lam_init)
            o_ref[:, cols] = y.astype(BF16)


def _decode_attn(st, layer, q, k_new, v_new, cache_k, cache_v, lam, subln_g, lam_init):
    rows_bd = 2 * N_HEADS * DEC_SEQ
    tok = pl.BlockSpec((DEC_SEQ, ATTN_WIDTH), lambda b, t: (b, 0))
    cache = pl.BlockSpec((None, TK_DEC, ATTN_WIDTH), lambda b, t: (layer * st.batch + b, t, 0))
    return pl.pallas_call(
        functools.partial(_decode_attn_kernel, lam_init=lam_init),
        grid=(st.batch, PAST_LEN // TK_DEC),
        in_specs=[
            tok, cache, cache, tok, tok,
            pl.BlockSpec((None, 4, HEAD_DIM), lambda b, t: (layer, 0, 0)),
            pl.BlockSpec((None, 1, V_DIM), lambda b, t: (layer, 0, 0)),
        ],
        out_specs=tok,
        out_shape=jax.ShapeDtypeStruct((st.rows, ATTN_WIDTH), BF16),
        scratch_shapes=[
            pltpu.VMEM((rows_bd, ATTN_WIDTH), BF16),
            pltpu.VMEM((rows_bd, 1), F32),
            pltpu.VMEM((rows_bd, 1), F32),
            pltpu.VMEM((rows_bd, ATTN_WIDTH), F32),
        ],
        compiler_params=_params("arbitrary", "arbitrary"),
        name="decode_attn",
    )(q, cache_k, cache_v, k_new, v_new, lam, subln_g)


def _out_proj_kernel(x_ref, pool_ref, att_ref, w_ref, gate_ref, g_ref, sc_ref, sh_ref,
                     xo_ref, h_ref):
    mixed = (jnp.dot(pool_ref[...], w_ref[0:POOL_WIDTH, :], preferred_element_type=F32)
             + jnp.dot(att_ref[...], w_ref[POOL_WIDTH:, :], preferred_element_type=F32))
    x = x_ref[...] + gate_ref[...] * mixed
    xo_ref[...] = x
    y = _rms(x, NORM_EPS) * g_ref[...]
    h_ref[...] = (y * (1.0 + sc_ref[...]) + sh_ref[...]).astype(BF16)


def _out_proj(st, layer, x, pool_out, att, mod, w_out, norm_g):
    m = st.rows
    tm = TM
    row = lambda i, j: (i, 0)
    return pl.pallas_call(
        _out_proj_kernel,
        grid=(m // tm, 1),
        in_specs=[
            pl.BlockSpec((tm, D_MODEL), row),
            pl.BlockSpec((tm, POOL_WIDTH), row),
            pl.BlockSpec((tm, ATTN_WIDTH), row),
            pl.BlockSpec((None, D_MODEL, D_MODEL), lambda i, j: (layer, 0, 0),
                         pipeline_mode=pl.Buffered(1)),
            _mod_spec(st, layer, MOD_GATE_A, tm),
            pl.BlockSpec((None, 1, D_MODEL), lambda i, j: (layer, 0, 0)),
            _mod_spec(st, layer, MOD_SCALE_F, tm),
            _mod_spec(st, layer, MOD_SHIFT_F, tm),
        ],
        out_specs=[pl.BlockSpec((tm, D_MODEL), row), pl.BlockSpec((tm, D_MODEL), row)],
        out_shape=[jax.ShapeDtypeStruct((m, D_MODEL), F32), jax.ShapeDtypeStruct((m, D_MODEL), BF16)],
        compiler_params=_params("arbitrary", "arbitrary"),
        name="out_proj",
    )(x, pool_out, att, w_out, mod, norm_g, mod, mod)


def _gate_up_kernel(h_ref, wg_ref, wu_ref, o_ref):
    h = h_ref[...]
    g = jnp.dot(h, wg_ref[...], preferred_element_type=F32)
    u = jnp.dot(h, wu_ref[...], preferred_element_type=F32)
    o_ref[...] = (g * jax.nn.sigmoid(g) * u).astype(BF16)


def _gate_up(st, layer, h, w_gate, w_up):
    m = st.rows
    tm = min(TM_FFN, m)
    w_spec = pl.BlockSpec((None, D_MODEL, TF), lambda i, j: (layer, 0, j))
    return pl.pallas_call(
        _gate_up_kernel,
        grid=(m // tm, D_FF // TF),
        in_specs=[pl.BlockSpec((tm, D_MODEL), lambda i, j: (i, 0)), w_spec, w_spec],
        out_specs=pl.BlockSpec((tm, TF), lambda i, j: (i, j)),
        out_shape=jax.ShapeDtypeStruct((m, D_FF), BF16),
        compiler_params=_params("arbitrary", "arbitrary"),
        name="ffn_gate_up",
    )(h, w_gate, w_up)


def _down_kernel(a_ref, w_ref, x_ref, gate_ref, o_ref):
    y = jnp.dot(a_ref[...], w_ref[...], preferred_element_type=F32)
    o_ref[...] = x_ref[...] + gate_ref[...] * y


def _down(st, layer, act, x, mod, w_down):
    m = st.rows
    tm = TM
    return pl.pallas_call(
        _down_kernel,
        grid=(m // tm, D_MODEL // TN_DOWN),
        in_specs=[
            pl.BlockSpec((tm, D_FF), lambda i, j: (i, 0)),
            pl.BlockSpec((None, D_FF, TN_DOWN), lambda i, j: (layer, 0, j)),
            pl.BlockSpec((tm, TN_DOWN), lambda i, j: (i, j)),
            _mod_spec(st, layer, MOD_GATE_F, tm, TN_DOWN),
        ],
        out_specs=pl.BlockSpec((tm, TN_DOWN), lambda i, j: (i, j)),
        out_shape=jax.ShapeDtypeStruct((m, D_MODEL), F32),
        compiler_params=_params("arbitrary", "arbitrary"),
        name="ffn_down",
    )(act, w_down, x, mod)


def _final_norm_kernel(x_ref, g_ref, o_ref):
    o_ref[...] = _rms(x_ref[...], NORM_EPS) * g_ref[...]


def _final_norm(x, g):
    m = x.shape[0]
    return pl.pallas_call(
        _final_norm_kernel,
        grid=(m // TM,),
        in_specs=[pl.BlockSpec((TM, D_MODEL), lambda i: (i, 0)), pl.BlockSpec((1, D_MODEL), lambda i: (0, 0))],
        out_specs=pl.BlockSpec((TM, D_MODEL), lambda i: (i, 0)),
        out_shape=jax.ShapeDtypeStruct((m, D_MODEL), F32),
        compiler_params=_params("arbitrary"),
        name="final_norm",
    )(x, g)


def _rope_tables(st):
    half = HEAD_DIM // 2
    inv = 1.0 / (ROPE_THETA ** (jnp.arange(half, dtype=F32) * (2.0 / HEAD_DIM)))
    pos = st.pos0 + jnp.arange(st.seq)
    ang = pos.astype(F32)[:, None] * inv[None, :]
    cos, sin = jnp.cos(ang), jnp.sin(ang)
    reps = V7X_LANES // HEAD_DIM
    cos_t = jnp.tile(jnp.concatenate([cos, cos], axis=-1), (1, reps))
    sin_t = jnp.tile(jnp.concatenate([-sin, sin], axis=-1), (1, reps))
    if st.seq < TM:
        cos_t = jnp.tile(cos_t, (TM // st.seq, 1))
        sin_t = jnp.tile(sin_t, (TM // st.seq, 1))
    return cos_t, sin_t


def _run_stream(st, x, mod, hist, caches, p):
    m = st.rows
    x = x.reshape(m, D_MODEL)
    cos_tab, sin_tab = _rope_tables(st)
    ks, vs, us = [], [], []
    for l in range(DEPTH):
        lam_init = 0.8 - 0.6 * math.exp(-0.3 * l)
        u, q, k, kb, v, vb = _in_proj(st, l, x, mod, p["norm_mix_g"], p["w_in"], cos_tab, sin_tab)
        pool_out = _pool(st, l, u, hist[l], p["w_pool"], p["pool_scale"])
        if caches is None:
            att = _prompt_attn(st, l, q, kb, vb, p["lam"], p["subln_g"], lam_init)
        else:
            att = _decode_attn(st, l, q, kb, vb, caches[0], caches[1], p["lam"], p["subln_g"], lam_init)
        x, h = _out_proj(st, l, x, pool_out, att, mod, p["w_out"], p["norm_ffn_g"])
        act = _gate_up(st, l, h, p["w_gate"], p["w_up"])
        x = _down(st, l, act, x, mod, p["w_down"])
        ks.append(k)
        vs.append(v)
        us.append(u)
    y = _final_norm(x, p["final_g"]).reshape(st.batch, st.seq, D_MODEL)
    k_out = jnp.stack(ks).reshape(DEPTH, st.batch, st.seq, 2 * N_HEADS, HEAD_DIM)
    v_out = jnp.stack(vs).reshape(DEPTH, st.batch, st.seq, N_HEADS, V_DIM)
    u_all = jnp.stack(us).reshape(DEPTH, st.batch, st.seq, POOL_WIDTH)
    return y, k_out, v_out, u_all[:, :, st.seq - POOL_HIST:]


def kernel(x_prompt, x_sample, c_prompt, c_sample, cache_k, cache_v, state_pool, w_mod, b_mod,
           norm_mix_g, w_in, w_pool, pool_scale, lam_q1, lam_k1, lam_q2, lam_k2, subln_g, w_out,
           norm_ffn_g, w_gate, w_up, w_down, final_g):
    p = {
        "norm_mix_g": norm_mix_g.reshape(DEPTH, 1, D_MODEL),
        "norm_ffn_g": norm_ffn_g.reshape(DEPTH, 1, D_MODEL),
        "final_g": final_g.reshape(1, D_MODEL),
        "w_in": w_in.astype(BF16),
        "w_pool": w_pool.astype(BF16),
        "pool_scale": pool_scale.reshape(DEPTH, 1, POOL_WIDTH),
        "lam": jnp.stack([lam_q1, lam_k1, lam_q2, lam_k2], axis=1),
        "subln_g": subln_g.reshape(DEPTH, 1, V_DIM),
        "w_out": w_out.astype(BF16),
        "w_gate": w_gate.astype(BF16),
        "w_up": w_up.astype(BF16),
        "w_down": w_down.astype(BF16),
    }
    c_all = jnp.concatenate([c_prompt, c_sample], axis=0)
    c_all = jnp.pad(c_all, ((0, -c_all.shape[0] % 8), (0, 0)))
    mod = _modulation(c_all, w_mod, b_mod)[:, :BATCH + DEC_BATCH]
    mod_prompt = mod[:, :BATCH].reshape(DEPTH * BATCH, 1, N_MOD * D_MODEL)
    mod_sample = jnp.repeat(mod[:, BATCH:], DEC_SEQ, axis=1)

    hist_prompt = jnp.zeros((DEPTH, BATCH, HALO, POOL_WIDTH), F32)
    hist_sample = jnp.pad(state_pool, ((0, 0), (0, 0), (HALO - POOL_HIST, 0), (0, 0)))
    caches = (cache_k.reshape(DEPTH * DEC_BATCH, PAST_LEN, ATTN_WIDTH),
              cache_v.reshape(DEPTH * DEC_BATCH, PAST_LEN, ATTN_WIDTH))

    y_p, k_p, v_p, pool_p = _run_stream(PROMPT, x_prompt, mod_prompt, hist_prompt, None, p)
    y_s, k_s, v_s, pool_s = _run_stream(SAMPLE, x_sample, mod_sample, hist_sample, caches, p)
    return (y_p, y_s, k_p, v_p, pool_p, k_s, v_s, pool_s)
```

```python
import functools
import math
from typing import NamedTuple

import jax
import jax.numpy as jnp
from jax import lax
from jax.experimental import pallas as pl
from jax.experimental.pallas import tpu as pltpu

F32 = jnp.float32
BF16 = jnp.bfloat16

D_MODEL = 2048
BATCH = 4
SEQ = 4096
DEPTH = 4
DEC_BATCH = 32
DEC_SEQ = 16
PAST_LEN = 2048
CHUNK = 64
POOL_WIDTH = D_MODEL // 2
N_POOL_GROUPS = 4
POOL_GROUP = POOL_WIDTH // N_POOL_GROUPS
POOL_WINDOWS = (2, 4, 8, 16)
POOL_HIST = max(POOL_WINDOWS) - 1
HALO = POOL_HIST + 1
ATTN_WIDTH = D_MODEL - POOL_WIDTH
HEAD_DIM = 64
N_HEADS = ATTN_WIDTH // (2 * HEAD_DIM)
V_DIM = 2 * HEAD_DIM
IN_COLS = POOL_WIDTH + 3 * ATTN_WIDTH
D_FF = -(-8 * D_MODEL // (3 * 256)) * 256
ROPE_THETA = 10000.0
NORM_EPS = 1e-6
SUBLN_EPS = 1e-5
N_MOD = 6
MOD_SHIFT_A, MOD_SCALE_A, MOD_GATE_A, MOD_SHIFT_F, MOD_SCALE_F, MOD_GATE_F = range(N_MOD)

V7X_VMEM_BYTES = 64 * 1024 * 1024
V7X_LANES = 128
VMEM_LIMIT = V7X_VMEM_BYTES * 7 // 8

TM = 512
TM_FFN = 1024
TN_MOD = 1024
TF = 512
TN_DOWN = 512
TQ = 512
TK_DEC = 512


class Stream(NamedTuple):
    batch: int
    seq: int
    pos0: int
    mod_rows: int

    @property
    def rows(self):
        return self.batch * self.seq


PROMPT = Stream(BATCH, SEQ, 0, 1)
SAMPLE = Stream(DEC_BATCH, DEC_SEQ, PAST_LEN, DEC_BATCH * DEC_SEQ)


def _params(*semantics):
    return pltpu.CompilerParams(dimension_semantics=semantics, vmem_limit_bytes=VMEM_LIMIT)


def _rms(x, eps, axis=-1):
    return x * lax.rsqrt(jnp.mean(x * x, axis=axis, keepdims=True) + eps)


def _mod_spec(st, layer, which, tm, tn=D_MODEL):
    per_chunk = D_MODEL // tn
    if st.mod_rows == 1:
        tiles_per_batch = st.seq // tm
        return pl.BlockSpec(
            (None, 1, tn),
            lambda i, j: (layer * st.batch + i // tiles_per_batch, 0, which * per_chunk + j % per_chunk))
    return pl.BlockSpec((None, tm, tn), lambda i, j: (layer, i, which * per_chunk + j % per_chunk))


def _mod_kernel(c_ref, w_ref, b_ref, o_ref):
    c = c_ref[...]
    cs = (c * jax.nn.sigmoid(c)).astype(BF16)
    o_ref[...] = jnp.dot(cs, w_ref[...].astype(BF16), preferred_element_type=F32) + b_ref[...]


def _modulation(c_all, w_mod, b_mod):
    nb = c_all.shape[0]
    n = N_MOD * D_MODEL
    return pl.pallas_call(
        _mod_kernel,
        grid=(DEPTH, n // TN_MOD),
        in_specs=[
            pl.BlockSpec((nb, D_MODEL), lambda l, j: (0, 0)),
            pl.BlockSpec((None, D_MODEL, TN_MOD), lambda l, j: (l, 0, j)),
            pl.BlockSpec((None, 1, TN_MOD), lambda l, j: (l, 0, j)),
        ],
        out_specs=pl.BlockSpec((None, nb, TN_MOD), lambda l, j: (l, 0, j)),
        out_shape=jax.ShapeDtypeStruct((DEPTH, nb, n), F32),
        compiler_params=_params("arbitrary", "arbitrary"),
        name="modulation",
    )(c_all, w_mod, b_mod.reshape(DEPTH, 1, n))


def _rope_store(acc, cos_ref, sin_ref, scale, out_refs):
    tm = acc.shape[0]
    lane = lax.broadcasted_iota(jnp.int32, (tm, V7X_LANES), 1)
    first_half = (lane % HEAD_DIM) < (HEAD_DIM // 2)
    cos = cos_ref[...]
    sin = sin_ref[...]
    for c in range(ATTN_WIDTH // V7X_LANES):
        cols = slice(c * V7X_LANES, (c + 1) * V7X_LANES)
        xs = acc[:, cols]
        partner = jnp.where(first_half,
                            pltpu.roll(xs, V7X_LANES - HEAD_DIM // 2, 1),
                            pltpu.roll(xs, HEAD_DIM // 2, 1))
        y = xs * cos + partner * sin
        if scale != 1.0:
            y = y * scale
        for r in out_refs:
            r[:, cols] = y.astype(r.dtype)


def _in_proj_kernel(x_ref, g_ref, sc_ref, sh_ref, w_ref, cos_ref, sin_ref,
                    u_ref, q_ref, k_ref, kb_ref, v_ref, vb_ref, h_scr):
    j = pl.program_id(1)

    @pl.when(j == 0)
    def _():
        y = _rms(x_ref[...], NORM_EPS) * g_ref[...]
        h_scr[...] = (y * (1.0 + sc_ref[...]) + sh_ref[...]).astype(BF16)

    acc = jnp.dot(h_scr[...], w_ref[...], preferred_element_type=F32)

    @pl.when(j == 0)
    def _():
        u_ref[...] = acc

    @pl.when(j == 1)
    def _():
        _rope_store(acc, cos_ref, sin_ref, HEAD_DIM ** -0.5, (q_ref,))

    @pl.when(j == 2)
    def _():
        _rope_store(acc, cos_ref, sin_ref, 1.0, (k_ref, kb_ref))

    @pl.when(j == 3)
    def _():
        v_ref[...] = acc
        vb_ref[...] = acc.astype(BF16)


def _in_proj(st, layer, x, mod, norm_g, w_in, cos_tab, sin_tab):
    m = st.rows
    tm = TM
    tab_tiles = cos_tab.shape[0] // tm
    row = lambda i, j: (i, 0)
    out_spec = pl.BlockSpec((tm, ATTN_WIDTH), row)
    sds = lambda dt: jax.ShapeDtypeStruct((m, ATTN_WIDTH), dt)
    return pl.pallas_call(
        _in_proj_kernel,
        grid=(m // tm, IN_COLS // ATTN_WIDTH),
        in_specs=[
            pl.BlockSpec((tm, D_MODEL), row),
            pl.BlockSpec((None, 1, D_MODEL), lambda i, j: (layer, 0, 0)),
            _mod_spec(st, layer, MOD_SCALE_A, tm),
            _mod_spec(st, layer, MOD_SHIFT_A, tm),
            pl.BlockSpec((None, D_MODEL, ATTN_WIDTH), lambda i, j: (layer, 0, j)),
            pl.BlockSpec((tm, V7X_LANES), lambda i, j: (i % tab_tiles, 0)),
            pl.BlockSpec((tm, V7X_LANES), lambda i, j: (i % tab_tiles, 0)),
        ],
        out_specs=[out_spec] * 6,
        out_shape=[sds(F32), sds(BF16), sds(F32), sds(BF16), sds(F32), sds(BF16)],
        scratch_shapes=[pltpu.VMEM((tm, D_MODEL), BF16)],
        compiler_params=_params("arbitrary", "arbitrary"),
        name="in_proj",
    )(x, norm_g, mod, mod, w_in, cos_tab, sin_tab)


def _pool_kernel(u_ref, hist_ref, wp_ref, ps_ref, o_ref, ext_ref, *, tm, pos0):
    i = pl.program_id(1)

    @pl.when(i == 0)
    def _():
        ext_ref[0:HALO, :] = hist_ref[...]

    @pl.when(i > 0)
    def _():
        ext_ref[0:HALO, :] = ext_ref[tm:tm + HALO, :]

    ext_ref[HALO:, :] = u_ref[...]
    pos = pos0 + i * tm + lax.broadcasted_iota(jnp.int32, (tm, 1), 0)
    for g, w in enumerate(POOL_WINDOWS):
        cols = slice(g * POOL_GROUP, (g + 1) * POOL_GROUP)
        cur = u_ref[:, cols]
        s = cur
        for back in range(1, w):
            s = s + ext_ref[HALO - back:HALO - back + tm, cols]
        count = jnp.minimum(w, pos + 1).astype(F32)
        d = (s / count - cur).astype(BF16)
        y = jnp.dot(d, wp_ref[g], preferred_element_type=F32) * ps_ref[:, cols]
        o_ref[:, cols] = y.astype(BF16)


def _pool(st, layer, u, hist, w_pool, pool_scale):
    tm = min(TM, st.seq)
    tiles = st.seq // tm
    return pl.pallas_call(
        functools.partial(_pool_kernel, tm=tm, pos0=st.pos0),
        grid=(st.batch, tiles),
        in_specs=[
            pl.BlockSpec((tm, POOL_WIDTH), lambda b, i: (b * tiles + i, 0)),
            pl.BlockSpec((None, HALO, POOL_WIDTH), lambda b, i: (b, 0, 0)),
            pl.BlockSpec((None, N_POOL_GROUPS, POOL_GROUP, POOL_GROUP), lambda b, i: (layer, 0, 0, 0)),
            pl.BlockSpec((None, 1, POOL_WIDTH), lambda b, i: (layer, 0, 0)),
        ],
        out_specs=pl.BlockSpec((tm, POOL_WIDTH), lambda b, i: (b * tiles + i, 0)),
        out_shape=jax.ShapeDtypeStruct((st.rows, POOL_WIDTH), BF16),
        scratch_shapes=[pltpu.VMEM((tm + HALO, POOL_WIDTH), F32)],
        compiler_params=_params("arbitrary", "arbitrary"),
        name="pool_mix",
    )(u, hist, w_pool, pool_scale)


def _lambda(lam_ref, lam_init):
    lam = lam_ref[...]
    s1 = jnp.sum(lam[0:1] * lam[1:2], axis=-1, keepdims=True)
    s2 = jnp.sum(lam[2:3] * lam[3:4], axis=-1, keepdims=True)
    return jnp.exp(s1) - jnp.exp(s2) + lam_init


def _online_softmax_step(s, v, m_ref, l_ref, acc_ref):
    m_prev = m_ref[...]
    m_new = jnp.maximum(m_prev, jnp.max(s, axis=-1, keepdims=True))
    alpha = jnp.exp(m_prev - m_new)
    p = jnp.exp(s - m_new)
    l_ref[...] = alpha * l_ref[...] + jnp.sum(p, axis=-1, keepdims=True)
    acc_ref[...] = alpha * acc_ref[...] + jnp.dot(p.astype(BF16), v, preferred_element_type=F32)
    m_ref[...] = m_new


def _sub_ln(o1, o2, lam, g, lam_init, axis=-1):
    a = o1 - lam * o2
    return (_rms(a, SUBLN_EPS, axis) * g) * (1.0 - lam_init)


_NT = (((1,), (1,)), ((), ()))


def _prompt_attn_kernel(q_ref, k_ref, v_ref, lam_ref, g_ref, o_ref,
                        qz_ref, vt_ref, m_ref, l_ref, acc_ref, *, lam_init):
    i = pl.program_id(2)

    @pl.when(i == 0)
    def _():
        for t in range(vt_ref.shape[0]):
            vt_ref[t] = v_ref[t * TQ:(t + 1) * TQ, :].astype(F32).T.astype(BF16)

    q = q_ref[...]
    lane = lax.broadcasted_iota(jnp.int32, (TQ, V_DIM), 1)
    zero = jnp.zeros_like(q)
    qz_ref[0:TQ, :] = jnp.where(lane < HEAD_DIM, q, zero)
    qz_ref[TQ:, :] = jnp.where(lane >= HEAD_DIM, q, zero)
    m_ref[...] = jnp.full_like(m_ref, -jnp.inf)
    l_ref[...] = jnp.zeros_like(l_ref)
    acc_ref[...] = jnp.zeros_like(acc_ref)

    def tile(t, masked):
        start = pl.multiple_of(t * TQ, TQ)
        s = lax.dot_general(k_ref[pl.ds(start, TQ), :], qz_ref[...], _NT, preferred_element_type=F32)
        if masked:
            key = lax.broadcasted_iota(jnp.int32, (TQ, 2 * TQ), 0)
            col = lax.broadcasted_iota(jnp.int32, (TQ, 2 * TQ), 1)
            query = jnp.where(col >= TQ, col - TQ, col)
            s = jnp.where((key // CHUNK) <= (query // CHUNK), s, -jnp.inf)
        m_prev = m_ref[...]
        m_new = jnp.maximum(m_prev, jnp.max(s, axis=0, keepdims=True))
        alpha = jnp.exp(m_prev - m_new)
        p = jnp.exp(s - m_new)
        l_ref[...] = alpha * l_ref[...] + jnp.sum(p, axis=0, keepdims=True)
        acc_ref[...] = alpha * acc_ref[...] + jnp.dot(vt_ref[t], p.astype(BF16),
                                                      preferred_element_type=F32)
        m_ref[...] = m_new

    def full_tile(t, carry):
        tile(t, False)
        return carry

    lax.fori_loop(0, i, full_tile, 0)
    tile(i, True)

    o = acc_ref[...] / l_ref[...]
    y = _sub_ln(o[:, 0:TQ], o[:, TQ:], _lambda(lam_ref, lam_init), g_ref[...], lam_init, axis=0)
    o_ref[...] = y.T.astype(BF16)


def _prompt_attn(st, layer, q, k, v, lam, subln_g, lam_init):
    nq = st.seq // TQ
    return pl.pallas_call(
        functools.partial(_prompt_attn_kernel, lam_init=lam_init),
        grid=(st.batch, N_HEADS, nq),
        in_specs=[
            pl.BlockSpec((TQ, V_DIM), lambda b, h, i: (b * nq + i, h)),
            pl.BlockSpec((st.seq, V_DIM), lambda b, h, i: (b, h)),
            pl.BlockSpec((st.seq, V_DIM), lambda b, h, i: (b, h)),
            pl.BlockSpec((None, 4, HEAD_DIM), lambda b, h, i: (layer, 0, 0)),
            pl.BlockSpec((None, V_DIM, 1), lambda b, h, i: (layer, 0, 0)),
        ],
        out_specs=pl.BlockSpec((TQ, V_DIM), lambda b, h, i: (b * nq + i, h)),
        out_shape=jax.ShapeDtypeStruct((st.rows, ATTN_WIDTH), BF16),
        scratch_shapes=[
            pltpu.VMEM((2 * TQ, V_DIM), BF16),
            pltpu.VMEM((nq, V_DIM, TQ), BF16),
            pltpu.VMEM((1, 2 * TQ), F32),
            pltpu.VMEM((1, 2 * TQ), F32),
            pltpu.VMEM((V_DIM, 2 * TQ), F32),
        ],
        compiler_params=_params("arbitrary", "arbitrary", "arbitrary"),
        name="prompt_attn",
    )(q, k, v, lam, subln_g.reshape(DEPTH, V_DIM, 1))


def _decode_attn_kernel(q_ref, kc_ref, vc_ref, kn_ref, vn_ref, lam_ref, g_ref, o_ref,
                        qbd_ref, kbf_ref, vbf_ref, m_ref, l_ref, acc_ref, *, lam_init):
    t = pl.program_id(1)
    n_score_heads = 2 * N_HEADS

    @pl.when(t == 0)
    def _():
        q = q_ref[...]
        col = lax.broadcasted_iota(jnp.int32, (DEC_SEQ, ATTN_WIDTH), 1)
        zero = jnp.zeros_like(q)
        for h in range(n_score_heads):
            qbd_ref[h * DEC_SEQ:(h + 1) * DEC_SEQ, :] = jnp.where(col // HEAD_DIM == h, q, zero)
        m_ref[...] = jnp.full_like(m_ref, -jnp.inf)
        l_ref[...] = jnp.zeros_like(l_ref)
        acc_ref[...] = jnp.zeros_like(acc_ref)

    def step(k, v):
        s = lax.dot_general(qbd_ref[...], k, _NT, preferred_element_type=F32)
        _online_softmax_step(s, v, m_ref, l_ref, acc_ref)

    for h in range(n_score_heads):
        kbf_ref[:, h * HEAD_DIM:(h + 1) * HEAD_DIM] = (
            kc_ref[pl.ds(h, TK_DEC, stride=n_score_heads), :].astype(BF16))
    for h in range(N_HEADS):
        vbf_ref[:, h * V_DIM:(h + 1) * V_DIM] = (
            vc_ref[pl.ds(h, TK_DEC, stride=N_HEADS), :].astype(BF16))
    step(kbf_ref[...], vbf_ref[...])

    @pl.when(t == pl.num_programs(1) - 1)
    def _():
        step(kn_ref[...], vn_ref[...])
        o = acc_ref[...] / l_ref[...]
        lam = _lambda(lam_ref, lam_init)
        g = g_ref[...]
        for h in range(N_HEADS):
            cols = slice(h * V_DIM, (h + 1) * V_DIM)
            r0 = 2 * h * DEC_SEQ
            y = _sub_ln(o[r0:r0 + DEC_SEQ, cols], o[r0 + DEC_SEQ:r0 + 2 * DEC_SEQ, cols], lam, g, lam_init)
            o_ref[:, cols] = y.astype(BF16)


def _decode_attn(st, layer, q, k_new, v_new, cache_k, cache_v, lam, subln_g, lam_init):
    rows_bd = 2 * N_HEADS * DEC_SEQ
    tok = pl.BlockSpec((DEC_SEQ, ATTN_WIDTH), lambda b, t: (b, 0))
    tiles = PAST_LEN // TK_DEC
    cache_idx = lambda b, t: ((layer * st.batch + b) * tiles + t, 0)
    cache_k_spec = pl.BlockSpec((TK_DEC * 2 * N_HEADS, HEAD_DIM), cache_idx)
    cache_v_spec = pl.BlockSpec((TK_DEC * N_HEADS, V_DIM), cache_idx)
    return pl.pallas_call(
        functools.partial(_decode_attn_kernel, lam_init=lam_init),
        grid=(st.batch, tiles),
        in_specs=[
            tok, cache_k_spec, cache_v_spec, tok, tok,
            pl.BlockSpec((None, 4, HEAD_DIM), lambda b, t: (layer, 0, 0)),
            pl.BlockSpec((None, 1, V_DIM), lambda b, t: (layer, 0, 0)),
        ],
        out_specs=tok,
        out_shape=jax.ShapeDtypeStruct((st.rows, ATTN_WIDTH), BF16),
        scratch_shapes=[
            pltpu.VMEM((rows_bd, ATTN_WIDTH), BF16),
            pltpu.VMEM((TK_DEC, ATTN_WIDTH), BF16),
            pltpu.VMEM((TK_DEC, ATTN_WIDTH), BF16),
            pltpu.VMEM((rows_bd, 1), F32),
            pltpu.VMEM((rows_bd, 1), F32),
            pltpu.VMEM((rows_bd, ATTN_WIDTH), F32),
        ],
        compiler_params=_params("arbitrary", "arbitrary"),
        name="decode_attn",
    )(q, cache_k, cache_v, k_new, v_new, lam, subln_g)


def _out_proj_kernel(x_ref, pool_ref, att_ref, w_ref, gate_ref, g_ref, sc_ref, sh_ref,
                     xo_ref, h_ref):
    mixed = (jnp.dot(pool_ref[...], w_ref[0:POOL_WIDTH, :], preferred_element_type=F32)
             + jnp.dot(att_ref[...], w_ref[POOL_WIDTH:, :], preferred_element_type=F32))
    x = x_ref[...] + gate_ref[...] * mixed
    xo_ref[...] = x
    y = _rms(x, NORM_EPS) * g_ref[...]
    h_ref[...] = (y * (1.0 + sc_ref[...]) + sh_ref[...]).astype(BF16)


def _out_proj(st, layer, x, pool_out, att, mod, w_out, norm_g):
    m = st.rows
    tm = TM
    row = lambda i, j: (i, 0)
    return pl.pallas_call(
        _out_proj_kernel,
        grid=(m // tm, 1),
        in_specs=[
            pl.BlockSpec((tm, D_MODEL), row),
            pl.BlockSpec((tm, POOL_WIDTH), row),
            pl.BlockSpec((tm, ATTN_WIDTH), row),
            pl.BlockSpec((None, D_MODEL, D_MODEL), lambda i, j: (layer, 0, 0),
                         pipeline_mode=pl.Buffered(1)),
            _mod_spec(st, layer, MOD_GATE_A, tm),
            pl.BlockSpec((None, 1, D_MODEL), lambda i, j: (layer, 0, 0)),
            _mod_spec(st, layer, MOD_SCALE_F, tm),
            _mod_spec(st, layer, MOD_SHIFT_F, tm),
        ],
        out_specs=[pl.BlockSpec((tm, D_MODEL), row), pl.BlockSpec((tm, D_MODEL), row)],
        out_shape=[jax.ShapeDtypeStruct((m, D_MODEL), F32), jax.ShapeDtypeStruct((m, D_MODEL), BF16)],
        compiler_params=_params("arbitrary", "arbitrary"),
        name="out_proj",
    )(x, pool_out, att, w_out, mod, norm_g, mod, mod)


def _gate_up_kernel(h_ref, wg_ref, wu_ref, o_ref):
    h = h_ref[...]
    g = jnp.dot(h, wg_ref[...], preferred_element_type=F32)
    u = jnp.dot(h, wu_ref[...], preferred_element_type=F32)
    o_ref[...] = (g * jax.nn.sigmoid(g) * u).astype(BF16)


def _gate_up(st, layer, h, w_gate, w_up):
    m = st.rows
    tm = min(TM_FFN, m)
    w_spec = pl.BlockSpec((None, D_MODEL, TF), lambda i, j: (layer, 0, j))
    return pl.pallas_call(
        _gate_up_kernel,
        grid=(m // tm, D_FF // TF),
        in_specs=[pl.BlockSpec((tm, D_MODEL), lambda i, j: (i, 0)), w_spec, w_spec],
        out_specs=pl.BlockSpec((tm, TF), lambda i, j: (i, j)),
        out_shape=jax.ShapeDtypeStruct((m, D_FF), BF16),
        compiler_params=_params("arbitrary", "arbitrary"),
        name="ffn_gate_up",
    )(h, w_gate, w_up)


def _down_kernel(a_ref, w_ref, x_ref, gate_ref, o_ref):
    y = jnp.dot(a_ref[...], w_ref[...], preferred_element_type=F32)
    o_ref[...] = x_ref[...] + gate_ref[...] * y


def _down(st, layer, act, x, mod, w_down):
    m = st.rows
    tm = TM
    return pl.pallas_call(
        _down_kernel,
        grid=(m // tm, D_MODEL // TN_DOWN),
        in_specs=[
            pl.BlockSpec((tm, D_FF), lambda i, j: (i, 0)),
            pl.BlockSpec((None, D_FF, TN_DOWN), lambda i, j: (layer, 0, j)),
            pl.BlockSpec((tm, TN_DOWN), lambda i, j: (i, j)),
            _mod_spec(st, layer, MOD_GATE_F, tm, TN_DOWN),
        ],
        out_specs=pl.BlockSpec((tm, TN_DOWN), lambda i, j: (i, j)),
        out_shape=jax.ShapeDtypeStruct((m, D_MODEL), F32),
        compiler_params=_params("arbitrary", "arbitrary"),
        name="ffn_down",
    )(act, w_down, x, mod)


def _final_norm_kernel(x_ref, g_ref, o_ref):
    o_ref[...] = _rms(x_ref[...], NORM_EPS) * g_ref[...]


def _final_norm(x, g):
    m = x.shape[0]
    return pl.pallas_call(
        _final_norm_kernel,
        grid=(m // TM,),
        in_specs=[pl.BlockSpec((TM, D_MODEL), lambda i: (i, 0)), pl.BlockSpec((1, D_MODEL), lambda i: (0, 0))],
        out_specs=pl.BlockSpec((TM, D_MODEL), lambda i: (i, 0)),
        out_shape=jax.ShapeDtypeStruct((m, D_MODEL), F32),
        compiler_params=_params("arbitrary"),
        name="final_norm",
    )(x, g)


def _rope_tables(st):
    half = HEAD_DIM // 2
    inv = 1.0 / (ROPE_THETA ** (jnp.arange(half, dtype=F32) * (2.0 / HEAD_DIM)))
    pos = st.pos0 + jnp.arange(st.seq)
    ang = pos.astype(F32)[:, None] * inv[None, :]
    cos, sin = jnp.cos(ang), jnp.sin(ang)
    reps = V7X_LANES // HEAD_DIM
    cos_t = jnp.tile(jnp.concatenate([cos, cos], axis=-1), (1, reps))
    sin_t = jnp.tile(jnp.concatenate([-sin, sin], axis=-1), (1, reps))
    if st.seq < TM:
        cos_t = jnp.tile(cos_t, (TM // st.seq, 1))
        sin_t = jnp.tile(sin_t, (TM // st.seq, 1))
    return cos_t, sin_t


def _run_stream(st, x, mod, hist, caches, p):
    m = st.rows
    x = x.reshape(m, D_MODEL)
    cos_tab, sin_tab = _rope_tables(st)
    ks, vs, us = [], [], []
    for l in range(DEPTH):
        lam_init = 0.8 - 0.6 * math.exp(-0.3 * l)
        u, q, k, kb, v, vb = _in_proj(st, l, x, mod, p["norm_mix_g"], p["w_in"], cos_tab, sin_tab)
        pool_out = _pool(st, l, u, hist[l], p["w_pool"], p["pool_scale"])
        if caches is None:
            att = _prompt_attn(st, l, q, kb, vb, p["lam"], p["subln_g"], lam_init)
        else:
            att = _decode_attn(st, l, q, kb, vb, caches[0], caches[1], p["lam"], p["subln_g"], lam_init)
        x, h = _out_proj(st, l, x, pool_out, att, mod, p["w_out"], p["norm_ffn_g"])
        act = _gate_up(st, l, h, p["w_gate"], p["w_up"])
        x = _down(st, l, act, x, mod, p["w_down"])
        ks.append(k)
        vs.append(v)
        us.append(u)
    y = _final_norm(x, p["final_g"]).reshape(st.batch, st.seq, D_MODEL)
    k_out = jnp.stack(ks).reshape(DEPTH, st.batch, st.seq, 2 * N_HEADS, HEAD_DIM)
    v_out = jnp.stack(vs).reshape(DEPTH, st.batch, st.seq, N_HEADS, V_DIM)
    u_all = jnp.stack(us).reshape(DEPTH, st.batch, st.seq, POOL_WIDTH)
    return y, k_out, v_out, u_all[:, :, st.seq - POOL_HIST:]


def kernel(x_prompt, x_sample, c_prompt, c_sample, cache_k, cache_v, state_pool, w_mod, b_mod,
           norm_mix_g, w_in, w_pool, pool_scale, lam_q1, lam_k1, lam_q2, lam_k2, subln_g, w_out,
           norm_ffn_g, w_gate, w_up, w_down, final_g):
    p = {
        "norm_mix_g": norm_mix_g.reshape(DEPTH, 1, D_MODEL),
        "norm_ffn_g": norm_ffn_g.reshape(DEPTH, 1, D_MODEL),
        "final_g": final_g.reshape(1, D_MODEL),
        "w_in": w_in.astype(BF16),
        "w_pool": w_pool.astype(BF16),
        "pool_scale": pool_scale.reshape(DEPTH, 1, POOL_WIDTH),
        "lam": jnp.stack([lam_q1, lam_k1, lam_q2, lam_k2], axis=1),
        "subln_g": subln_g.reshape(DEPTH, 1, V_DIM),
        "w_out": w_out.astype(BF16),
        "w_gate": w_gate.astype(BF16),
        "w_up": w_up.astype(BF16),
        "w_down": w_down.astype(BF16),
    }
    c_all = jnp.concatenate([c_prompt, c_sample], axis=0)
    c_all = jnp.pad(c_all, ((0, -c_all.shape[0] % 8), (0, 0)))
    mod = _modulation(c_all, w_mod, b_mod)[:, :BATCH + DEC_BATCH]
    mod_prompt = mod[:, :BATCH].reshape(DEPTH * BATCH, 1, N_MOD * D_MODEL)
    mod_sample = jnp.repeat(mod[:, BATCH:], DEC_SEQ, axis=1)

    hist_prompt = jnp.zeros((DEPTH, BATCH, HALO, POOL_WIDTH), F32)
    hist_sample = jnp.pad(state_pool, ((0, 0), (0, 0), (HALO - POOL_HIST, 0), (0, 0)))
    caches = (cache_k.reshape(-1, HEAD_DIM), cache_v.reshape(-1, V_DIM))

    y_p, k_p, v_p, pool_p = _run_stream(PROMPT, x_prompt, mod_prompt, hist_prompt, None, p)
    y_s, k_s, v_s, pool_s = _run_stream(SAMPLE, x_sample, mod_sample, hist_sample, caches, p)
    return (y_p, y_s, k_p, v_p, pool_p, k_s, v_s, pool_s)
```

```python
import functools
import math
from typing import NamedTuple

import jax
import jax.numpy as jnp
from jax import lax
from jax.experimental import pallas as pl
from jax.experimental.pallas import tpu as pltpu

F32 = jnp.float32
BF16 = jnp.bfloat16

D_MODEL = 2048
BATCH = 4
SEQ = 4096
DEPTH = 4
DEC_BATCH = 32
DEC_SEQ = 16
PAST_LEN = 2048
CHUNK = 64
POOL_WIDTH = D_MODEL // 2
N_POOL_GROUPS = 4
POOL_GROUP = POOL_WIDTH // N_POOL_GROUPS
POOL_WINDOWS = (2, 4, 8, 16)
POOL_HIST = max(POOL_WINDOWS) - 1
HALO = POOL_HIST + 1
ATTN_WIDTH = D_MODEL - POOL_WIDTH
HEAD_DIM = 64
N_HEADS = ATTN_WIDTH // (2 * HEAD_DIM)
V_DIM = 2 * HEAD_DIM
IN_COLS = POOL_WIDTH + 3 * ATTN_WIDTH
D_FF = -(-8 * D_MODEL // (3 * 256)) * 256
ROPE_THETA = 10000.0
NORM_EPS = 1e-6
SUBLN_EPS = 1e-5
N_MOD = 6
MOD_SHIFT_A, MOD_SCALE_A, MOD_GATE_A, MOD_SHIFT_F, MOD_SCALE_F, MOD_GATE_F = range(N_MOD)

V7X_VMEM_BYTES = 64 * 1024 * 1024
V7X_LANES = 128
VMEM_LIMIT = V7X_VMEM_BYTES * 7 // 8

TM = 512
TM_FFN = 1024
TN_MOD = 1024
TF = 512
TN_DOWN = 512
TQ = 512
TK_DEC = 512


class Stream(NamedTuple):
    batch: int
    seq: int
    pos0: int
    mod_rows: int
    k_transposed: bool

    @property
    def rows(self):
        return self.batch * self.seq


PROMPT = Stream(BATCH, SEQ, 0, 1, True)
SAMPLE = Stream(DEC_BATCH, DEC_SEQ, PAST_LEN, DEC_BATCH * DEC_SEQ, False)


def _params(*semantics):
    return pltpu.CompilerParams(dimension_semantics=semantics, vmem_limit_bytes=VMEM_LIMIT)


def _rms(x, eps, axis=-1):
    return x * lax.rsqrt(jnp.mean(x * x, axis=axis, keepdims=True) + eps)


def _mod_spec(st, layer, which, tm, tn=D_MODEL):
    per_chunk = D_MODEL // tn
    if st.mod_rows == 1:
        tiles_per_batch = st.seq // tm
        return pl.BlockSpec(
            (None, 1, tn),
            lambda i, j: (layer * st.batch + i // tiles_per_batch, 0, which * per_chunk + j % per_chunk))
    return pl.BlockSpec((None, tm, tn), lambda i, j: (layer, i, which * per_chunk + j % per_chunk))


def _mod_kernel(c_ref, w_ref, b_ref, o_ref):
    c = c_ref[...]
    cs = (c * jax.nn.sigmoid(c)).astype(BF16)
    o_ref[...] = jnp.dot(cs, w_ref[...].astype(BF16), preferred_element_type=F32) + b_ref[...]


def _modulation(c_all, w_mod, b_mod):
    nb = c_all.shape[0]
    n = N_MOD * D_MODEL
    return pl.pallas_call(
        _mod_kernel,
        grid=(DEPTH, n // TN_MOD),
        in_specs=[
            pl.BlockSpec((nb, D_MODEL), lambda l, j: (0, 0)),
            pl.BlockSpec((None, D_MODEL, TN_MOD), lambda l, j: (l, 0, j)),
            pl.BlockSpec((None, 1, TN_MOD), lambda l, j: (l, 0, j)),
        ],
        out_specs=pl.BlockSpec((None, nb, TN_MOD), lambda l, j: (l, 0, j)),
        out_shape=jax.ShapeDtypeStruct((DEPTH, nb, n), F32),
        compiler_params=_params("arbitrary", "arbitrary"),
        name="modulation",
    )(c_all, w_mod, b_mod.reshape(DEPTH, 1, n))


def _rope_store(acc, cos_ref, sin_ref, scale, out_refs, transposed_ref=None):
    tm = acc.shape[0]
    lane = lax.broadcasted_iota(jnp.int32, (tm, V7X_LANES), 1)
    first_half = (lane % HEAD_DIM) < (HEAD_DIM // 2)
    cos = cos_ref[...]
    sin = sin_ref[...]
    for c in range(ATTN_WIDTH // V7X_LANES):
        cols = slice(c * V7X_LANES, (c + 1) * V7X_LANES)
        xs = acc[:, cols]
        partner = jnp.where(first_half,
                            pltpu.roll(xs, V7X_LANES - HEAD_DIM // 2, 1),
                            pltpu.roll(xs, HEAD_DIM // 2, 1))
        y = xs * cos + partner * sin
        if scale != 1.0:
            y = y * scale
        for r in out_refs:
            r[:, cols] = y.astype(r.dtype)
        if transposed_ref is not None:
            transposed_ref[cols, :] = y.T


def _in_proj_kernel(x_ref, g_ref, sc_ref, sh_ref, w_ref, cos_ref, sin_ref,
                    u_ref, q_ref, k_ref, kb_ref, v_ref, vb_ref, h_scr, *, k_transposed):
    j = pl.program_id(1)

    @pl.when(j == 0)
    def _():
        y = _rms(x_ref[...], NORM_EPS) * g_ref[...]
        h_scr[...] = (y * (1.0 + sc_ref[...]) + sh_ref[...]).astype(BF16)

    acc = jnp.dot(h_scr[...], w_ref[...], preferred_element_type=F32)

    @pl.when(j == 0)
    def _():
        u_ref[...] = acc

    @pl.when(j == 1)
    def _():
        _rope_store(acc, cos_ref, sin_ref, HEAD_DIM ** -0.5, (q_ref,))

    @pl.when(j == 2)
    def _():
        if k_transposed:
            _rope_store(acc, cos_ref, sin_ref, 1.0, (kb_ref,), k_ref)
        else:
            _rope_store(acc, cos_ref, sin_ref, 1.0, (k_ref, kb_ref))

    @pl.when(j == 3)
    def _():
        v_ref[...] = acc
        vb_ref[...] = acc.astype(BF16)


def _in_proj(st, layer, x, mod, norm_g, w_in, cos_tab, sin_tab):
    m = st.rows
    tm = TM
    tab_tiles = cos_tab.shape[0] // tm
    row = lambda i, j: (i, 0)
    out_spec = pl.BlockSpec((tm, ATTN_WIDTH), row)
    sds = lambda dt: jax.ShapeDtypeStruct((m, ATTN_WIDTH), dt)
    out_specs = [out_spec] * 6
    out_shape = [sds(F32), sds(BF16), sds(F32), sds(BF16), sds(F32), sds(BF16)]
    if st.k_transposed:
        tiles_per_batch = st.seq // tm
        out_specs[2] = pl.BlockSpec((None, ATTN_WIDTH, tm),
                                    lambda i, j: (i // tiles_per_batch, 0, i % tiles_per_batch))
        out_shape[2] = jax.ShapeDtypeStruct((st.batch, ATTN_WIDTH, st.seq), F32)
    return pl.pallas_call(
        functools.partial(_in_proj_kernel, k_transposed=st.k_transposed),
        grid=(m // tm, IN_COLS // ATTN_WIDTH),
        in_specs=[
            pl.BlockSpec((tm, D_MODEL), row),
            pl.BlockSpec((None, 1, D_MODEL), lambda i, j: (layer, 0, 0)),
            _mod_spec(st, layer, MOD_SCALE_A, tm),
            _mod_spec(st, layer, MOD_SHIFT_A, tm),
            pl.BlockSpec((None, D_MODEL, ATTN_WIDTH), lambda i, j: (layer, 0, j)),
            pl.BlockSpec((tm, V7X_LANES), lambda i, j: (i % tab_tiles, 0)),
            pl.BlockSpec((tm, V7X_LANES), lambda i, j: (i % tab_tiles, 0)),
        ],
        out_specs=out_specs,
        out_shape=out_shape,
        scratch_shapes=[pltpu.VMEM((tm, D_MODEL), BF16)],
        compiler_params=_params("arbitrary", "arbitrary"),
        name="in_proj",
    )(x, norm_g, mod, mod, w_in, cos_tab, sin_tab)


def _pool_kernel(u_ref, hist_ref, wp_ref, ps_ref, o_ref, ext_ref, *, tm, pos0):
    i = pl.program_id(1)

    @pl.when(i == 0)
    def _():
        ext_ref[0:HALO, :] = hist_ref[...]

    @pl.when(i > 0)
    def _():
        ext_ref[0:HALO, :] = ext_ref[tm:tm + HALO, :]

    ext_ref[HALO:, :] = u_ref[...]
    pos = pos0 + i * tm + lax.broadcasted_iota(jnp.int32, (tm, 1), 0)
    for g, w in enumerate(POOL_WINDOWS):
        cols = slice(g * POOL_GROUP, (g + 1) * POOL_GROUP)
        cur = u_ref[:, cols]
        s = cur
        for back in range(1, w):
            s = s + ext_ref[HALO - back:HALO - back + tm, cols]
        count = jnp.minimum(w, pos + 1).astype(F32)
        d = (s / count - cur).astype(BF16)
        y = jnp.dot(d, wp_ref[g], preferred_element_type=F32) * ps_ref[:, cols]
        o_ref[:, cols] = y.astype(BF16)


def _pool(st, layer, u, hist, w_pool, pool_scale):
    tm = min(TM, st.seq)
    tiles = st.seq // tm
    return pl.pallas_call(
        functools.partial(_pool_kernel, tm=tm, pos0=st.pos0),
        grid=(st.batch, tiles),
        in_specs=[
            pl.BlockSpec((tm, POOL_WIDTH), lambda b, i: (b * tiles + i, 0)),
            pl.BlockSpec((None, HALO, POOL_WIDTH), lambda b, i: (b, 0, 0)),
            pl.BlockSpec((None, N_POOL_GROUPS, POOL_GROUP, POOL_GROUP), lambda b, i: (layer, 0, 0, 0)),
            pl.BlockSpec((None, 1, POOL_WIDTH), lambda b, i: (layer, 0, 0)),
        ],
        out_specs=pl.BlockSpec((tm, POOL_WIDTH), lambda b, i: (b * tiles + i, 0)),
        out_shape=jax.ShapeDtypeStruct((st.rows, POOL_WIDTH), BF16),
        scratch_shapes=[pltpu.VMEM((tm + HALO, POOL_WIDTH), F32)],
        compiler_params=_params("arbitrary", "arbitrary"),
        name="pool_mix",
    )(u, hist, w_pool, pool_scale)


def _lambda(lam_ref, lam_init):
    lam = lam_ref[...]
    s1 = jnp.sum(lam[0:1] * lam[1:2], axis=-1, keepdims=True)
    s2 = jnp.sum(lam[2:3] * lam[3:4], axis=-1, keepdims=True)
    return jnp.exp(s1) - jnp.exp(s2) + lam_init


def _online_softmax_step(s, v, m_ref, l_ref, acc_ref):
    m_prev = m_ref[...]
    m_new = jnp.maximum(m_prev, jnp.max(s, axis=-1, keepdims=True))
    alpha = jnp.exp(m_prev - m_new)
    p = jnp.exp(s - m_new)
    l_ref[...] = alpha * l_ref[...] + jnp.sum(p, axis=-1, keepdims=True)
    acc_ref[...] = alpha * acc_ref[...] + jnp.dot(p.astype(BF16), v, preferred_element_type=F32)
    m_ref[...] = m_new


def _sub_ln(o1, o2, lam, g, lam_init, axis=-1):
    a = o1 - lam * o2
    return (_rms(a, SUBLN_EPS, axis) * g) * (1.0 - lam_init)


_NT = (((1,), (1,)), ((), ()))


def _prompt_attn_kernel(q_ref, k_ref, v_ref, lam_ref, g_ref, o_ref,
                        qz_ref, vt_ref, m_ref, l_ref, acc_ref, *, lam_init):
    i = pl.program_id(2)

    @pl.when(i == 0)
    def _():
        for t in range(vt_ref.shape[0]):
            vt_ref[t] = v_ref[t * TQ:(t + 1) * TQ, :].astype(F32).T.astype(BF16)

    q = q_ref[...]
    lane = lax.broadcasted_iota(jnp.int32, (TQ, V_DIM), 1)
    zero = jnp.zeros_like(q)
    qz_ref[0:TQ, :] = jnp.where(lane < HEAD_DIM, q, zero)
    qz_ref[TQ:, :] = jnp.where(lane >= HEAD_DIM, q, zero)
    m_ref[...] = jnp.full_like(m_ref, -jnp.inf)
    l_ref[...] = jnp.zeros_like(l_ref)
    acc_ref[...] = jnp.zeros_like(acc_ref)

    def tile(t, masked):
        start = pl.multiple_of(t * TQ, TQ)
        s = lax.dot_general(k_ref[pl.ds(start, TQ), :], qz_ref[...], _NT, preferred_element_type=F32)
        if masked:
            key = lax.broadcasted_iota(jnp.int32, (TQ, 2 * TQ), 0)
            col = lax.broadcasted_iota(jnp.int32, (TQ, 2 * TQ), 1)
            query = jnp.where(col >= TQ, col - TQ, col)
            s = jnp.where((key // CHUNK) <= (query // CHUNK), s, -jnp.inf)
        m_prev = m_ref[...]
        m_new = jnp.maximum(m_prev, jnp.max(s, axis=0, keepdims=True))
        alpha = jnp.exp(m_prev - m_new)
        p = jnp.exp(s - m_new)
        l_ref[...] = alpha * l_ref[...] + jnp.sum(p, axis=0, keepdims=True)
        acc_ref[...] = alpha * acc_ref[...] + jnp.dot(vt_ref[t], p.astype(BF16),
                                                      preferred_element_type=F32)
        m_ref[...] = m_new

    def full_tile(t, carry):
        tile(t, False)
        return carry

    lax.fori_loop(0, i, full_tile, 0)
    tile(i, True)

    o = acc_ref[...] / l_ref[...]
    y = _sub_ln(o[:, 0:TQ], o[:, TQ:], _lambda(lam_ref, lam_init), g_ref[...], lam_init, axis=0)
    o_ref[...] = y.T.astype(BF16)


def _prompt_attn(st, layer, q, k, v, lam, subln_g, lam_init):
    nq = st.seq // TQ
    return pl.pallas_call(
        functools.partial(_prompt_attn_kernel, lam_init=lam_init),
        grid=(st.batch, N_HEADS, nq),
        in_specs=[
            pl.BlockSpec((TQ, V_DIM), lambda b, h, i: (b * nq + i, h)),
            pl.BlockSpec((st.seq, V_DIM), lambda b, h, i: (b, h)),
            pl.BlockSpec((st.seq, V_DIM), lambda b, h, i: (b, h)),
            pl.BlockSpec((None, 4, HEAD_DIM), lambda b, h, i: (layer, 0, 0)),
            pl.BlockSpec((None, V_DIM, 1), lambda b, h, i: (layer, 0, 0)),
        ],
        out_specs=pl.BlockSpec((TQ, V_DIM), lambda b, h, i: (b * nq + i, h)),
        out_shape=jax.ShapeDtypeStruct((st.rows, ATTN_WIDTH), BF16),
        scratch_shapes=[
            pltpu.VMEM((2 * TQ, V_DIM), BF16),
            pltpu.VMEM((nq, V_DIM, TQ), BF16),
            pltpu.VMEM((1, 2 * TQ), F32),
            pltpu.VMEM((1, 2 * TQ), F32),
            pltpu.VMEM((V_DIM, 2 * TQ), F32),
        ],
        compiler_params=_params("arbitrary", "arbitrary", "arbitrary"),
        name="prompt_attn",
    )(q, k, v, lam, subln_g.reshape(DEPTH, V_DIM, 1))


def _decode_attn_kernel(q_ref, kc_ref, vc_ref, kn_ref, vn_ref, lam_ref, g_ref, o_ref,
                        qbd_ref, vbf_ref, m_ref, l_ref, acc_ref, *, lam_init):
    t = pl.program_id(1)
    n_score_heads = 2 * N_HEADS

    @pl.when(t == 0)
    def _():
        q = q_ref[...]
        col = lax.broadcasted_iota(jnp.int32, (DEC_SEQ, ATTN_WIDTH), 1)
        zero = jnp.zeros_like(q)
        for h in range(n_score_heads):
            qbd_ref[h * DEC_SEQ:(h + 1) * DEC_SEQ, :] = jnp.where(col // HEAD_DIM == h, q, zero)
        m_ref[...] = jnp.full_like(m_ref, -jnp.inf)
        l_ref[...] = jnp.zeros_like(l_ref)
        acc_ref[...] = jnp.zeros_like(acc_ref)

    s = jnp.dot(qbd_ref[...], kc_ref[...].astype(BF16), preferred_element_type=F32)
    for h in range(N_HEADS):
        vbf_ref[:, h * V_DIM:(h + 1) * V_DIM] = (
            vc_ref[pl.ds(h, TK_DEC, stride=N_HEADS), :].astype(BF16))
    _online_softmax_step(s, vbf_ref[...], m_ref, l_ref, acc_ref)

    @pl.when(t == pl.num_programs(1) - 1)
    def _():
        s_new = lax.dot_general(qbd_ref[...], kn_ref[...], _NT, preferred_element_type=F32)
        _online_softmax_step(s_new, vn_ref[...], m_ref, l_ref, acc_ref)
        o = acc_ref[...] / l_ref[...]
        lam = _lambda(lam_ref, lam_init)
        g = g_ref[...]
        for h in range(N_HEADS):
            cols = slice(h * V_DIM, (h + 1) * V_DIM)
            r0 = 2 * h * DEC_SEQ
            y = _sub_ln(o[r0:r0 + DEC_SEQ, cols], o[r0 + DEC_SEQ:r0 + 2 * DEC_SEQ, cols], lam, g, lam_init)
            o_ref[:, cols] = y.astype(BF16)


def _decode_attn(st, layer, q, k_new, v_new, cache_k, cache_v, lam, subln_g, lam_init):
    rows_bd = 2 * N_HEADS * DEC_SEQ
    tok = pl.BlockSpec((DEC_SEQ, ATTN_WIDTH), lambda b, t: (b, 0))
    tiles = PAST_LEN // TK_DEC
    cache_k_spec = pl.BlockSpec((None, ATTN_WIDTH, TK_DEC), lambda b, t: (layer * st.batch + b, 0, t))
    cache_v_spec = pl.BlockSpec((TK_DEC * N_HEADS, V_DIM),
                                lambda b, t: ((layer * st.batch + b) * tiles + t, 0))
    return pl.pallas_call(
        functools.partial(_decode_attn_kernel, lam_init=lam_init),
        grid=(st.batch, tiles),
        in_specs=[
            tok, cache_k_spec, cache_v_spec, tok, tok,
            pl.BlockSpec((None, 4, HEAD_DIM), lambda b, t: (layer, 0, 0)),
            pl.BlockSpec((None, 1, V_DIM), lambda b, t: (layer, 0, 0)),
        ],
        out_specs=tok,
        out_shape=jax.ShapeDtypeStruct((st.rows, ATTN_WIDTH), BF16),
        scratch_shapes=[
            pltpu.VMEM((rows_bd, ATTN_WIDTH), BF16),
            pltpu.VMEM((TK_DEC, ATTN_WIDTH), BF16),
            pltpu.VMEM((rows_bd, 1), F32),
            pltpu.VMEM((rows_bd, 1), F32),
            pltpu.VMEM((rows_bd, ATTN_WIDTH), F32),
        ],
        compiler_params=_params("arbitrary", "arbitrary"),
        name="decode_attn",
    )(q, cache_k, cache_v, k_new, v_new, lam, subln_g)


def _out_proj_kernel(x_ref, pool_ref, att_ref, w_ref, gate_ref, g_ref, sc_ref, sh_ref,
                     xo_ref, h_ref):
    mixed = (jnp.dot(pool_ref[...], w_ref[0:POOL_WIDTH, :], preferred_element_type=F32)
             + jnp.dot(att_ref[...], w_ref[POOL_WIDTH:, :], preferred_element_type=F32))
    x = x_ref[...] + gate_ref[...] * mixed
    xo_ref[...] = x
    y = _rms(x, NORM_EPS) * g_ref[...]
    h_ref[...] = (y * (1.0 + sc_ref[...]) + sh_ref[...]).astype(BF16)


def _out_proj(st, layer, x, pool_out, att, mod, w_out, norm_g):
    m = st.rows
    tm = TM
    row = lambda i, j: (i, 0)
    return pl.pallas_call(
        _out_proj_kernel,
        grid=(m // tm, 1),
        in_specs=[
            pl.BlockSpec((tm, D_MODEL), row),
            pl.BlockSpec((tm, POOL_WIDTH), row),
            pl.BlockSpec((tm, ATTN_WIDTH), row),
            pl.BlockSpec((None, D_MODEL, D_MODEL), lambda i, j: (layer, 0, 0),
                         pipeline_mode=pl.Buffered(1)),
            _mod_spec(st, layer, MOD_GATE_A, tm),
            pl.BlockSpec((None, 1, D_MODEL), lambda i, j: (layer, 0, 0)),
            _mod_spec(st, layer, MOD_SCALE_F, tm),
            _mod_spec(st, layer, MOD_SHIFT_F, tm),
        ],
        out_specs=[pl.BlockSpec((tm, D_MODEL), row), pl.BlockSpec((tm, D_MODEL), row)],
        out_shape=[jax.ShapeDtypeStruct((m, D_MODEL), F32), jax.ShapeDtypeStruct((m, D_MODEL), BF16)],
        compiler_params=_params("arbitrary", "arbitrary"),
        name="out_proj",
    )(x, pool_out, att, w_out, mod, norm_g, mod, mod)


def _gate_up_kernel(h_ref, wg_ref, wu_ref, o_ref):
    h = h_ref[...]
    g = jnp.dot(h, wg_ref[...], preferred_element_type=F32)
    u = jnp.dot(h, wu_ref[...], preferred_element_type=F32)
    o_ref[...] = (g * jax.nn.sigmoid(g) * u).astype(BF16)


def _gate_up(st, layer, h, w_gate, w_up):
    m = st.rows
    tm = min(TM_FFN, m)
    w_spec = pl.BlockSpec((None, D_MODEL, TF), lambda i, j: (layer, 0, j))
    return pl.pallas_call(
        _gate_up_kernel,
        grid=(m // tm, D_FF // TF),
        in_specs=[pl.BlockSpec((tm, D_MODEL), lambda i, j: (i, 0)), w_spec, w_spec],
        out_specs=pl.BlockSpec((tm, TF), lambda i, j: (i, j)),
        out_shape=jax.ShapeDtypeStruct((m, D_FF), BF16),
        compiler_params=_params("arbitrary", "arbitrary"),
        name="ffn_gate_up",
    )(h, w_gate, w_up)


def _down_kernel(a_ref, w_ref, x_ref, gate_ref, o_ref):
    y = jnp.dot(a_ref[...], w_ref[...], preferred_element_type=F32)
    o_ref[...] = x_ref[...] + gate_ref[...] * y


def _down(st, layer, act, x, mod, w_down):
    m = st.rows
    tm = TM
    return pl.pallas_call(
        _down_kernel,
        grid=(m // tm, D_MODEL // TN_DOWN),
        in_specs=[
            pl.BlockSpec((tm, D_FF), lambda i, j: (i, 0)),
            pl.BlockSpec((None, D_FF, TN_DOWN), lambda i, j: (layer, 0, j)),
            pl.BlockSpec((tm, TN_DOWN), lambda i, j: (i, j)),
            _mod_spec(st, layer, MOD_GATE_F, tm, TN_DOWN),
        ],
        out_specs=pl.BlockSpec((tm, TN_DOWN), lambda i, j: (i, j)),
        out_shape=jax.ShapeDtypeStruct((m, D_MODEL), F32),
        compiler_params=_params("arbitrary", "arbitrary"),
        name="ffn_down",
    )(act, w_down, x, mod)


def _final_norm_kernel(x_ref, g_ref, o_ref):
    o_ref[...] = _rms(x_ref[...], NORM_EPS) * g_ref[...]


def _final_norm(x, g):
    m = x.shape[0]
    return pl.pallas_call(
        _final_norm_kernel,
        grid=(m // TM,),
        in_specs=[pl.BlockSpec((TM, D_MODEL), lambda i: (i, 0)), pl.BlockSpec((1, D_MODEL), lambda i: (0, 0))],
        out_specs=pl.BlockSpec((TM, D_MODEL), lambda i: (i, 0)),
        out_shape=jax.ShapeDtypeStruct((m, D_MODEL), F32),
        compiler_params=_params("arbitrary"),
        name="final_norm",
    )(x, g)


def _rope_tables(st):
    half = HEAD_DIM // 2
    inv = 1.0 / (ROPE_THETA ** (jnp.arange(half, dtype=F32) * (2.0 / HEAD_DIM)))
    pos = st.pos0 + jnp.arange(st.seq)
    ang = pos.astype(F32)[:, None] * inv[None, :]
    cos, sin = jnp.cos(ang), jnp.sin(ang)
    reps = V7X_LANES // HEAD_DIM
    cos_t = jnp.tile(jnp.concatenate([cos, cos], axis=-1), (1, reps))
    sin_t = jnp.tile(jnp.concatenate([-sin, sin], axis=-1), (1, reps))
    if st.seq < TM:
        cos_t = jnp.tile(cos_t, (TM // st.seq, 1))
        sin_t = jnp.tile(sin_t, (TM // st.seq, 1))
    return cos_t, sin_t


def _run_stream(st, x, mod, hist, caches, p):
    m = st.rows
    x = x.reshape(m, D_MODEL)
    cos_tab, sin_tab = _rope_tables(st)
    ks, vs, us = [], [], []
    for l in range(DEPTH):
        lam_init = 0.8 - 0.6 * math.exp(-0.3 * l)
        u, q, k, kb, v, vb = _in_proj(st, l, x, mod, p["norm_mix_g"], p["w_in"], cos_tab, sin_tab)
        pool_out = _pool(st, l, u, hist[l], p["w_pool"], p["pool_scale"])
        if caches is None:
            att = _prompt_attn(st, l, q, kb, vb, p["lam"], p["subln_g"], lam_init)
        else:
            att = _decode_attn(st, l, q, kb, vb, caches[0], caches[1], p["lam"], p["subln_g"], lam_init)
        x, h = _out_proj(st, l, x, pool_out, att, mod, p["w_out"], p["norm_ffn_g"])
        act = _gate_up(st, l, h, p["w_gate"], p["w_up"])
        x = _down(st, l, act, x, mod, p["w_down"])
        ks.append(k)
        vs.append(v)
        us.append(u)
    y = _final_norm(x, p["final_g"]).reshape(st.batch, st.seq, D_MODEL)
    if st.k_transposed:
        k_out = jnp.stack(ks).reshape(DEPTH, st.batch, 2 * N_HEADS, HEAD_DIM, st.seq)
        k_out = jnp.transpose(k_out, (0, 1, 4, 2, 3))
    else:
        k_out = jnp.stack(ks).reshape(DEPTH, st.batch, st.seq, 2 * N_HEADS, HEAD_DIM)
    v_out = jnp.stack(vs).reshape(DEPTH, st.batch, st.seq, N_HEADS, V_DIM)
    u_all = jnp.stack(us).reshape(DEPTH, st.batch, st.seq, POOL_WIDTH)
    return y, k_out, v_out, u_all[:, :, st.seq - POOL_HIST:]


def kernel(x_prompt, x_sample, c_prompt, c_sample, cache_k, cache_v, state_pool, w_mod, b_mod,
           norm_mix_g, w_in, w_pool, pool_scale, lam_q1, lam_k1, lam_q2, lam_k2, subln_g, w_out,
           norm_ffn_g, w_gate, w_up, w_down, final_g):
    p = {
        "norm_mix_g": norm_mix_g.reshape(DEPTH, 1, D_MODEL),
        "norm_ffn_g": norm_ffn_g.reshape(DEPTH, 1, D_MODEL),
        "final_g": final_g.reshape(1, D_MODEL),
        "w_in": w_in.astype(BF16),
        "w_pool": w_pool.astype(BF16),
        "pool_scale": pool_scale.reshape(DEPTH, 1, POOL_WIDTH),
        "lam": jnp.stack([lam_q1, lam_k1, lam_q2, lam_k2], axis=1),
        "subln_g": subln_g.reshape(DEPTH, 1, V_DIM),
        "w_out": w_out.astype(BF16),
        "w_gate": w_gate.astype(BF16),
        "w_up": w_up.astype(BF16),
        "w_down": w_down.astype(BF16),
    }
    c_all = jnp.concatenate([c_prompt, c_sample], axis=0)
    c_all = jnp.pad(c_all, ((0, -c_all.shape[0] % 8), (0, 0)))
    mod = _modulation(c_all, w_mod, b_mod)[:, :BATCH + DEC_BATCH]
    mod_prompt = mod[:, :BATCH].reshape(DEPTH * BATCH, 1, N_MOD * D_MODEL)
    mod_sample = jnp.repeat(mod[:, BATCH:], DEC_SEQ, axis=1)

    hist_prompt = jnp.zeros((DEPTH, BATCH, HALO, POOL_WIDTH), F32)
    hist_sample = jnp.pad(state_pool, ((0, 0), (0, 0), (HALO - POOL_HIST, 0), (0, 0)))
    caches = (jnp.transpose(cache_k, (0, 1, 3, 4, 2)).reshape(DEPTH * DEC_BATCH, ATTN_WIDTH, PAST_LEN),
              cache_v.reshape(-1, V_DIM))

    y_p, k_p, v_p, pool_p = _run_stream(PROMPT, x_prompt, mod_prompt, hist_prompt, None, p)
    y_s, k_s, v_s, pool_s = _run_stream(SAMPLE, x_sample, mod_sample, hist_sample, caches, p)
    return (y_p, y_s, k_p, v_p, pool_p, k_s, v_s, pool_s)
```

```python
import functools
import math
from typing import NamedTuple

import jax
import jax.numpy as jnp
from jax import lax
from jax.experimental import pallas as pl
from jax.experimental.pallas import tpu as pltpu

F32 = jnp.float32
BF16 = jnp.bfloat16

D_MODEL = 2048
BATCH = 4
SEQ = 4096
DEPTH = 4
DEC_BATCH = 32
DEC_SEQ = 16
PAST_LEN = 2048
CHUNK = 64
POOL_WIDTH = D_MODEL // 2
N_POOL_GROUPS = 4
POOL_GROUP = POOL_WIDTH // N_POOL_GROUPS
POOL_WINDOWS = (2, 4, 8, 16)
POOL_HIST = max(POOL_WINDOWS) - 1
HALO = POOL_HIST + 1
ATTN_WIDTH = D_MODEL - POOL_WIDTH
HEAD_DIM = 64
N_HEADS = ATTN_WIDTH // (2 * HEAD_DIM)
V_DIM = 2 * HEAD_DIM
IN_COLS = POOL_WIDTH + 3 * ATTN_WIDTH
D_FF = -(-8 * D_MODEL // (3 * 256)) * 256
ROPE_THETA = 10000.0
NORM_EPS = 1e-6
SUBLN_EPS = 1e-5
N_MOD = 6
MOD_SHIFT_A, MOD_SCALE_A, MOD_GATE_A, MOD_SHIFT_F, MOD_SCALE_F, MOD_GATE_F = range(N_MOD)

V7X_VMEM_BYTES = 64 * 1024 * 1024
V7X_LANES = 128
VMEM_LIMIT = V7X_VMEM_BYTES * 7 // 8

TM = 512
TM_FFN = 1024
TN_MOD = 1024
TF = 512
TN_DOWN = 512
TQ = 512
TQ_COLS = 512
SCORE_SCALE_LOG2 = HEAD_DIM ** -0.5 * math.log2(math.e)
TK_DEC = 512


class Stream(NamedTuple):
    batch: int
    seq: int
    pos0: int
    mod_rows: int
    k_transposed: bool

    @property
    def rows(self):
        return self.batch * self.seq


PROMPT = Stream(BATCH, SEQ, 0, 1, True)
SAMPLE = Stream(DEC_BATCH, DEC_SEQ, PAST_LEN, DEC_BATCH * DEC_SEQ, False)


def _params(*semantics, flags=None):
    return pltpu.CompilerParams(dimension_semantics=semantics, vmem_limit_bytes=VMEM_LIMIT, flags=flags)


def _rms(x, eps, axis=-1):
    return x * lax.rsqrt(jnp.mean(x * x, axis=axis, keepdims=True) + eps)


def _mod_spec(st, layer, which, tm, tn=D_MODEL):
    per_chunk = D_MODEL // tn
    if st.mod_rows == 1:
        tiles_per_batch = st.seq // tm
        return pl.BlockSpec(
            (None, 1, tn),
            lambda i, j: (layer * st.batch + i // tiles_per_batch, 0, which * per_chunk + j % per_chunk))
    return pl.BlockSpec((None, tm, tn), lambda i, j: (layer, i, which * per_chunk + j % per_chunk))


def _mod_kernel(c_ref, w_ref, b_ref, o_ref):
    c = c_ref[...]
    cs = (c * jax.nn.sigmoid(c)).astype(BF16)
    o_ref[...] = jnp.dot(cs, w_ref[...].astype(BF16), preferred_element_type=F32) + b_ref[...]


def _modulation(c_all, w_mod, b_mod):
    nb = c_all.shape[0]
    n = N_MOD * D_MODEL
    return pl.pallas_call(
        _mod_kernel,
        grid=(DEPTH, n // TN_MOD),
        in_specs=[
            pl.BlockSpec((nb, D_MODEL), lambda l, j: (0, 0)),
            pl.BlockSpec((None, D_MODEL, TN_MOD), lambda l, j: (l, 0, j)),
            pl.BlockSpec((None, 1, TN_MOD), lambda l, j: (l, 0, j)),
        ],
        out_specs=pl.BlockSpec((None, nb, TN_MOD), lambda l, j: (l, 0, j)),
        out_shape=jax.ShapeDtypeStruct((DEPTH, nb, n), F32),
        compiler_params=_params("arbitrary", "arbitrary"),
        name="modulation",
    )(c_all, w_mod, b_mod.reshape(DEPTH, 1, n))


def _rope_store(acc, cos_ref, sin_ref, scale, out_refs, transposed_ref=None):
    tm = acc.shape[0]
    lane = lax.broadcasted_iota(jnp.int32, (tm, V7X_LANES), 1)
    first_half = (lane % HEAD_DIM) < (HEAD_DIM // 2)
    cos = cos_ref[...]
    sin = sin_ref[...]
    for c in range(ATTN_WIDTH // V7X_LANES):
        cols = slice(c * V7X_LANES, (c + 1) * V7X_LANES)
        xs = acc[:, cols]
        partner = jnp.where(first_half,
                            pltpu.roll(xs, V7X_LANES - HEAD_DIM // 2, 1),
                            pltpu.roll(xs, HEAD_DIM // 2, 1))
        y = xs * cos + partner * sin
        if scale != 1.0:
            y = y * scale
        for r in out_refs:
            r[:, cols] = y.astype(r.dtype)
        if transposed_ref is not None:
            transposed_ref[cols, :] = y.T


def _in_proj_kernel(x_ref, g_ref, sc_ref, sh_ref, w_ref, cos_ref, sin_ref,
                    u_ref, q_ref, k_ref, kb_ref, v_ref, vb_ref, h_scr, *, k_transposed):
    j = pl.program_id(1)

    @pl.when(j == 0)
    def _():
        y = _rms(x_ref[...], NORM_EPS) * g_ref[...]
        h_scr[...] = (y * (1.0 + sc_ref[...]) + sh_ref[...]).astype(BF16)

    acc = jnp.dot(h_scr[...], w_ref[...], preferred_element_type=F32)

    @pl.when(j == 0)
    def _():
        u_ref[...] = acc

    @pl.when(j == 1)
    def _():
        _rope_store(acc, cos_ref, sin_ref, SCORE_SCALE_LOG2, (q_ref,))

    @pl.when(j == 2)
    def _():
        if k_transposed:
            _rope_store(acc, cos_ref, sin_ref, 1.0, (kb_ref,), k_ref)
        else:
            _rope_store(acc, cos_ref, sin_ref, 1.0, (k_ref, kb_ref))

    @pl.when(j == 3)
    def _():
        v_ref[...] = acc
        vb_ref[...] = acc.astype(BF16)


def _in_proj(st, layer, x, mod, norm_g, w_in, cos_tab, sin_tab):
    m = st.rows
    tm = TM
    tab_tiles = cos_tab.shape[0] // tm
    row = lambda i, j: (i, 0)
    out_spec = pl.BlockSpec((tm, ATTN_WIDTH), row)
    sds = lambda dt: jax.ShapeDtypeStruct((m, ATTN_WIDTH), dt)
    out_specs = [out_spec] * 6
    out_shape = [sds(F32), sds(BF16), sds(F32), sds(BF16), sds(F32), sds(BF16)]
    if st.k_transposed:
        tiles_per_batch = st.seq // tm
        out_specs[2] = pl.BlockSpec((None, ATTN_WIDTH, tm),
                                    lambda i, j: (i // tiles_per_batch, 0, i % tiles_per_batch))
        out_shape[2] = jax.ShapeDtypeStruct((st.batch, ATTN_WIDTH, st.seq), F32)
    return pl.pallas_call(
        functools.partial(_in_proj_kernel, k_transposed=st.k_transposed),
        grid=(m // tm, IN_COLS // ATTN_WIDTH),
        in_specs=[
            pl.BlockSpec((tm, D_MODEL), row),
            pl.BlockSpec((None, 1, D_MODEL), lambda i, j: (layer, 0, 0)),
            _mod_spec(st, layer, MOD_SCALE_A, tm),
            _mod_spec(st, layer, MOD_SHIFT_A, tm),
            pl.BlockSpec((None, D_MODEL, ATTN_WIDTH), lambda i, j: (layer, 0, j)),
            pl.BlockSpec((tm, V7X_LANES), lambda i, j: (i % tab_tiles, 0)),
            pl.BlockSpec((tm, V7X_LANES), lambda i, j: (i % tab_tiles, 0)),
        ],
        out_specs=out_specs,
        out_shape=out_shape,
        scratch_shapes=[pltpu.VMEM((tm, D_MODEL), BF16)],
        compiler_params=_params("arbitrary", "arbitrary"),
        name="in_proj",
    )(x, norm_g, mod, mod, w_in, cos_tab, sin_tab)


def _pool_kernel(u_ref, hist_ref, wp_ref, ps_ref, o_ref, ext_ref, *, tm, pos0):
    i = pl.program_id(1)

    @pl.when(i == 0)
    def _():
        ext_ref[0:HALO, :] = hist_ref[...]

    @pl.when(i > 0)
    def _():
        ext_ref[0:HALO, :] = ext_ref[tm:tm + HALO, :]

    ext_ref[HALO:, :] = u_ref[...]
    pos = pos0 + i * tm + lax.broadcasted_iota(jnp.int32, (tm, 1), 0)
    for g, w in enumerate(POOL_WINDOWS):
        cols = slice(g * POOL_GROUP, (g + 1) * POOL_GROUP)
        cur = u_ref[:, cols]
        s = cur
        for back in range(1, w):
            s = s + ext_ref[HALO - back:HALO - back + tm, cols]
        count = jnp.minimum(w, pos + 1).astype(F32)
        d = (s / count - cur).astype(BF16)
        y = jnp.dot(d, wp_ref[g], preferred_element_type=F32) * ps_ref[:, cols]
        o_ref[:, cols] = y.astype(BF16)


def _pool(st, layer, u, hist, w_pool, pool_scale):
    tm = min(TM, st.seq)
    tiles = st.seq // tm
    return pl.pallas_call(
        functools.partial(_pool_kernel, tm=tm, pos0=st.pos0),
        grid=(st.batch, tiles),
        in_specs=[
            pl.BlockSpec((tm, POOL_WIDTH), lambda b, i: (b * tiles + i, 0)),
            pl.BlockSpec((None, HALO, POOL_WIDTH), lambda b, i: (b, 0, 0)),
            pl.BlockSpec((None, N_POOL_GROUPS, POOL_GROUP, POOL_GROUP), lambda b, i: (layer, 0, 0, 0)),
            pl.BlockSpec((None, 1, POOL_WIDTH), lambda b, i: (layer, 0, 0)),
        ],
        out_specs=pl.BlockSpec((tm, POOL_WIDTH), lambda b, i: (b * tiles + i, 0)),
        out_shape=jax.ShapeDtypeStruct((st.rows, POOL_WIDTH), BF16),
        scratch_shapes=[pltpu.VMEM((tm + HALO, POOL_WIDTH), F32)],
        compiler_params=_params("arbitrary", "arbitrary"),
        name="pool_mix",
    )(u, hist, w_pool, pool_scale)


def _lambda(lam_ref, lam_init):
    lam = lam_ref[...]
    s1 = jnp.sum(lam[0:1] * lam[1:2], axis=-1, keepdims=True)
    s2 = jnp.sum(lam[2:3] * lam[3:4], axis=-1, keepdims=True)
    return jnp.exp(s1) - jnp.exp(s2) + lam_init


def _online_softmax_step(s, v, m_ref, l_ref, acc_ref):
    m_prev = m_ref[...]
    m_new = jnp.maximum(m_prev, jnp.max(s, axis=-1, keepdims=True))
    alpha = jnp.exp2(m_prev - m_new)
    p = jnp.exp2(s - m_new)
    l_ref[...] = alpha * l_ref[...] + jnp.sum(p, axis=-1, keepdims=True)
    acc_ref[...] = alpha * acc_ref[...] + jnp.dot(p.astype(BF16), v, preferred_element_type=F32)
    m_ref[...] = m_new


def _sub_ln(o1, o2, lam, g, lam_init, axis=-1):
    a = o1 - lam * o2
    return (_rms(a, SUBLN_EPS, axis) * g) * (1.0 - lam_init)


_NT = (((1,), (1,)), ((), ()))


def _prompt_attn_kernel(q_ref, k_ref, v_ref, lam_ref, g_ref, o_ref,
                        qz_ref, vt_ref, sa_ref, sb_ref, m_ref, l_ref, acc_ref, *, lam_init):
    i = pl.program_id(2)

    @pl.when(i == 0)
    def _():
        for t in range(vt_ref.shape[0]):
            vt_ref[t] = v_ref[t * TQ:(t + 1) * TQ, :].astype(F32).T.astype(BF16)

    q = q_ref[...]
    lane = lax.broadcasted_iota(jnp.int32, (TQ, V_DIM), 1)
    zero = jnp.zeros_like(q)
    qz_ref[0:TQ, :] = jnp.where(lane < HEAD_DIM, q, zero)
    qz_ref[TQ:, :] = jnp.where(lane >= HEAD_DIM, q, zero)
    m_ref[...] = jnp.full_like(m_ref, -jnp.inf)
    l_ref[...] = jnp.zeros_like(l_ref)
    acc_ref[...] = jnp.zeros_like(acc_ref)

    def scores(t, s_ref):
        start = pl.multiple_of(t * TQ, TQ)
        s_ref[...] = lax.dot_general(k_ref[pl.ds(start, TQ), :], qz_ref[...], _NT,
                                     preferred_element_type=F32)

    def consume(t, s_ref, masked):
        s = s_ref[...]
        if masked:
            key = lax.broadcasted_iota(jnp.int32, (TQ, 2 * TQ), 0)
            col = lax.broadcasted_iota(jnp.int32, (TQ, 2 * TQ), 1)
            query = jnp.where(col >= TQ, col - TQ, col)
            s = jnp.where((key // CHUNK) <= (query // CHUNK), s, -jnp.inf)
        m_prev = m_ref[...]
        m_new = jnp.maximum(m_prev, jnp.max(s, axis=0, keepdims=True))
        alpha = jnp.exp2(m_prev - m_new)
        p = jnp.exp2(s - m_new)
        l_ref[...] = alpha * l_ref[...] + jnp.sum(p, axis=0, keepdims=True)
        acc_ref[...] = alpha * acc_ref[...] + jnp.dot(vt_ref[t], p.astype(BF16),
                                                      preferred_element_type=F32)
        m_ref[...] = m_new

    scores(0, sa_ref)

    def pair(p, carry):
        t = 2 * p
        scores(t + 1, sb_ref)
        consume(t, sa_ref, False)
        scores(t + 2, sa_ref)
        consume(t + 1, sb_ref, False)
        return carry

    lax.fori_loop(0, i // 2, pair, 0)

    @pl.when(i % 2 == 1)
    def _():
        scores(i, sb_ref)
        consume(i - 1, sa_ref, False)
        consume(i, sb_ref, True)

    @pl.when(i % 2 == 0)
    def _():
        consume(i, sa_ref, True)

    o = acc_ref[...] / l_ref[...]
    y = _sub_ln(o[:, 0:TQ], o[:, TQ:], _lambda(lam_ref, lam_init), g_ref[...], lam_init, axis=0)
    o_ref[...] = y.T.astype(BF16)


def _prompt_attn(st, layer, q, k, v, lam, subln_g, lam_init):
    nq = st.seq // TQ
    return pl.pallas_call(
        functools.partial(_prompt_attn_kernel, lam_init=lam_init),
        grid=(st.batch, N_HEADS, nq),
        in_specs=[
            pl.BlockSpec((TQ, V_DIM), lambda b, h, i: (b * nq + i, h)),
            pl.BlockSpec((st.seq, V_DIM), lambda b, h, i: (b, h)),
            pl.BlockSpec((st.seq, V_DIM), lambda b, h, i: (b, h)),
            pl.BlockSpec((None, 4, HEAD_DIM), lambda b, h, i: (layer, 0, 0)),
            pl.BlockSpec((None, V_DIM, 1), lambda b, h, i: (layer, 0, 0)),
        ],
        out_specs=pl.BlockSpec((TQ, V_DIM), lambda b, h, i: (b * nq + i, h)),
        out_shape=jax.ShapeDtypeStruct((st.rows, ATTN_WIDTH), BF16),
        scratch_shapes=[
            pltpu.VMEM((2 * TQ, V_DIM), BF16),
            pltpu.VMEM((nq, V_DIM, TQ), BF16),
            pltpu.VMEM((TQ, 2 * TQ), F32),
            pltpu.VMEM((TQ, 2 * TQ), F32),
            pltpu.VMEM((1, 2 * TQ), F32),
            pltpu.VMEM((1, 2 * TQ), F32),
            pltpu.VMEM((V_DIM, 2 * TQ), F32),
        ],
        compiler_params=_params("arbitrary", "arbitrary", "arbitrary"),
        name="prompt_attn",
    )(q, k, v, lam, subln_g.reshape(DEPTH, V_DIM, 1))


def _decode_attn_kernel(q_ref, kc_ref, vc_ref, kn_ref, vn_ref, lam_ref, g_ref, o_ref,
                        qbd_ref, vbf_ref, m_ref, l_ref, acc_ref, *, lam_init):
    t = pl.program_id(1)
    n_score_heads = 2 * N_HEADS

    @pl.when(t == 0)
    def _():
        q = q_ref[...]
        col = lax.broadcasted_iota(jnp.int32, (DEC_SEQ, ATTN_WIDTH), 1)
        zero = jnp.zeros_like(q)
        for h in range(n_score_heads):
            qbd_ref[h * DEC_SEQ:(h + 1) * DEC_SEQ, :] = jnp.where(col // HEAD_DIM == h, q, zero)
        m_ref[...] = jnp.full_like(m_ref, -jnp.inf)
        l_ref[...] = jnp.zeros_like(l_ref)
        acc_ref[...] = jnp.zeros_like(acc_ref)

    s = jnp.dot(qbd_ref[...], kc_ref[...].astype(BF16), preferred_element_type=F32)
    for h in range(N_HEADS):
        vbf_ref[:, h * V_DIM:(h + 1) * V_DIM] = (
            vc_ref[pl.ds(h, TK_DEC, stride=N_HEADS), :].astype(BF16))
    _online_softmax_step(s, vbf_ref[...], m_ref, l_ref, acc_ref)

    @pl.when(t == pl.num_programs(1) - 1)
    def _():
        s_new = lax.dot_general(qbd_ref[...], kn_ref[...], _NT, preferred_element_type=F32)
        _online_softmax_step(s_new, vn_ref[...], m_ref, l_ref, acc_ref)
        o = acc_ref[...] / l_ref[...]
        lam = _lambda(lam_ref, lam_init)
        g = g_ref[...]
        for h in range(N_HEADS):
            cols = slice(h * V_DIM, (h + 1) * V_DIM)
            r0 = 2 * h * DEC_SEQ
            y = _sub_ln(o[r0:r0 + DEC_SEQ, cols], o[r0 + DEC_SEQ:r0 + 2 * DEC_SEQ, cols], lam, g, lam_init)
            o_ref[:, cols] = y.astype(BF16)


def _decode_attn(st, layer, q, k_new, v_new, cache_k, cache_v, lam, subln_g, lam_init):
    rows_bd = 2 * N_HEADS * DEC_SEQ
    tok = pl.BlockSpec((DEC_SEQ, ATTN_WIDTH), lambda b, t: (b, 0))
    tiles = PAST_LEN // TK_DEC
    cache_k_spec = pl.BlockSpec((None, ATTN_WIDTH, TK_DEC), lambda b, t: (layer * st.batch + b, 0, t))
    cache_v_spec = pl.BlockSpec((TK_DEC * N_HEADS, V_DIM),
                                lambda b, t: ((layer * st.batch + b) * tiles + t, 0))
    return pl.pallas_call(
        functools.partial(_decode_attn_kernel, lam_init=lam_init),
        grid=(st.batch, tiles),
        in_specs=[
            tok, cache_k_spec, cache_v_spec, tok, tok,
            pl.BlockSpec((None, 4, HEAD_DIM), lambda b, t: (layer, 0, 0)),
            pl.BlockSpec((None, 1, V_DIM), lambda b, t: (layer, 0, 0)),
        ],
        out_specs=tok,
        out_shape=jax.ShapeDtypeStruct((st.rows, ATTN_WIDTH), BF16),
        scratch_shapes=[
            pltpu.VMEM((rows_bd, ATTN_WIDTH), BF16),
            pltpu.VMEM((TK_DEC, ATTN_WIDTH), BF16),
            pltpu.VMEM((rows_bd, 1), F32),
            pltpu.VMEM((rows_bd, 1), F32),
            pltpu.VMEM((rows_bd, ATTN_WIDTH), F32),
        ],
        compiler_params=_params("arbitrary", "arbitrary"),
        name="decode_attn",
    )(q, cache_k, cache_v, k_new, v_new, lam, subln_g)


def _out_proj_kernel(x_ref, pool_ref, att_ref, w_ref, gate_ref, g_ref, sc_ref, sh_ref,
                     xo_ref, h_ref):
    mixed = (jnp.dot(pool_ref[...], w_ref[0:POOL_WIDTH, :], preferred_element_type=F32)
             + jnp.dot(att_ref[...], w_ref[POOL_WIDTH:, :], preferred_element_type=F32))
    x = x_ref[...] + gate_ref[...] * mixed
    xo_ref[...] = x
    y = _rms(x, NORM_EPS) * g_ref[...]
    h_ref[...] = (y * (1.0 + sc_ref[...]) + sh_ref[...]).astype(BF16)


def _out_proj(st, layer, x, pool_out, att, mod, w_out, norm_g):
    m = st.rows
    tm = TM
    row = lambda i, j: (i, 0)
    return pl.pallas_call(
        _out_proj_kernel,
        grid=(m // tm, 1),
        in_specs=[
            pl.BlockSpec((tm, D_MODEL), row),
            pl.BlockSpec((tm, POOL_WIDTH), row),
            pl.BlockSpec((tm, ATTN_WIDTH), row),
            pl.BlockSpec((None, D_MODEL, D_MODEL), lambda i, j: (layer, 0, 0),
                         pipeline_mode=pl.Buffered(1)),
            _mod_spec(st, layer, MOD_GATE_A, tm),
            pl.BlockSpec((None, 1, D_MODEL), lambda i, j: (layer, 0, 0)),
            _mod_spec(st, layer, MOD_SCALE_F, tm),
            _mod_spec(st, layer, MOD_SHIFT_F, tm),
        ],
        out_specs=[pl.BlockSpec((tm, D_MODEL), row), pl.BlockSpec((tm, D_MODEL), row)],
        out_shape=[jax.ShapeDtypeStruct((m, D_MODEL), F32), jax.ShapeDtypeStruct((m, D_MODEL), BF16)],
        compiler_params=_params("arbitrary", "arbitrary"),
        name="out_proj",
    )(x, pool_out, att, w_out, mod, norm_g, mod, mod)


def _gate_up_kernel(h_ref, wg_ref, wu_ref, o_ref):
    h = h_ref[...]
    g = jnp.dot(h, wg_ref[...], preferred_element_type=F32)
    u = jnp.dot(h, wu_ref[...], preferred_element_type=F32)
    o_ref[...] = (g * jax.nn.sigmoid(g) * u).astype(BF16)


def _gate_up(st, layer, h, w_gate, w_up):
    m = st.rows
    tm = min(TM_FFN, m)
    w_spec = pl.BlockSpec((None, D_MODEL, TF), lambda i, j: (layer, 0, j))
    return pl.pallas_call(
        _gate_up_kernel,
        grid=(m // tm, D_FF // TF),
        in_specs=[pl.BlockSpec((tm, D_MODEL), lambda i, j: (i, 0)), w_spec, w_spec],
        out_specs=pl.BlockSpec((tm, TF), lambda i, j: (i, j)),
        out_shape=jax.ShapeDtypeStruct((m, D_FF), BF16),
        compiler_params=_params("arbitrary", "arbitrary"),
        name="ffn_gate_up",
    )(h, w_gate, w_up)


def _down_kernel(a_ref, w_ref, x_ref, gate_ref, o_ref):
    y = jnp.dot(a_ref[...], w_ref[...], preferred_element_type=F32)
    o_ref[...] = x_ref[...] + gate_ref[...] * y


def _down(st, layer, act, x, mod, w_down):
    m = st.rows
    tm = TM
    return pl.pallas_call(
        _down_kernel,
        grid=(m // tm, D_MODEL // TN_DOWN),
        in_specs=[
            pl.BlockSpec((tm, D_FF), lambda i, j: (i, 0)),
            pl.BlockSpec((None, D_FF, TN_DOWN), lambda i, j: (layer, 0, j)),
            pl.BlockSpec((tm, TN_DOWN), lambda i, j: (i, j)),
            _mod_spec(st, layer, MOD_GATE_F, tm, TN_DOWN),
        ],
        out_specs=pl.BlockSpec((tm, TN_DOWN), lambda i, j: (i, j)),
        out_shape=jax.ShapeDtypeStruct((m, D_MODEL), F32),
        compiler_params=_params("arbitrary", "arbitrary"),
        name="ffn_down",
    )(act, w_down, x, mod)


def _final_norm_kernel(x_ref, g_ref, o_ref):
    o_ref[...] = _rms(x_ref[...], NORM_EPS) * g_ref[...]


def _final_norm(x, g):
    m = x.shape[0]
    return pl.pallas_call(
        _final_norm_kernel,
        grid=(m // TM,),
        in_specs=[pl.BlockSpec((TM, D_MODEL), lambda i: (i, 0)), pl.BlockSpec((1, D_MODEL), lambda i: (0, 0))],
        out_specs=pl.BlockSpec((TM, D_MODEL), lambda i: (i, 0)),
        out_shape=jax.ShapeDtypeStruct((m, D_MODEL), F32),
        compiler_params=_params("arbitrary"),
        name="final_norm",
    )(x, g)


def _rope_tables(st):
    half = HEAD_DIM // 2
    inv = 1.0 / (ROPE_THETA ** (jnp.arange(half, dtype=F32) * (2.0 / HEAD_DIM)))
    pos = st.pos0 + jnp.arange(st.seq)
    ang = pos.astype(F32)[:, None] * inv[None, :]
    cos, sin = jnp.cos(ang), jnp.sin(ang)
    reps = V7X_LANES // HEAD_DIM
    cos_t = jnp.tile(jnp.concatenate([cos, cos], axis=-1), (1, reps))
    sin_t = jnp.tile(jnp.concatenate([-sin, sin], axis=-1), (1, reps))
    if st.seq < TM:
        cos_t = jnp.tile(cos_t, (TM // st.seq, 1))
        sin_t = jnp.tile(sin_t, (TM // st.seq, 1))
    return cos_t, sin_t


def _run_stream(st, x, mod, hist, caches, p):
    m = st.rows
    x = x.reshape(m, D_MODEL)
    cos_tab, sin_tab = _rope_tables(st)
    ks, vs, us = [], [], []
    for l in range(DEPTH):
        lam_init = 0.8 - 0.6 * math.exp(-0.3 * l)
        u, q, k, kb, v, vb = _in_proj(st, l, x, mod, p["norm_mix_g"], p["w_in"], cos_tab, sin_tab)
        pool_out = _pool(st, l, u, hist[l], p["w_pool"], p["pool_scale"])
        if caches is None:
            att = _prompt_attn(st, l, q, kb, vb, p["lam"], p["subln_g"], lam_init)
        else:
            att = _decode_attn(st, l, q, kb, vb, caches[0], caches[1], p["lam"], p["subln_g"], lam_init)
        x, h = _out_proj(st, l, x, pool_out, att, mod, p["w_out"], p["norm_ffn_g"])
        act = _gate_up(st, l, h, p["w_gate"], p["w_up"])
        x = _down(st, l, act, x, mod, p["w_down"])
        ks.append(k)
        vs.append(v)
        us.append(u)
    y = _final_norm(x, p["final_g"]).reshape(st.batch, st.seq, D_MODEL)
    if st.k_transposed:
        k_out = jnp.stack(ks).reshape(DEPTH, st.batch, 2 * N_HEADS, HEAD_DIM, st.seq)
        k_out = jnp.transpose(k_out, (0, 1, 4, 2, 3))
    else:
        k_out = jnp.stack(ks).reshape(DEPTH, st.batch, st.seq, 2 * N_HEADS, HEAD_DIM)
    v_out = jnp.stack(vs).reshape(DEPTH, st.batch, st.seq, N_HEADS, V_DIM)
    u_all = jnp.stack(us).reshape(DEPTH, st.batch, st.seq, POOL_WIDTH)
    return y, k_out, v_out, u_all[:, :, st.seq - POOL_HIST:]


def kernel(x_prompt, x_sample, c_prompt, c_sample, cache_k, cache_v, state_pool, w_mod, b_mod,
           norm_mix_g, w_in, w_pool, pool_scale, lam_q1, lam_k1, lam_q2, lam_k2, subln_g, w_out,
           norm_ffn_g, w_gate, w_up, w_down, final_g):
    p = {
        "norm_mix_g": norm_mix_g.reshape(DEPTH, 1, D_MODEL),
        "norm_ffn_g": norm_ffn_g.reshape(DEPTH, 1, D_MODEL),
        "final_g": final_g.reshape(1, D_MODEL),
        "w_in": w_in.astype(BF16),
        "w_pool": w_pool.astype(BF16),
        "pool_scale": pool_scale.reshape(DEPTH, 1, POOL_WIDTH),
        "lam": jnp.stack([lam_q1, lam_k1, lam_q2, lam_k2], axis=1),
        "subln_g": subln_g.reshape(DEPTH, 1, V_DIM),
        "w_out": w_out.astype(BF16),
        "w_gate": w_gate.astype(BF16),
        "w_up": w_up.astype(BF16),
        "w_down": w_down.astype(BF16),
    }
    c_all = jnp.concatenate([c_prompt, c_sample], axis=0)
    c_all = jnp.pad(c_all, ((0, -c_all.shape[0] % 8), (0, 0)))
    mod = _modulation(c_all, w_mod, b_mod)[:, :BATCH + DEC_BATCH]
    mod_prompt = mod[:, :BATCH].reshape(DEPTH * BATCH, 1, N_MOD * D_MODEL)
    mod_sample = jnp.repeat(mod[:, BATCH:], DEC_SEQ, axis=1)

    hist_prompt = jnp.zeros((DEPTH, BATCH, HALO, POOL_WIDTH), F32)
    hist_sample = jnp.pad(state_pool, ((0, 0), (0, 0), (HALO - POOL_HIST, 0), (0, 0)))
    caches = (jnp.transpose(cache_k, (0, 1, 3, 4, 2)).reshape(DEPTH * DEC_BATCH, ATTN_WIDTH, PAST_LEN),
              cache_v.reshape(-1, V_DIM))

    y_p, k_p, v_p, pool_p = _run_stream(PROMPT, x_prompt, mod_prompt, hist_prompt, None, p)
    y_s, k_s, v_s, pool_s = _run_stream(SAMPLE, x_sample, mod_sample, hist_sample, caches, p)
    return (y_p, y_s, k_p, v_p, pool_p, k_s, v_s, pool_s)
```

```python
import functools
import math
from typing import NamedTuple

import jax
import jax.numpy as jnp
from jax import lax
from jax.experimental import pallas as pl
from jax.experimental.pallas import tpu as pltpu

F32 = jnp.float32
BF16 = jnp.bfloat16

D_MODEL = 2048
BATCH = 4
SEQ = 4096
DEPTH = 4
DEC_BATCH = 32
DEC_SEQ = 16
PAST_LEN = 2048
CHUNK = 64
POOL_WIDTH = D_MODEL // 2
N_POOL_GROUPS = 4
POOL_GROUP = POOL_WIDTH // N_POOL_GROUPS
POOL_WINDOWS = (2, 4, 8, 16)
POOL_HIST = max(POOL_WINDOWS) - 1
HALO = POOL_HIST + 1
ATTN_WIDTH = D_MODEL - POOL_WIDTH
HEAD_DIM = 64
N_HEADS = ATTN_WIDTH // (2 * HEAD_DIM)
V_DIM = 2 * HEAD_DIM
IN_COLS = POOL_WIDTH + 3 * ATTN_WIDTH
D_FF = -(-8 * D_MODEL // (3 * 256)) * 256
ROPE_THETA = 10000.0
NORM_EPS = 1e-6
SUBLN_EPS = 1e-5
N_MOD = 6
MOD_SHIFT_A, MOD_SCALE_A, MOD_GATE_A, MOD_SHIFT_F, MOD_SCALE_F, MOD_GATE_F = range(N_MOD)

V7X_VMEM_BYTES = 64 * 1024 * 1024
V7X_LANES = 128
VMEM_LIMIT = V7X_VMEM_BYTES * 7 // 8

TM = 512
TM_FFN = 1024
TN_MOD = 1024
TF = 512
TN_DOWN = 512
TQ = 1024
TK = 512
DIAG_TILES = TQ // TK
SCORE_SCALE_LOG2 = HEAD_DIM ** -0.5 * math.log2(math.e)
TK_DEC = 512


class Stream(NamedTuple):
    batch: int
    seq: int
    pos0: int
    mod_rows: int
    k_transposed: bool

    @property
    def rows(self):
        return self.batch * self.seq


PROMPT = Stream(BATCH, SEQ, 0, 1, True)
SAMPLE = Stream(DEC_BATCH, DEC_SEQ, PAST_LEN, DEC_BATCH * DEC_SEQ, False)


def _params(*semantics, flags=None):
    return pltpu.CompilerParams(dimension_semantics=semantics, vmem_limit_bytes=VMEM_LIMIT, flags=flags)


def _rms(x, eps, axis=-1):
    return x * lax.rsqrt(jnp.mean(x * x, axis=axis, keepdims=True) + eps)


def _mod_spec(st, layer, which, tm, tn=D_MODEL):
    per_chunk = D_MODEL // tn
    if st.mod_rows == 1:
        tiles_per_batch = st.seq // tm
        return pl.BlockSpec(
            (None, 1, tn),
            lambda i, j: (layer * st.batch + i // tiles_per_batch, 0, which * per_chunk + j % per_chunk))
    return pl.BlockSpec((None, tm, tn), lambda i, j: (layer, i, which * per_chunk + j % per_chunk))


def _mod_kernel(c_ref, w_ref, b_ref, o_ref):
    c = c_ref[...]
    cs = (c * jax.nn.sigmoid(c)).astype(BF16)
    o_ref[...] = jnp.dot(cs, w_ref[...].astype(BF16), preferred_element_type=F32) + b_ref[...]


def _modulation(c_all, w_mod, b_mod):
    nb = c_all.shape[0]
    n = N_MOD * D_MODEL
    return pl.pallas_call(
        _mod_kernel,
        grid=(DEPTH, n // TN_MOD),
        in_specs=[
            pl.BlockSpec((nb, D_MODEL), lambda l, j: (0, 0)),
            pl.BlockSpec((None, D_MODEL, TN_MOD), lambda l, j: (l, 0, j)),
            pl.BlockSpec((None, 1, TN_MOD), lambda l, j: (l, 0, j)),
        ],
        out_specs=pl.BlockSpec((None, nb, TN_MOD), lambda l, j: (l, 0, j)),
        out_shape=jax.ShapeDtypeStruct((DEPTH, nb, n), F32),
        compiler_params=_params("arbitrary", "arbitrary"),
        name="modulation",
    )(c_all, w_mod, b_mod.reshape(DEPTH, 1, n))


def _rope_store(acc, cos_ref, sin_ref, scale, out_refs, transposed_ref=None):
    tm = acc.shape[0]
    lane = lax.broadcasted_iota(jnp.int32, (tm, V7X_LANES), 1)
    first_half = (lane % HEAD_DIM) < (HEAD_DIM // 2)
    cos = cos_ref[...]
    sin = sin_ref[...]
    for c in range(ATTN_WIDTH // V7X_LANES):
        cols = slice(c * V7X_LANES, (c + 1) * V7X_LANES)
        xs = acc[:, cols]
        partner = jnp.where(first_half,
                            pltpu.roll(xs, V7X_LANES - HEAD_DIM // 2, 1),
                            pltpu.roll(xs, HEAD_DIM // 2, 1))
        y = xs * cos + partner * sin
        if scale != 1.0:
            y = y * scale
        for r in out_refs:
            r[:, cols] = y.astype(r.dtype)
        if transposed_ref is not None:
            transposed_ref[cols, :] = y.T


def _in_proj_kernel(x_ref, g_ref, sc_ref, sh_ref, w_ref, cos_ref, sin_ref,
                    u_ref, q_ref, k_ref, kb_ref, v_ref, vb_ref, h_scr, *, k_transposed):
    j = pl.program_id(1)

    @pl.when(j == 0)
    def _():
        y = _rms(x_ref[...], NORM_EPS) * g_ref[...]
        h_scr[...] = (y * (1.0 + sc_ref[...]) + sh_ref[...]).astype(BF16)

    acc = jnp.dot(h_scr[...], w_ref[...], preferred_element_type=F32)

    @pl.when(j == 0)
    def _():
        u_ref[...] = acc

    @pl.when(j == 1)
    def _():
        _rope_store(acc, cos_ref, sin_ref, SCORE_SCALE_LOG2, (q_ref,))

    @pl.when(j == 2)
    def _():
        if k_transposed:
            _rope_store(acc, cos_ref, sin_ref, 1.0, (kb_ref,), k_ref)
        else:
            _rope_store(acc, cos_ref, sin_ref, 1.0, (k_ref, kb_ref))

    @pl.when(j == 3)
    def _():
        v_ref[...] = acc
        vb_ref[...] = acc.astype(BF16)


def _in_proj(st, layer, x, mod, norm_g, w_in, cos_tab, sin_tab):
    m = st.rows
    tm = TM
    tab_tiles = cos_tab.shape[0] // tm
    row = lambda i, j: (i, 0)
    out_spec = pl.BlockSpec((tm, ATTN_WIDTH), row)
    sds = lambda dt: jax.ShapeDtypeStruct((m, ATTN_WIDTH), dt)
    out_specs = [out_spec] * 6
    out_shape = [sds(F32), sds(BF16), sds(F32), sds(BF16), sds(F32), sds(BF16)]
    if st.k_transposed:
        tiles_per_batch = st.seq // tm
        out_specs[2] = pl.BlockSpec((None, ATTN_WIDTH, tm),
                                    lambda i, j: (i // tiles_per_batch, 0, i % tiles_per_batch))
        out_shape[2] = jax.ShapeDtypeStruct((st.batch, ATTN_WIDTH, st.seq), F32)
    return pl.pallas_call(
        functools.partial(_in_proj_kernel, k_transposed=st.k_transposed),
        grid=(m // tm, IN_COLS // ATTN_WIDTH),
        in_specs=[
            pl.BlockSpec((tm, D_MODEL), row),
            pl.BlockSpec((None, 1, D_MODEL), lambda i, j: (layer, 0, 0)),
            _mod_spec(st, layer, MOD_SCALE_A, tm),
            _mod_spec(st, layer, MOD_SHIFT_A, tm),
            pl.BlockSpec((None, D_MODEL, ATTN_WIDTH), lambda i, j: (layer, 0, j)),
            pl.BlockSpec((tm, V7X_LANES), lambda i, j: (i % tab_tiles, 0)),
            pl.BlockSpec((tm, V7X_LANES), lambda i, j: (i % tab_tiles, 0)),
        ],
        out_specs=out_specs,
        out_shape=out_shape,
        scratch_shapes=[pltpu.VMEM((tm, D_MODEL), BF16)],
        compiler_params=_params("arbitrary", "arbitrary"),
        name="in_proj",
    )(x, norm_g, mod, mod, w_in, cos_tab, sin_tab)


def _pool_kernel(u_ref, hist_ref, wp_ref, ps_ref, o_ref, ext_ref, *, tm, pos0):
    i = pl.program_id(1)

    @pl.when(i == 0)
    def _():
        ext_ref[0:HALO, :] = hist_ref[...]

    @pl.when(i > 0)
    def _():
        ext_ref[0:HALO, :] = ext_ref[tm:tm + HALO, :]

    ext_ref[HALO:, :] = u_ref[...]
    pos = pos0 + i * tm + lax.broadcasted_iota(jnp.int32, (tm, 1), 0)
    for g, w in enumerate(POOL_WINDOWS):
        cols = slice(g * POOL_GROUP, (g + 1) * POOL_GROUP)
        cur = u_ref[:, cols]
        s = cur
        for back in range(1, w):
            s = s + ext_ref[HALO - back:HALO - back + tm, cols]
        count = jnp.minimum(w, pos + 1).astype(F32)
        d = (s / count - cur).astype(BF16)
        y = jnp.dot(d, wp_ref[g], preferred_element_type=F32) * ps_ref[:, cols]
        o_ref[:, cols] = y.astype(BF16)


def _pool(st, layer, u, hist, w_pool, pool_scale):
    tm = min(TM, st.seq)
    tiles = st.seq // tm
    return pl.pallas_call(
        functools.partial(_pool_kernel, tm=tm, pos0=st.pos0),
        grid=(st.batch, tiles),
        in_specs=[
            pl.BlockSpec((tm, POOL_WIDTH), lambda b, i: (b * tiles + i, 0)),
            pl.BlockSpec((None, HALO, POOL_WIDTH), lambda b, i: (b, 0, 0)),
            pl.BlockSpec((None, N_POOL_GROUPS, POOL_GROUP, POOL_GROUP), lambda b, i: (layer, 0, 0, 0)),
            pl.BlockSpec((None, 1, POOL_WIDTH), lambda b, i: (layer, 0, 0)),
        ],
        out_specs=pl.BlockSpec((tm, POOL_WIDTH), lambda b, i: (b * tiles + i, 0)),
        out_shape=jax.ShapeDtypeStruct((st.rows, POOL_WIDTH), BF16),
        scratch_shapes=[pltpu.VMEM((tm + HALO, POOL_WIDTH), F32)],
        compiler_params=_params("arbitrary", "arbitrary"),
        name="pool_mix",
    )(u, hist, w_pool, pool_scale)


def _lambda(lam_ref, lam_init):
    lam = lam_ref[...]
    s1 = jnp.sum(lam[0:1] * lam[1:2], axis=-1, keepdims=True)
    s2 = jnp.sum(lam[2:3] * lam[3:4], axis=-1, keepdims=True)
    return jnp.exp(s1) - jnp.exp(s2) + lam_init


def _online_softmax_step(s, v, m_ref, l_ref, acc_ref):
    m_prev = m_ref[...]
    m_new = jnp.maximum(m_prev, jnp.max(s, axis=-1, keepdims=True))
    alpha = jnp.exp2(m_prev - m_new)
    p = jnp.exp2(s - m_new)
    l_ref[...] = alpha * l_ref[...] + jnp.sum(p, axis=-1, keepdims=True)
    acc_ref[...] = alpha * acc_ref[...] + jnp.dot(p.astype(BF16), v, preferred_element_type=F32)
    m_ref[...] = m_new


def _sub_ln(o1, o2, lam, g, lam_init, axis=-1):
    a = o1 - lam * o2
    return (_rms(a, SUBLN_EPS, axis) * g) * (1.0 - lam_init)


_NT = (((1,), (1,)), ((), ()))


def _prompt_attn_kernel(q_ref, k_ref, v_ref, lam_ref, g_ref, o_ref,
                        qz_ref, vt_ref, sa_ref, sb_ref, m_ref, l_ref, acc_ref, *, lam_init):
    i = pl.program_id(2)

    @pl.when(i == 0)
    def _():
        for t in range(vt_ref.shape[0]):
            vt_ref[t] = v_ref[t * TK:(t + 1) * TK, :].astype(F32).T.astype(BF16)

    q = q_ref[...]
    lane = lax.broadcasted_iota(jnp.int32, (TQ, V_DIM), 1)
    zero = jnp.zeros_like(q)
    qz_ref[0:TQ, :] = jnp.where(lane < HEAD_DIM, q, zero)
    qz_ref[TQ:, :] = jnp.where(lane >= HEAD_DIM, q, zero)
    m_ref[...] = jnp.full_like(m_ref, -jnp.inf)
    l_ref[...] = jnp.zeros_like(l_ref)
    acc_ref[...] = jnp.zeros_like(acc_ref)

    def scores(t, s_ref):
        start = pl.multiple_of(t * TK, TK)
        s_ref[...] = lax.dot_general(k_ref[pl.ds(start, TK), :], qz_ref[...], _NT,
                                     preferred_element_type=F32)

    def consume(t, s_ref, diag=None):
        q_lo = 0 if diag is None else diag * TK
        width = TQ - q_lo
        col_sets = (slice(q_lo, TQ), slice(TQ + q_lo, 2 * TQ))

        def gather(ref):
            if q_lo == 0:
                return ref[...]
            return jnp.concatenate([ref[:, c] for c in col_sets], axis=1)

        def scatter(ref, val):
            if q_lo == 0:
                ref[...] = val
            else:
                for n, c in enumerate(col_sets):
                    ref[:, c] = val[:, n * width:(n + 1) * width]

        s = gather(s_ref)
        if diag is not None:
            key = q_lo + lax.broadcasted_iota(jnp.int32, (TK, 2 * width), 0)
            col = lax.broadcasted_iota(jnp.int32, (TK, 2 * width), 1)
            query = q_lo + jnp.where(col >= width, col - width, col)
            s = jnp.where((key // CHUNK) <= (query // CHUNK), s, -jnp.inf)
        m_prev = gather(m_ref)
        m_new = jnp.maximum(m_prev, jnp.max(s, axis=0, keepdims=True))
        alpha = jnp.exp2(m_prev - m_new)
        p = jnp.exp2(s - m_new)
        scatter(l_ref, alpha * gather(l_ref) + jnp.sum(p, axis=0, keepdims=True))
        scatter(acc_ref, alpha * gather(acc_ref) + jnp.dot(vt_ref[t], p.astype(BF16),
                                                           preferred_element_type=F32))
        scatter(m_ref, m_new)

    scores(0, sa_ref)

    def pair(p, carry):
        t = 2 * p
        scores(t + 1, sb_ref)
        consume(t, sa_ref)
        scores(t + 2, sa_ref)
        consume(t + 1, sb_ref)
        return carry

    lax.fori_loop(0, i, pair, 0)
    first_diag = DIAG_TILES * i
    scores(first_diag + 1, sb_ref)
    consume(first_diag, sa_ref, diag=0)
    consume(first_diag + 1, sb_ref, diag=1)

    o = acc_ref[...] / l_ref[...]
    y = _sub_ln(o[:, 0:TQ], o[:, TQ:], _lambda(lam_ref, lam_init), g_ref[...], lam_init, axis=0)
    o_ref[...] = y.T.astype(BF16)


def _prompt_attn(st, layer, q, k, v, lam, subln_g, lam_init):
    assert DIAG_TILES == 2
    nq = st.seq // TQ
    return pl.pallas_call(
        functools.partial(_prompt_attn_kernel, lam_init=lam_init),
        grid=(st.batch, N_HEADS, nq),
        in_specs=[
            pl.BlockSpec((TQ, V_DIM), lambda b, h, i: (b * nq + i, h)),
            pl.BlockSpec((st.seq, V_DIM), lambda b, h, i: (b, h)),
            pl.BlockSpec((st.seq, V_DIM), lambda b, h, i: (b, h)),
            pl.BlockSpec((None, 4, HEAD_DIM), lambda b, h, i: (layer, 0, 0)),
            pl.BlockSpec((None, V_DIM, 1), lambda b, h, i: (layer, 0, 0)),
        ],
        out_specs=pl.BlockSpec((TQ, V_DIM), lambda b, h, i: (b * nq + i, h)),
        out_shape=jax.ShapeDtypeStruct((st.rows, ATTN_WIDTH), BF16),
        scratch_shapes=[
            pltpu.VMEM((2 * TQ, V_DIM), BF16),
            pltpu.VMEM((st.seq // TK, V_DIM, TK), BF16),
            pltpu.VMEM((TK, 2 * TQ), F32),
            pltpu.VMEM((TK, 2 * TQ), F32),
            pltpu.VMEM((1, 2 * TQ), F32),
            pltpu.VMEM((1, 2 * TQ), F32),
            pltpu.VMEM((V_DIM, 2 * TQ), F32),
        ],
        compiler_params=_params("arbitrary", "arbitrary", "arbitrary"),
        name="prompt_attn",
    )(q, k, v, lam, subln_g.reshape(DEPTH, V_DIM, 1))


def _decode_attn_kernel(q_ref, kc_ref, vc_ref, kn_ref, vn_ref, lam_ref, g_ref, o_ref,
                        qbd_ref, vbf_ref, m_ref, l_ref, acc_ref, *, lam_init):
    t = pl.program_id(1)
    n_score_heads = 2 * N_HEADS

    @pl.when(t == 0)
    def _():
        q = q_ref[...]
        col = lax.broadcasted_iota(jnp.int32, (DEC_SEQ, ATTN_WIDTH), 1)
        zero = jnp.zeros_like(q)
        for h in range(n_score_heads):
            qbd_ref[h * DEC_SEQ:(h + 1) * DEC_SEQ, :] = jnp.where(col // HEAD_DIM == h, q, zero)
        m_ref[...] = jnp.full_like(m_ref, -jnp.inf)
        l_ref[...] = jnp.zeros_like(l_ref)
        acc_ref[...] = jnp.zeros_like(acc_ref)

    s = jnp.dot(qbd_ref[...], kc_ref[...].astype(BF16), preferred_element_type=F32)
    for h in range(N_HEADS):
        vbf_ref[:, h * V_DIM:(h + 1) * V_DIM] = (
            vc_ref[pl.ds(h, TK_DEC, stride=N_HEADS), :].astype(BF16))
    _online_softmax_step(s, vbf_ref[...], m_ref, l_ref, acc_ref)

    @pl.when(t == pl.num_programs(1) - 1)
    def _():
        s_new = lax.dot_general(qbd_ref[...], kn_ref[...], _NT, preferred_element_type=F32)
        _online_softmax_step(s_new, vn_ref[...], m_ref, l_ref, acc_ref)
        o = acc_ref[...] / l_ref[...]
        lam = _lambda(lam_ref, lam_init)
        g = g_ref[...]
        for h in range(N_HEADS):
            cols = slice(h * V_DIM, (h + 1) * V_DIM)
            r0 = 2 * h * DEC_SEQ
            y = _sub_ln(o[r0:r0 + DEC_SEQ, cols], o[r0 + DEC_SEQ:r0 + 2 * DEC_SEQ, cols], lam, g, lam_init)
            o_ref[:, cols] = y.astype(BF16)


def _decode_attn(st, layer, q, k_new, v_new, cache_k, cache_v, lam, subln_g, lam_init):
    rows_bd = 2 * N_HEADS * DEC_SEQ
    tok = pl.BlockSpec((DEC_SEQ, ATTN_WIDTH), lambda b, t: (b, 0))
    tiles = PAST_LEN // TK_DEC
    cache_k_spec = pl.BlockSpec((None, ATTN_WIDTH, TK_DEC), lambda b, t: (layer * st.batch + b, 0, t))
    cache_v_spec = pl.BlockSpec((TK_DEC * N_HEADS, V_DIM),
                                lambda b, t: ((layer * st.batch + b) * tiles + t, 0))
    return pl.pallas_call(
        functools.partial(_decode_attn_kernel, lam_init=lam_init),
        grid=(st.batch, tiles),
        in_specs=[
            tok, cache_k_spec, cache_v_spec, tok, tok,
            pl.BlockSpec((None, 4, HEAD_DIM), lambda b, t: (layer, 0, 0)),
            pl.BlockSpec((None, 1, V_DIM), lambda b, t: (layer, 0, 0)),
        ],
        out_specs=tok,
        out_shape=jax.ShapeDtypeStruct((st.rows, ATTN_WIDTH), BF16),
        scratch_shapes=[
            pltpu.VMEM((rows_bd, ATTN_WIDTH), BF16),
            pltpu.VMEM((TK_DEC, ATTN_WIDTH), BF16),
            pltpu.VMEM((rows_bd, 1), F32),
            pltpu.VMEM((rows_bd, 1), F32),
            pltpu.VMEM((rows_bd, ATTN_WIDTH), F32),
        ],
        compiler_params=_params("arbitrary", "arbitrary"),
        name="decode_attn",
    )(q, cache_k, cache_v, k_new, v_new, lam, subln_g)


def _out_proj_kernel(x_ref, pool_ref, att_ref, w_ref, gate_ref, g_ref, sc_ref, sh_ref,
                     xo_ref, h_ref):
    mixed = (jnp.dot(pool_ref[...], w_ref[0:POOL_WIDTH, :], preferred_element_type=F32)
             + jnp.dot(att_ref[...], w_ref[POOL_WIDTH:, :], preferred_element_type=F32))
    x = x_ref[...] + gate_ref[...] * mixed
    xo_ref[...] = x
    y = _rms(x, NORM_EPS) * g_ref[...]
    h_ref[...] = (y * (1.0 + sc_ref[...]) + sh_ref[...]).astype(BF16)


def _out_proj(st, layer, x, pool_out, att, mod, w_out, norm_g):
    m = st.rows
    tm = TM
    row = lambda i, j: (i, 0)
    return pl.pallas_call(
        _out_proj_kernel,
        grid=(m // tm, 1),
        in_specs=[
            pl.BlockSpec((tm, D_MODEL), row),
            pl.BlockSpec((tm, POOL_WIDTH), row),
            pl.BlockSpec((tm, ATTN_WIDTH), row),
            pl.BlockSpec((None, D_MODEL, D_MODEL), lambda i, j: (layer, 0, 0),
                         pipeline_mode=pl.Buffered(1)),
            _mod_spec(st, layer, MOD_GATE_A, tm),
            pl.BlockSpec((None, 1, D_MODEL), lambda i, j: (layer, 0, 0)),
            _mod_spec(st, layer, MOD_SCALE_F, tm),
            _mod_spec(st, layer, MOD_SHIFT_F, tm),
        ],
        out_specs=[pl.BlockSpec((tm, D_MODEL), row), pl.BlockSpec((tm, D_MODEL), row)],
        out_shape=[jax.ShapeDtypeStruct((m, D_MODEL), F32), jax.ShapeDtypeStruct((m, D_MODEL), BF16)],
        compiler_params=_params("arbitrary", "arbitrary"),
        name="out_proj",
    )(x, pool_out, att, w_out, mod, norm_g, mod, mod)


def _gate_up_kernel(h_ref, wg_ref, wu_ref, o_ref):
    h = h_ref[...]
    g = jnp.dot(h, wg_ref[...], preferred_element_type=F32)
    u = jnp.dot(h, wu_ref[...], preferred_element_type=F32)
    o_ref[...] = (g * jax.nn.sigmoid(g) * u).astype(BF16)


def _gate_up(st, layer, h, w_gate, w_up):
    m = st.rows
    tm = min(TM_FFN, m)
    w_spec = pl.BlockSpec((None, D_MODEL, TF), lambda i, j: (layer, 0, j))
    return pl.pallas_call(
        _gate_up_kernel,
        grid=(m // tm, D_FF // TF),
        in_specs=[pl.BlockSpec((tm, D_MODEL), lambda i, j: (i, 0)), w_spec, w_spec],
        out_specs=pl.BlockSpec((tm, TF), lambda i, j: (i, j)),
        out_shape=jax.ShapeDtypeStruct((m, D_FF), BF16),
        compiler_params=_params("arbitrary", "arbitrary"),
        name="ffn_gate_up",
    )(h, w_gate, w_up)


def _down_kernel(a_ref, w_ref, x_ref, gate_ref, o_ref):
    y = jnp.dot(a_ref[...], w_ref[...], preferred_element_type=F32)
    o_ref[...] = x_ref[...] + gate_ref[...] * y


def _down(st, layer, act, x, mod, w_down):
    m = st.rows
    tm = TM
    return pl.pallas_call(
        _down_kernel,
        grid=(m // tm, D_MODEL // TN_DOWN),
        in_specs=[
            pl.BlockSpec((tm, D_FF), lambda i, j: (i, 0)),
            pl.BlockSpec((None, D_FF, TN_DOWN), lambda i, j: (layer, 0, j)),
            pl.BlockSpec((tm, TN_DOWN), lambda i, j: (i, j)),
            _mod_spec(st, layer, MOD_GATE_F, tm, TN_DOWN),
        ],
        out_specs=pl.BlockSpec((tm, TN_DOWN), lambda i, j: (i, j)),
        out_shape=jax.ShapeDtypeStruct((m, D_MODEL), F32),
        compiler_params=_params("arbitrary", "arbitrary"),
        name="ffn_down",
    )(act, w_down, x, mod)


def _final_norm_kernel(x_ref, g_ref, o_ref):
    o_ref[...] = _rms(x_ref[...], NORM_EPS) * g_ref[...]


def _final_norm(x, g):
    m = x.shape[0]
    return pl.pallas_call(
        _final_norm_kernel,
        grid=(m // TM,),
        in_specs=[pl.BlockSpec((TM, D_MODEL), lambda i: (i, 0)), pl.BlockSpec((1, D_MODEL), lambda i: (0, 0))],
        out_specs=pl.BlockSpec((TM, D_MODEL), lambda i: (i, 0)),
        out_shape=jax.ShapeDtypeStruct((m, D_MODEL), F32),
        compiler_params=_params("arbitrary"),
        name="final_norm",
    )(x, g)


def _rope_tables(st):
    half = HEAD_DIM // 2
    inv = 1.0 / (ROPE_THETA ** (jnp.arange(half, dtype=F32) * (2.0 / HEAD_DIM)))
    pos = st.pos0 + jnp.arange(st.seq)
    ang = pos.astype(F32)[:, None] * inv[None, :]
    cos, sin = jnp.cos(ang), jnp.sin(ang)
    reps = V7X_LANES // HEAD_DIM
    cos_t = jnp.tile(jnp.concatenate([cos, cos], axis=-1), (1, reps))
    sin_t = jnp.tile(jnp.concatenate([-sin, sin], axis=-1), (1, reps))
    if st.seq < TM:
        cos_t = jnp.tile(cos_t, (TM // st.seq, 1))
        sin_t = jnp.tile(sin_t, (TM // st.seq, 1))
    return cos_t, sin_t


def _run_stream(st, x, mod, hist, caches, p):
    m = st.rows
    x = x.reshape(m, D_MODEL)
    cos_tab, sin_tab = _rope_tables(st)
    ks, vs, us = [], [], []
    for l in range(DEPTH):
        lam_init = 0.8 - 0.6 * math.exp(-0.3 * l)
        u, q, k, kb, v, vb = _in_proj(st, l, x, mod, p["norm_mix_g"], p["w_in"], cos_tab, sin_tab)
        pool_out = _pool(st, l, u, hist[l], p["w_pool"], p["pool_scale"])
        if caches is None:
            att = _prompt_attn(st, l, q, kb, vb, p["lam"], p["subln_g"], lam_init)
        else:
            att = _decode_attn(st, l, q, kb, vb, caches[0], caches[1], p["lam"], p["subln_g"], lam_init)
        x, h = _out_proj(st, l, x, pool_out, att, mod, p["w_out"], p["norm_ffn_g"])
        act = _gate_up(st, l, h, p["w_gate"], p["w_up"])
        x = _down(st, l, act, x, mod, p["w_down"])
        ks.append(k)
        vs.append(v)
        us.append(u)
    y = _final_norm(x, p["final_g"]).reshape(st.batch, st.seq, D_MODEL)
    if st.k_transposed:
        k_out = jnp.stack(ks).reshape(DEPTH, st.batch, 2 * N_HEADS, HEAD_DIM, st.seq)
        k_out = jnp.transpose(k_out, (0, 1, 4, 2, 3))
    else:
        k_out = jnp.stack(ks).reshape(DEPTH, st.batch, st.seq, 2 * N_HEADS, HEAD_DIM)
    v_out = jnp.stack(vs).reshape(DEPTH, st.batch, st.seq, N_HEADS, V_DIM)
    u_all = jnp.stack(us).reshape(DEPTH, st.batch, st.seq, POOL_WIDTH)
    return y, k_out, v_out, u_all[:, :, st.seq - POOL_HIST:]


def kernel(x_prompt, x_sample, c_prompt, c_sample, cache_k, cache_v, state_pool, w_mod, b_mod,
           norm_mix_g, w_in, w_pool, pool_scale, lam_q1, lam_k1, lam_q2, lam_k2, subln_g, w_out,
           norm_ffn_g, w_gate, w_up, w_down, final_g):
    p = {
        "norm_mix_g": norm_mix_g.reshape(DEPTH, 1, D_MODEL),
        "norm_ffn_g": norm_ffn_g.reshape(DEPTH, 1, D_MODEL),
        "final_g": final_g.reshape(1, D_MODEL),
        "w_in": w_in.astype(BF16),
        "w_pool": w_pool.astype(BF16),
        "pool_scale": pool_scale.reshape(DEPTH, 1, POOL_WIDTH),
        "lam": jnp.stack([lam_q1, lam_k1, lam_q2, lam_k2], axis=1),
        "subln_g": subln_g.reshape(DEPTH, 1, V_DIM),
        "w_out": w_out.astype(BF16),
        "w_gate": w_gate.astype(BF16),
        "w_up": w_up.astype(BF16),
        "w_down": w_down.astype(BF16),
    }
    c_all = jnp.concatenate([c_prompt, c_sample], axis=0)
    c_all = jnp.pad(c_all, ((0, -c_all.shape[0] % 8), (0, 0)))
    mod = _modulation(c_all, w_mod, b_mod)[:, :BATCH + DEC_BATCH]
    mod_prompt = mod[:, :BATCH].reshape(DEPTH * BATCH, 1, N_MOD * D_MODEL)
    mod_sample = jnp.repeat(mod[:, BATCH:], DEC_SEQ, axis=1)

    hist_prompt = jnp.zeros((DEPTH, BATCH, HALO, POOL_WIDTH), F32)
    hist_sample = jnp.pad(state_pool, ((0, 0), (0, 0), (HALO - POOL_HIST, 0), (0, 0)))
    caches = (jnp.transpose(cache_k, (0, 1, 3, 4, 2)).reshape(DEPTH * DEC_BATCH, ATTN_WIDTH, PAST_LEN),
              cache_v.reshape(-1, V_DIM))

    y_p, k_p, v_p, pool_p = _run_stream(PROMPT, x_prompt, mod_prompt, hist_prompt, None, p)
    y_s, k_s, v_s, pool_s = _run_stream(SAMPLE, x_sample, mod_sample, hist_sample, caches, p)
    return (y_p, y_s, k_p, v_p, pool_p, k_s, v_s, pool_s)
```

```python
import functools
import math
from typing import NamedTuple

import jax
import jax.numpy as jnp
from jax import lax
from jax.experimental import pallas as pl
from jax.experimental.pallas import tpu as pltpu

F32 = jnp.float32
BF16 = jnp.bfloat16

D_MODEL = 2048
BATCH = 4
SEQ = 4096
DEPTH = 4
DEC_BATCH = 32
DEC_SEQ = 16
PAST_LEN = 2048
CHUNK = 64
POOL_WIDTH = D_MODEL // 2
N_POOL_GROUPS = 4
POOL_GROUP = POOL_WIDTH // N_POOL_GROUPS
POOL_WINDOWS = (2, 4, 8, 16)
POOL_HIST = max(POOL_WINDOWS) - 1
HALO = POOL_HIST + 1
ATTN_WIDTH = D_MODEL - POOL_WIDTH
HEAD_DIM = 64
N_HEADS = ATTN_WIDTH // (2 * HEAD_DIM)
V_DIM = 2 * HEAD_DIM
IN_COLS = POOL_WIDTH + 3 * ATTN_WIDTH
D_FF = -(-8 * D_MODEL // (3 * 256)) * 256
ROPE_THETA = 10000.0
NORM_EPS = 1e-6
SUBLN_EPS = 1e-5
N_MOD = 6
MOD_SHIFT_A, MOD_SCALE_A, MOD_GATE_A, MOD_SHIFT_F, MOD_SCALE_F, MOD_GATE_F = range(N_MOD)

V7X_VMEM_BYTES = 64 * 1024 * 1024
V7X_LANES = 128
VMEM_LIMIT = V7X_VMEM_BYTES * 7 // 8

TM = 512
TM_FFN = 1024
TN_MOD = 1024
TF = 512
TN_DOWN = 512
TQ = 1024
TK = 512
DIAG_TILES = TQ // TK
SCORE_SCALE_LOG2 = HEAD_DIM ** -0.5 * math.log2(math.e)
TK_DEC = 512


class Stream(NamedTuple):
    batch: int
    seq: int
    pos0: int
    mod_rows: int
    k_transposed: bool

    @property
    def rows(self):
        return self.batch * self.seq


PROMPT = Stream(BATCH, SEQ, 0, 1, True)
SAMPLE = Stream(DEC_BATCH, DEC_SEQ, PAST_LEN, DEC_BATCH * DEC_SEQ, False)


def _params(*semantics, flags=None):
    return pltpu.CompilerParams(dimension_semantics=semantics, vmem_limit_bytes=VMEM_LIMIT, flags=flags)


def _rms(x, eps, axis=-1):
    return x * lax.rsqrt(jnp.mean(x * x, axis=axis, keepdims=True) + eps)


def _mod_spec(st, layer, which, tm, tn=D_MODEL):
    per_chunk = D_MODEL // tn
    if st.mod_rows == 1:
        tiles_per_batch = st.seq // tm
        return pl.BlockSpec(
            (None, 1, tn),
            lambda i, j: (layer * st.batch + i // tiles_per_batch, 0, which * per_chunk + j % per_chunk))
    return pl.BlockSpec((None, tm, tn), lambda i, j: (layer, i, which * per_chunk + j % per_chunk))


def _mod_kernel(c_ref, w_ref, b_ref, o_ref):
    c = c_ref[...]
    cs = (c * jax.nn.sigmoid(c)).astype(BF16)
    o_ref[...] = jnp.dot(cs, w_ref[...].astype(BF16), preferred_element_type=F32) + b_ref[...]


def _modulation(c_all, w_mod, b_mod):
    nb = c_all.shape[0]
    n = N_MOD * D_MODEL
    return pl.pallas_call(
        _mod_kernel,
        grid=(DEPTH, n // TN_MOD),
        in_specs=[
            pl.BlockSpec((nb, D_MODEL), lambda l, j: (0, 0)),
            pl.BlockSpec((None, D_MODEL, TN_MOD), lambda l, j: (l, 0, j)),
            pl.BlockSpec((None, 1, TN_MOD), lambda l, j: (l, 0, j)),
        ],
        out_specs=pl.BlockSpec((None, nb, TN_MOD), lambda l, j: (l, 0, j)),
        out_shape=jax.ShapeDtypeStruct((DEPTH, nb, n), F32),
        compiler_params=_params("arbitrary", "arbitrary"),
        name="modulation",
    )(c_all, w_mod, b_mod.reshape(DEPTH, 1, n))


def _rope_store(acc, cos_ref, sin_ref, scale, out_refs, transposed_ref=None):
    tm = acc.shape[0]
    lane = lax.broadcasted_iota(jnp.int32, (tm, V7X_LANES), 1)
    first_half = (lane % HEAD_DIM) < (HEAD_DIM // 2)
    cos = cos_ref[...]
    sin = sin_ref[...]
    for c in range(ATTN_WIDTH // V7X_LANES):
        cols = slice(c * V7X_LANES, (c + 1) * V7X_LANES)
        xs = acc[:, cols]
        partner = jnp.where(first_half,
                            pltpu.roll(xs, V7X_LANES - HEAD_DIM // 2, 1),
                            pltpu.roll(xs, HEAD_DIM // 2, 1))
        y = xs * cos + partner * sin
        if scale != 1.0:
            y = y * scale
        for r in out_refs:
            r[:, cols] = y.astype(r.dtype)
        if transposed_ref is not None:
            transposed_ref[cols, :] = y.T


def _in_proj_kernel(x_ref, g_ref, sc_ref, sh_ref, w_ref, cos_ref, sin_ref, *rest, k_transposed):
    u_ref, q_ref, k_ref, kb_ref, v_ref, vb_ref, h_scr = rest[-7:]
    j = pl.program_id(1)

    @pl.when(j == 0)
    def _():
        y = _rms(x_ref[...], NORM_EPS) * g_ref[...]
        h_scr[...] = (y * (1.0 + sc_ref[...]) + sh_ref[...]).astype(BF16)

    acc = jnp.dot(h_scr[...], w_ref[...], preferred_element_type=F32)

    @pl.when(j == 0)
    def _():
        u_ref[...] = acc

    @pl.when(j == 1)
    def _():
        _rope_store(acc, cos_ref, sin_ref, SCORE_SCALE_LOG2, (q_ref,))

    @pl.when(j == 2)
    def _():
        if k_transposed:
            _rope_store(acc, cos_ref, sin_ref, 1.0, (kb_ref,), k_ref)
        else:
            _rope_store(acc, cos_ref, sin_ref, 1.0, (k_ref, kb_ref))

    @pl.when(j == 3)
    def _():
        v_ref[...] = acc
        vb_ref[...] = acc.astype(BF16)


def _in_proj(st, layer, x, mod, norm_g, w_in, cos_tab, sin_tab, kv_stacks):
    m = st.rows
    tm = TM
    tiles = m // tm
    tab_tiles = cos_tab.shape[0] // tm
    row = lambda i, j: (i, 0)
    layer_row = lambda i, j: (layer * tiles + i, 0)
    out_spec = pl.BlockSpec((tm, ATTN_WIDTH), row)
    stack_spec = pl.BlockSpec((tm, ATTN_WIDTH), layer_row)
    sds = lambda dt: jax.ShapeDtypeStruct((m, ATTN_WIDTH), dt)
    stack_sds = jax.ShapeDtypeStruct((DEPTH * m, ATTN_WIDTH), F32)
    out_specs = [out_spec, out_spec, stack_spec, out_spec, stack_spec, out_spec]
    out_shape = [sds(F32), sds(BF16), stack_sds, sds(BF16), stack_sds, sds(BF16)]
    if st.k_transposed:
        tiles_per_batch = st.seq // tm
        out_specs[2] = pl.BlockSpec(
            (None, ATTN_WIDTH, tm),
            lambda i, j: (layer * st.batch + i // tiles_per_batch, 0, i % tiles_per_batch))
        out_shape[2] = jax.ShapeDtypeStruct((DEPTH * st.batch, ATTN_WIDTH, st.seq), F32)
    in_specs = [
        pl.BlockSpec((tm, D_MODEL), row),
        pl.BlockSpec((None, 1, D_MODEL), lambda i, j: (layer, 0, 0)),
        _mod_spec(st, layer, MOD_SCALE_A, tm),
        _mod_spec(st, layer, MOD_SHIFT_A, tm),
        pl.BlockSpec((None, D_MODEL, ATTN_WIDTH), lambda i, j: (layer, 0, j)),
        pl.BlockSpec((tm, V7X_LANES), lambda i, j: (i % tab_tiles, 0)),
        pl.BlockSpec((tm, V7X_LANES), lambda i, j: (i % tab_tiles, 0)),
    ]
    args = [x, norm_g, mod, mod, w_in, cos_tab, sin_tab]
    aliases = {}
    if kv_stacks is not None:
        aliases = {len(args): 2, len(args) + 1: 4}
        in_specs += [pl.BlockSpec(memory_space=pl.ANY)] * 2
        args += list(kv_stacks)
    return pl.pallas_call(
        functools.partial(_in_proj_kernel, k_transposed=st.k_transposed),
        grid=(tiles, IN_COLS // ATTN_WIDTH),
        in_specs=in_specs,
        out_specs=out_specs,
        out_shape=out_shape,
        input_output_aliases=aliases,
        scratch_shapes=[pltpu.VMEM((tm, D_MODEL), BF16)],
        compiler_params=_params("arbitrary", "arbitrary"),
        name="in_proj",
    )(*args)


def _pool_kernel(u_ref, hist_ref, wp_ref, ps_ref, o_ref, ext_ref, *, tm, pos0):
    i = pl.program_id(1)

    @pl.when(i == 0)
    def _():
        ext_ref[0:HALO, :] = hist_ref[...]

    @pl.when(i > 0)
    def _():
        ext_ref[0:HALO, :] = ext_ref[tm:tm + HALO, :]

    ext_ref[HALO:, :] = u_ref[...]
    pos = pos0 + i * tm + lax.broadcasted_iota(jnp.int32, (tm, 1), 0)
    for g, w in enumerate(POOL_WINDOWS):
        cols = slice(g * POOL_GROUP, (g + 1) * POOL_GROUP)
        cur = u_ref[:, cols]
        s = cur
        for back in range(1, w):
            s = s + ext_ref[HALO - back:HALO - back + tm, cols]
        count = jnp.minimum(w, pos + 1).astype(F32)
        d = (s / count - cur).astype(BF16)
        y = jnp.dot(d, wp_ref[g], preferred_element_type=F32) * ps_ref[:, cols]
        o_ref[:, cols] = y.astype(BF16)


def _pool(st, layer, u, hist, w_pool, pool_scale):
    tm = min(TM, st.seq)
    tiles = st.seq // tm
    return pl.pallas_call(
        functools.partial(_pool_kernel, tm=tm, pos0=st.pos0),
        grid=(st.batch, tiles),
        in_specs=[
            pl.BlockSpec((tm, POOL_WIDTH), lambda b, i: (b * tiles + i, 0)),
            pl.BlockSpec((None, HALO, POOL_WIDTH), lambda b, i: (b, 0, 0)),
            pl.BlockSpec((None, N_POOL_GROUPS, POOL_GROUP, POOL_GROUP), lambda b, i: (layer, 0, 0, 0)),
            pl.BlockSpec((None, 1, POOL_WIDTH), lambda b, i: (layer, 0, 0)),
        ],
        out_specs=pl.BlockSpec((tm, POOL_WIDTH), lambda b, i: (b * tiles + i, 0)),
        out_shape=jax.ShapeDtypeStruct((st.rows, POOL_WIDTH), BF16),
        scratch_shapes=[pltpu.VMEM((tm + HALO, POOL_WIDTH), F32)],
        compiler_params=_params("arbitrary", "arbitrary"),
        name="pool_mix",
    )(u, hist, w_pool, pool_scale)


def _lambda(lam_ref, lam_init):
    lam = lam_ref[...]
    s1 = jnp.sum(lam[0:1] * lam[1:2], axis=-1, keepdims=True)
    s2 = jnp.sum(lam[2:3] * lam[3:4], axis=-1, keepdims=True)
    return jnp.exp(s1) - jnp.exp(s2) + lam_init


def _online_softmax_step(s, v, m_ref, l_ref, acc_ref):
    m_prev = m_ref[...]
    m_new = jnp.maximum(m_prev, jnp.max(s, axis=-1, keepdims=True))
    alpha = jnp.exp2(m_prev - m_new)
    p = jnp.exp2(s - m_new)
    l_ref[...] = alpha * l_ref[...] + jnp.sum(p, axis=-1, keepdims=True)
    acc_ref[...] = alpha * acc_ref[...] + jnp.dot(p.astype(BF16), v, preferred_element_type=F32)
    m_ref[...] = m_new


def _sub_ln(o1, o2, lam, g, lam_init, axis=-1):
    a = o1 - lam * o2
    return (_rms(a, SUBLN_EPS, axis) * g) * (1.0 - lam_init)


_NT = (((1,), (1,)), ((), ()))


def _prompt_attn_kernel(q_ref, k_ref, v_ref, lam_ref, g_ref, o_ref,
                        qz_ref, vt_ref, sa_ref, sb_ref, m_ref, l_ref, acc_ref, *, lam_init):
    i = pl.program_id(2)

    @pl.when(i == 0)
    def _():
        for t in range(vt_ref.shape[0]):
            vt_ref[t] = v_ref[t * TK:(t + 1) * TK, :].astype(F32).T.astype(BF16)

    q = q_ref[...]
    lane = lax.broadcasted_iota(jnp.int32, (TQ, V_DIM), 1)
    zero = jnp.zeros_like(q)
    qz_ref[0:TQ, :] = jnp.where(lane < HEAD_DIM, q, zero)
    qz_ref[TQ:, :] = jnp.where(lane >= HEAD_DIM, q, zero)
    m_ref[...] = jnp.full_like(m_ref, -jnp.inf)
    l_ref[...] = jnp.zeros_like(l_ref)
    acc_ref[...] = jnp.zeros_like(acc_ref)

    def scores(t, s_ref):
        start = pl.multiple_of(t * TK, TK)
        s_ref[...] = lax.dot_general(k_ref[pl.ds(start, TK), :], qz_ref[...], _NT,
                                     preferred_element_type=F32)

    def consume(t, s_ref, diag=None):
        q_lo = 0 if diag is None else diag * TK
        width = TQ - q_lo
        col_sets = (slice(q_lo, TQ), slice(TQ + q_lo, 2 * TQ))

        def gather(ref):
            if q_lo == 0:
                return ref[...]
            return jnp.concatenate([ref[:, c] for c in col_sets], axis=1)

        def scatter(ref, val):
            if q_lo == 0:
                ref[...] = val
            else:
                for n, c in enumerate(col_sets):
                    ref[:, c] = val[:, n * width:(n + 1) * width]

        s = gather(s_ref)
        if diag is not None:
            key = q_lo + lax.broadcasted_iota(jnp.int32, (TK, 2 * width), 0)
            col = lax.broadcasted_iota(jnp.int32, (TK, 2 * width), 1)
            query = q_lo + jnp.where(col >= width, col - width, col)
            s = jnp.where((key // CHUNK) <= (query // CHUNK), s, -jnp.inf)
        m_prev = gather(m_ref)
        m_new = jnp.maximum(m_prev, jnp.max(s, axis=0, keepdims=True))
        alpha = jnp.exp2(m_prev - m_new)
        p = jnp.exp2(s - m_new)
        scatter(l_ref, alpha * gather(l_ref) + jnp.sum(p, axis=0, keepdims=True))
        scatter(acc_ref, alpha * gather(acc_ref) + jnp.dot(vt_ref[t], p.astype(BF16),
                                                           preferred_element_type=F32))
        scatter(m_ref, m_new)

    scores(0, sa_ref)

    def pair(p, carry):
        t = 2 * p
        scores(t + 1, sb_ref)
        consume(t, sa_ref)
        scores(t + 2, sa_ref)
        consume(t + 1, sb_ref)
        return carry

    lax.fori_loop(0, i, pair, 0)
    first_diag = DIAG_TILES * i
    scores(first_diag + 1, sb_ref)
    consume(first_diag, sa_ref, diag=0)
    consume(first_diag + 1, sb_ref, diag=1)

    o = acc_ref[...] / l_ref[...]
    y = _sub_ln(o[:, 0:TQ], o[:, TQ:], _lambda(lam_ref, lam_init), g_ref[...], lam_init, axis=0)
    o_ref[...] = y.T.astype(BF16)


def _prompt_attn(st, layer, q, k, v, lam, subln_g, lam_init):
    assert DIAG_TILES == 2
    nq = st.seq // TQ
    return pl.pallas_call(
        functools.partial(_prompt_attn_kernel, lam_init=lam_init),
        grid=(st.batch, N_HEADS, nq),
        in_specs=[
            pl.BlockSpec((TQ, V_DIM), lambda b, h, i: (b * nq + i, h)),
            pl.BlockSpec((st.seq, V_DIM), lambda b, h, i: (b, h)),
            pl.BlockSpec((st.seq, V_DIM), lambda b, h, i: (b, h)),
            pl.BlockSpec((None, 4, HEAD_DIM), lambda b, h, i: (layer, 0, 0)),
            pl.BlockSpec((None, V_DIM, 1), lambda b, h, i: (layer, 0, 0)),
        ],
        out_specs=pl.BlockSpec((TQ, V_DIM), lambda b, h, i: (b * nq + i, h)),
        out_shape=jax.ShapeDtypeStruct((st.rows, ATTN_WIDTH), BF16),
        scratch_shapes=[
            pltpu.VMEM((2 * TQ, V_DIM), BF16),
            pltpu.VMEM((st.seq // TK, V_DIM, TK), BF16),
            pltpu.VMEM((TK, 2 * TQ), F32),
            pltpu.VMEM((TK, 2 * TQ), F32),
            pltpu.VMEM((1, 2 * TQ), F32),
            pltpu.VMEM((1, 2 * TQ), F32),
            pltpu.VMEM((V_DIM, 2 * TQ), F32),
        ],
        compiler_params=_params("arbitrary", "arbitrary", "arbitrary"),
        name="prompt_attn",
    )(q, k, v, lam, subln_g.reshape(DEPTH, V_DIM, 1))


def _decode_attn_kernel(q_ref, kc_ref, vc_ref, kn_ref, vn_ref, lam_ref, g_ref, o_ref,
                        qbd_ref, vbf_ref, m_ref, l_ref, acc_ref, *, lam_init):
    t = pl.program_id(1)
    n_score_heads = 2 * N_HEADS

    @pl.when(t == 0)
    def _():
        q = q_ref[...]
        col = lax.broadcasted_iota(jnp.int32, (DEC_SEQ, ATTN_WIDTH), 1)
        zero = jnp.zeros_like(q)
        for h in range(n_score_heads):
            qbd_ref[h * DEC_SEQ:(h + 1) * DEC_SEQ, :] = jnp.where(col // HEAD_DIM == h, q, zero)
        m_ref[...] = jnp.full_like(m_ref, -jnp.inf)
        l_ref[...] = jnp.zeros_like(l_ref)
        acc_ref[...] = jnp.zeros_like(acc_ref)

    s = jnp.dot(qbd_ref[...], kc_ref[...].astype(BF16), preferred_element_type=F32)
    for h in range(N_HEADS):
        vbf_ref[:, h * V_DIM:(h + 1) * V_DIM] = (
            vc_ref[pl.ds(h, TK_DEC, stride=N_HEADS), :].astype(BF16))
    _online_softmax_step(s, vbf_ref[...], m_ref, l_ref, acc_ref)

    @pl.when(t == pl.num_programs(1) - 1)
    def _():
        s_new = lax.dot_general(qbd_ref[...], kn_ref[...], _NT, preferred_element_type=F32)
        _online_softmax_step(s_new, vn_ref[...], m_ref, l_ref, acc_ref)
        o = acc_ref[...] / l_ref[...]
        lam = _lambda(lam_ref, lam_init)
        g = g_ref[...]
        for h in range(N_HEADS):
            cols = slice(h * V_DIM, (h + 1) * V_DIM)
            r0 = 2 * h * DEC_SEQ
            y = _sub_ln(o[r0:r0 + DEC_SEQ, cols], o[r0 + DEC_SEQ:r0 + 2 * DEC_SEQ, cols], lam, g, lam_init)
            o_ref[:, cols] = y.astype(BF16)


def _decode_attn(st, layer, q, k_new, v_new, cache_k, cache_v, lam, subln_g, lam_init):
    rows_bd = 2 * N_HEADS * DEC_SEQ
    tok = pl.BlockSpec((DEC_SEQ, ATTN_WIDTH), lambda b, t: (b, 0))
    tiles = PAST_LEN // TK_DEC
    cache_k_spec = pl.BlockSpec((None, ATTN_WIDTH, TK_DEC), lambda b, t: (layer * st.batch + b, 0, t))
    cache_v_spec = pl.BlockSpec((TK_DEC * N_HEADS, V_DIM),
                                lambda b, t: ((layer * st.batch + b) * tiles + t, 0))
    return pl.pallas_call(
        functools.partial(_decode_attn_kernel, lam_init=lam_init),
        grid=(st.batch, tiles),
        in_specs=[
            tok, cache_k_spec, cache_v_spec, tok, tok,
            pl.BlockSpec((None, 4, HEAD_DIM), lambda b, t: (layer, 0, 0)),
            pl.BlockSpec((None, 1, V_DIM), lambda b, t: (layer, 0, 0)),
        ],
        out_specs=tok,
        out_shape=jax.ShapeDtypeStruct((st.rows, ATTN_WIDTH), BF16),
        scratch_shapes=[
            pltpu.VMEM((rows_bd, ATTN_WIDTH), BF16),
            pltpu.VMEM((TK_DEC, ATTN_WIDTH), BF16),
            pltpu.VMEM((rows_bd, 1), F32),
            pltpu.VMEM((rows_bd, 1), F32),
            pltpu.VMEM((rows_bd, ATTN_WIDTH), F32),
        ],
        compiler_params=_params("arbitrary", "arbitrary"),
        name="decode_attn",
    )(q, cache_k, cache_v, k_new, v_new, lam, subln_g)


def _out_proj_kernel(x_ref, pool_ref, att_ref, w_ref, gate_ref, g_ref, sc_ref, sh_ref,
                     xo_ref, h_ref):
    mixed = (jnp.dot(pool_ref[...], w_ref[0:POOL_WIDTH, :], preferred_element_type=F32)
             + jnp.dot(att_ref[...], w_ref[POOL_WIDTH:, :], preferred_element_type=F32))
    x = x_ref[...] + gate_ref[...] * mixed
    xo_ref[...] = x
    y = _rms(x, NORM_EPS) * g_ref[...]
    h_ref[...] = (y * (1.0 + sc_ref[...]) + sh_ref[...]).astype(BF16)


def _out_proj(st, layer, x, pool_out, att, mod, w_out, norm_g):
    m = st.rows
    tm = TM
    row = lambda i, j: (i, 0)
    return pl.pallas_call(
        _out_proj_kernel,
        grid=(m // tm, 1),
        in_specs=[
            pl.BlockSpec((tm, D_MODEL), row),
            pl.BlockSpec((tm, POOL_WIDTH), row),
            pl.BlockSpec((tm, ATTN_WIDTH), row),
            pl.BlockSpec((None, D_MODEL, D_MODEL), lambda i, j: (layer, 0, 0),
                         pipeline_mode=pl.Buffered(1)),
            _mod_spec(st, layer, MOD_GATE_A, tm),
            pl.BlockSpec((None, 1, D_MODEL), lambda i, j: (layer, 0, 0)),
            _mod_spec(st, layer, MOD_SCALE_F, tm),
            _mod_spec(st, layer, MOD_SHIFT_F, tm),
        ],
        out_specs=[pl.BlockSpec((tm, D_MODEL), row), pl.BlockSpec((tm, D_MODEL), row)],
        out_shape=[jax.ShapeDtypeStruct((m, D_MODEL), F32), jax.ShapeDtypeStruct((m, D_MODEL), BF16)],
        compiler_params=_params("arbitrary", "arbitrary"),
        name="out_proj",
    )(x, pool_out, att, w_out, mod, norm_g, mod, mod)


def _gate_up_kernel(h_ref, wg_ref, wu_ref, o_ref):
    h = h_ref[...]
    g = jnp.dot(h, wg_ref[...], preferred_element_type=F32)
    u = jnp.dot(h, wu_ref[...], preferred_element_type=F32)
    o_ref[...] = (g * jax.nn.sigmoid(g) * u).astype(BF16)


def _gate_up(st, layer, h, w_gate, w_up):
    m = st.rows
    tm = min(TM_FFN, m)
    w_spec = pl.BlockSpec((None, D_MODEL, TF), lambda i, j: (layer, 0, j))
    return pl.pallas_call(
        _gate_up_kernel,
        grid=(m // tm, D_FF // TF),
        in_specs=[pl.BlockSpec((tm, D_MODEL), lambda i, j: (i, 0)), w_spec, w_spec],
        out_specs=pl.BlockSpec((tm, TF), lambda i, j: (i, j)),
        out_shape=jax.ShapeDtypeStruct((m, D_FF), BF16),
        compiler_params=_params("arbitrary", "arbitrary"),
        name="ffn_gate_up",
    )(h, w_gate, w_up)


def _down_kernel(a_ref, w_ref, x_ref, gate_ref, o_ref):
    y = jnp.dot(a_ref[...], w_ref[...], preferred_element_type=F32)
    o_ref[...] = x_ref[...] + gate_ref[...] * y


def _down(st, layer, act, x, mod, w_down):
    m = st.rows
    tm = min(TM_FFN, m)
    return pl.pallas_call(
        _down_kernel,
        grid=(m // tm, D_MODEL // TN_DOWN),
        in_specs=[
            pl.BlockSpec((tm, D_FF), lambda i, j: (i, 0)),
            pl.BlockSpec((None, D_FF, TN_DOWN), lambda i, j: (layer, 0, j)),
            pl.BlockSpec((tm, TN_DOWN), lambda i, j: (i, j)),
            _mod_spec(st, layer, MOD_GATE_F, tm, TN_DOWN),
        ],
        out_specs=pl.BlockSpec((tm, TN_DOWN), lambda i, j: (i, j)),
        out_shape=jax.ShapeDtypeStruct((m, D_MODEL), F32),
        compiler_params=_params("arbitrary", "arbitrary"),
        name="ffn_down",
    )(act, w_down, x, mod)


def _final_norm_kernel(x_ref, g_ref, o_ref):
    o_ref[...] = _rms(x_ref[...], NORM_EPS) * g_ref[...]


def _final_norm(x, g):
    m = x.shape[0]
    return pl.pallas_call(
        _final_norm_kernel,
        grid=(m // TM,),
        in_specs=[pl.BlockSpec((TM, D_MODEL), lambda i: (i, 0)), pl.BlockSpec((1, D_MODEL), lambda i: (0, 0))],
        out_specs=pl.BlockSpec((TM, D_MODEL), lambda i: (i, 0)),
        out_shape=jax.ShapeDtypeStruct((m, D_MODEL), F32),
        compiler_params=_params("arbitrary"),
        name="final_norm",
    )(x, g)


def _rope_tables(st):
    half = HEAD_DIM // 2
    inv = 1.0 / (ROPE_THETA ** (jnp.arange(half, dtype=F32) * (2.0 / HEAD_DIM)))
    pos = st.pos0 + jnp.arange(st.seq)
    ang = pos.astype(F32)[:, None] * inv[None, :]
    cos, sin = jnp.cos(ang), jnp.sin(ang)
    reps = V7X_LANES // HEAD_DIM
    cos_t = jnp.tile(jnp.concatenate([cos, cos], axis=-1), (1, reps))
    sin_t = jnp.tile(jnp.concatenate([-sin, sin], axis=-1), (1, reps))
    if st.seq < TM:
        cos_t = jnp.tile(cos_t, (TM // st.seq, 1))
        sin_t = jnp.tile(sin_t, (TM // st.seq, 1))
    return cos_t, sin_t


def _run_stream(st, x, mod, hist, caches, p):
    m = st.rows
    x = x.reshape(m, D_MODEL)
    cos_tab, sin_tab = _rope_tables(st)
    kv_stacks, us = None, []
    for l in range(DEPTH):
        lam_init = 0.8 - 0.6 * math.exp(-0.3 * l)
        u, q, k_stack, kb, v_stack, vb = _in_proj(st, l, x, mod, p["norm_mix_g"], p["w_in"],
                                                  cos_tab, sin_tab, kv_stacks)
        kv_stacks = (k_stack, v_stack)
        pool_out = _pool(st, l, u, hist[l], p["w_pool"], p["pool_scale"])
        if caches is None:
            att = _prompt_attn(st, l, q, kb, vb, p["lam"], p["subln_g"], lam_init)
        else:
            att = _decode_attn(st, l, q, kb, vb, caches[0], caches[1], p["lam"], p["subln_g"], lam_init)
        x, h = _out_proj(st, l, x, pool_out, att, mod, p["w_out"], p["norm_ffn_g"])
        act = _gate_up(st, l, h, p["w_gate"], p["w_up"])
        x = _down(st, l, act, x, mod, p["w_down"])
        us.append(u)
    y = _final_norm(x, p["final_g"]).reshape(st.batch, st.seq, D_MODEL)
    k_stack, v_stack = kv_stacks
    if st.k_transposed:
        k_out = k_stack.reshape(DEPTH, st.batch, 2 * N_HEADS, HEAD_DIM, st.seq)
        k_out = jnp.transpose(k_out, (0, 1, 4, 2, 3))
    else:
        k_out = k_stack.reshape(DEPTH, st.batch, st.seq, 2 * N_HEADS, HEAD_DIM)
    v_out = v_stack.reshape(DEPTH, st.batch, st.seq, N_HEADS, V_DIM)
    u_all = jnp.stack(us).reshape(DEPTH, st.batch, st.seq, POOL_WIDTH)
    return y, k_out, v_out, u_all[:, :, st.seq - POOL_HIST:]


def kernel(x_prompt, x_sample, c_prompt, c_sample, cache_k, cache_v, state_pool, w_mod, b_mod,
           norm_mix_g, w_in, w_pool, pool_scale, lam_q1, lam_k1, lam_q2, lam_k2, subln_g, w_out,
           norm_ffn_g, w_gate, w_up, w_down, final_g):
    p = {
        "norm_mix_g": norm_mix_g.reshape(DEPTH, 1, D_MODEL),
        "norm_ffn_g": norm_ffn_g.reshape(DEPTH, 1, D_MODEL),
        "final_g": final_g.reshape(1, D_MODEL),
        "w_in": w_in.astype(BF16),
        "w_pool": w_pool.astype(BF16),
        "pool_scale": pool_scale.reshape(DEPTH, 1, POOL_WIDTH),
        "lam": jnp.stack([lam_q1, lam_k1, lam_q2, lam_k2], axis=1),
        "subln_g": subln_g.reshape(DEPTH, 1, V_DIM),
        "w_out": w_out.astype(BF16),
        "w_gate": w_gate.astype(BF16),
        "w_up": w_up.astype(BF16),
        "w_down": w_down.astype(BF16),
    }
    c_all = jnp.concatenate([c_prompt, c_sample], axis=0)
    c_all = jnp.pad(c_all, ((0, -c_all.shape[0] % 8), (0, 0)))
    mod = _modulation(c_all, w_mod, b_mod)[:, :BATCH + DEC_BATCH]
    mod_prompt = mod[:, :BATCH].reshape(DEPTH * BATCH, 1, N_MOD * D_MODEL)
    mod_sample = jnp.repeat(mod[:, BATCH:], DEC_SEQ, axis=1)

    hist_prompt = jnp.zeros((DEPTH, BATCH, HALO, POOL_WIDTH), F32)
    hist_sample = jnp.pad(state_pool, ((0, 0), (0, 0), (HALO - POOL_HIST, 0), (0, 0)))
    caches = (jnp.transpose(cache_k, (0, 1, 3, 4, 2)).reshape(DEPTH * DEC_BATCH, ATTN_WIDTH, PAST_LEN),
              cache_v.reshape(-1, V_DIM))

    y_p, k_p, v_p, pool_p = _run_stream(PROMPT, x_prompt, mod_prompt, hist_prompt, None, p)
    y_s, k_s, v_s, pool_s = _run_stream(SAMPLE, x_sample, mod_sample, hist_sample, caches, p)
    return (y_p, y_s, k_p, v_p, pool_p, k_s, v_s, pool_s)
```

```python
import functools
import math
from typing import NamedTuple

import jax
import jax.numpy as jnp
from jax import lax
from jax.experimental import pallas as pl
from jax.experimental.pallas import tpu as pltpu

F32 = jnp.float32
BF16 = jnp.bfloat16

D_MODEL = 2048
BATCH = 4
SEQ = 4096
DEPTH = 4
DEC_BATCH = 32
DEC_SEQ = 16
PAST_LEN = 2048
CHUNK = 64
POOL_WIDTH = D_MODEL // 2
N_POOL_GROUPS = 4
POOL_GROUP = POOL_WIDTH // N_POOL_GROUPS
POOL_WINDOWS = (2, 4, 8, 16)
POOL_HIST = max(POOL_WINDOWS) - 1
HALO = POOL_HIST + 1
ATTN_WIDTH = D_MODEL - POOL_WIDTH
HEAD_DIM = 64
N_HEADS = ATTN_WIDTH // (2 * HEAD_DIM)
V_DIM = 2 * HEAD_DIM
IN_COLS = POOL_WIDTH + 3 * ATTN_WIDTH
IN_SECTIONS = IN_COLS // ATTN_WIDTH
D_FF = -(-8 * D_MODEL // (3 * 256)) * 256
ROPE_THETA = 10000.0
NORM_EPS = 1e-6
SUBLN_EPS = 1e-5
N_MOD = 6
MOD_SHIFT_A, MOD_SCALE_A, MOD_GATE_A, MOD_SHIFT_F, MOD_SCALE_F, MOD_GATE_F = range(N_MOD)

V7X_VMEM_BYTES = 64 * 1024 * 1024
V7X_LANES = 128
VMEM_LIMIT = V7X_VMEM_BYTES * 7 // 8

TM = 512
TM_FFN = 1024
TN_MOD = 1024
TF = 512
TN_DOWN = 512
TQ = 1024
TK = 512
DIAG_TILES = TQ // TK
SCORE_SCALE_LOG2 = HEAD_DIM ** -0.5 * math.log2(math.e)
TK_DEC = 512


class Stream(NamedTuple):
    batch: int
    seq: int
    pos0: int
    mod_rows: int
    k_transposed: bool

    @property
    def rows(self):
        return self.batch * self.seq


PROMPT = Stream(BATCH, SEQ, 0, 1, True)
SAMPLE = Stream(DEC_BATCH, DEC_SEQ, PAST_LEN, DEC_BATCH * DEC_SEQ, False)


def _params(*semantics, flags=None):
    return pltpu.CompilerParams(dimension_semantics=semantics, vmem_limit_bytes=VMEM_LIMIT, flags=flags)


def _rms(x, eps, axis=-1):
    return x * lax.rsqrt(jnp.mean(x * x, axis=axis, keepdims=True) + eps)


def _mod_spec(st, layer, which, tm, tn=D_MODEL):
    per_chunk = D_MODEL // tn
    if st.mod_rows == 1:
        tiles_per_batch = st.seq // tm
        return pl.BlockSpec(
            (None, 1, tn),
            lambda i, j: (layer * st.batch + i // tiles_per_batch, 0, which * per_chunk + j % per_chunk))
    return pl.BlockSpec((None, tm, tn), lambda i, j: (layer, i, which * per_chunk + j % per_chunk))


def _mod_kernel(c_ref, w_ref, b_ref, o_ref):
    c = c_ref[...]
    cs = (c * jax.nn.sigmoid(c)).astype(BF16)
    o_ref[...] = jnp.dot(cs, w_ref[...].astype(BF16), preferred_element_type=F32) + b_ref[...]


def _modulation(c_all, w_mod, b_mod):
    nb = c_all.shape[0]
    n = N_MOD * D_MODEL
    return pl.pallas_call(
        _mod_kernel,
        grid=(DEPTH, n // TN_MOD),
        in_specs=[
            pl.BlockSpec((nb, D_MODEL), lambda l, j: (0, 0)),
            pl.BlockSpec((None, D_MODEL, TN_MOD), lambda l, j: (l, 0, j)),
            pl.BlockSpec((None, 1, TN_MOD), lambda l, j: (l, 0, j)),
        ],
        out_specs=pl.BlockSpec((None, nb, TN_MOD), lambda l, j: (l, 0, j)),
        out_shape=jax.ShapeDtypeStruct((DEPTH, nb, n), F32),
        compiler_params=_params("arbitrary", "arbitrary"),
        name="modulation",
    )(c_all, w_mod, b_mod.reshape(DEPTH, 1, n))


def _rope_store(acc, cos_ref, sin_ref, scale, out_refs, transposed_ref=None):
    tm = acc.shape[0]
    lane = lax.broadcasted_iota(jnp.int32, (tm, V7X_LANES), 1)
    first_half = (lane % HEAD_DIM) < (HEAD_DIM // 2)
    cos = cos_ref[...]
    sin = sin_ref[...]
    for c in range(ATTN_WIDTH // V7X_LANES):
        cols = slice(c * V7X_LANES, (c + 1) * V7X_LANES)
        xs = acc[:, cols]
        partner = jnp.where(first_half,
                            pltpu.roll(xs, V7X_LANES - HEAD_DIM // 2, 1),
                            pltpu.roll(xs, HEAD_DIM // 2, 1))
        y = xs * cos + partner * sin
        if scale != 1.0:
            y = y * scale
        for r in out_refs:
            r[:, cols] = y.astype(r.dtype)
        if transposed_ref is not None:
            transposed_ref[cols, :] = y.T


def _in_proj_kernel(x_ref, g_ref, sc_ref, sh_ref, w_ref, cos_ref, sin_ref, *rest,
                    k_transposed, tiles):
    u_ref, q_ref, k_ref, kb_ref, v_ref, vb_ref, h_cur, h_next, acc_a, acc_b = rest[-10:]
    s = pl.program_id(0)
    section = s % IN_SECTIONS
    last = IN_SECTIONS * tiles

    def normalise(dst):
        y = _rms(x_ref[...], NORM_EPS) * g_ref[...]
        dst[...] = (y * (1.0 + sc_ref[...]) + sh_ref[...]).astype(BF16)

    def multiply(dst):
        dst[...] = jnp.dot(h_cur[...], w_ref[...], preferred_element_type=F32)

    def finish_v(acc):
        v_ref[...] = acc[...]
        vb_ref[...] = acc[...].astype(BF16)

    @pl.when(s == 0)
    def _():
        normalise(h_cur)
        multiply(acc_a)

    @pl.when((section == 0) & (s > 0) & (s < last))
    def _():
        h_cur[...] = h_next[...]
        multiply(acc_a)
        finish_v(acc_b)

    @pl.when(section == 1)
    def _():
        multiply(acc_b)
        u_ref[...] = acc_a[...]
        normalise(h_next)

    @pl.when(section == 2)
    def _():
        multiply(acc_a)
        _rope_store(acc_b, cos_ref, sin_ref, SCORE_SCALE_LOG2, (q_ref,))

    @pl.when(section == 3)
    def _():
        multiply(acc_b)
        if k_transposed:
            _rope_store(acc_a, cos_ref, sin_ref, 1.0, (kb_ref,), k_ref)
        else:
            _rope_store(acc_a, cos_ref, sin_ref, 1.0, (k_ref, kb_ref))

    @pl.when(s == last)
    def _():
        finish_v(acc_b)


def _in_proj(st, layer, x, mod, norm_g, w_in, cos_tab, sin_tab, kv_stacks):
    m = st.rows
    tm = TM
    tiles = m // tm
    tab_tiles = cos_tab.shape[0] // tm
    steps = IN_SECTIONS * tiles + 1
    done = lambda s: jnp.maximum(s - 1, 0) // IN_SECTIONS
    normed = lambda s: jnp.minimum((s + IN_SECTIONS - 1) // IN_SECTIONS, tiles - 1)
    out_spec = pl.BlockSpec((tm, ATTN_WIDTH), lambda s: (done(s), 0))
    stack_spec = pl.BlockSpec((tm, ATTN_WIDTH), lambda s: (layer * tiles + done(s), 0))
    sds = lambda dt: jax.ShapeDtypeStruct((m, ATTN_WIDTH), dt)
    stack_sds = jax.ShapeDtypeStruct((DEPTH * m, ATTN_WIDTH), F32)
    out_specs = [out_spec, out_spec, stack_spec, out_spec, stack_spec, out_spec]
    out_shape = [sds(F32), sds(BF16), stack_sds, sds(BF16), stack_sds, sds(BF16)]
    if st.k_transposed:
        tiles_per_batch = st.seq // tm
        out_specs[2] = pl.BlockSpec(
            (None, ATTN_WIDTH, tm),
            lambda s: (layer * st.batch + done(s) // tiles_per_batch, 0, done(s) % tiles_per_batch))
        out_shape[2] = jax.ShapeDtypeStruct((DEPTH * st.batch, ATTN_WIDTH, st.seq), F32)
    if st.mod_rows == 1:
        tiles_per_batch = st.seq // tm
        mod_spec = lambda which: pl.BlockSpec(
            (None, 1, D_MODEL), lambda s: (layer * st.batch + normed(s) // tiles_per_batch, 0, which))
    else:
        mod_spec = lambda which: pl.BlockSpec((None, tm, D_MODEL), lambda s: (layer, normed(s), which),
                                              pipeline_mode=pl.Buffered(1))
    in_specs = [
        pl.BlockSpec((tm, D_MODEL), lambda s: (normed(s), 0)),
        pl.BlockSpec((None, 1, D_MODEL), lambda s: (layer, 0, 0)),
        mod_spec(MOD_SCALE_A),
        mod_spec(MOD_SHIFT_A),
        pl.BlockSpec((None, D_MODEL, ATTN_WIDTH),
                     lambda s: (layer, 0, jnp.minimum(s, steps - 2) % IN_SECTIONS)),
        pl.BlockSpec((tm, V7X_LANES), lambda s: (done(s) % tab_tiles, 0)),
        pl.BlockSpec((tm, V7X_LANES), lambda s: (done(s) % tab_tiles, 0)),
    ]
    args = [x, norm_g, mod, mod, w_in, cos_tab, sin_tab]
    aliases = {}
    if kv_stacks is not None:
        aliases = {len(args): 2, len(args) + 1: 4}
        in_specs += [pl.BlockSpec(memory_space=pl.ANY)] * 2
        args += list(kv_stacks)
    return pl.pallas_call(
        functools.partial(_in_proj_kernel, k_transposed=st.k_transposed, tiles=tiles),
        grid=(steps,),
        in_specs=in_specs,
        out_specs=out_specs,
        out_shape=out_shape,
        input_output_aliases=aliases,
        scratch_shapes=[pltpu.VMEM((tm, D_MODEL), BF16), pltpu.VMEM((tm, D_MODEL), BF16),
                        pltpu.VMEM((tm, ATTN_WIDTH), F32), pltpu.VMEM((tm, ATTN_WIDTH), F32)],
        compiler_params=_params("arbitrary"),
        name="in_proj",
    )(*args)


def _pool_kernel(u_ref, hist_ref, wp_ref, ps_ref, o_ref, ext_ref, *, tm, pos0):
    i = pl.program_id(1)

    @pl.when(i == 0)
    def _():
        ext_ref[0:HALO, :] = hist_ref[...]

    @pl.when(i > 0)
    def _():
        ext_ref[0:HALO, :] = ext_ref[tm:tm + HALO, :]

    ext_ref[HALO:, :] = u_ref[...]
    pos = pos0 + i * tm + lax.broadcasted_iota(jnp.int32, (tm, 1), 0)
    for g, w in enumerate(POOL_WINDOWS):
        cols = slice(g * POOL_GROUP, (g + 1) * POOL_GROUP)
        cur = u_ref[:, cols]
        s = cur
        for back in range(1, w):
            s = s + ext_ref[HALO - back:HALO - back + tm, cols]
        count = jnp.minimum(w, pos + 1).astype(F32)
        d = (s / count - cur).astype(BF16)
        y = jnp.dot(d, wp_ref[g], preferred_element_type=F32) * ps_ref[:, cols]
        o_ref[:, cols] = y.astype(BF16)


def _pool(st, layer, u, hist, w_pool, pool_scale):
    tm = min(TM, st.seq)
    tiles = st.seq // tm
    return pl.pallas_call(
        functools.partial(_pool_kernel, tm=tm, pos0=st.pos0),
        grid=(st.batch, tiles),
        in_specs=[
            pl.BlockSpec((tm, POOL_WIDTH), lambda b, i: (b * tiles + i, 0)),
            pl.BlockSpec((None, HALO, POOL_WIDTH), lambda b, i: (b, 0, 0)),
            pl.BlockSpec((None, N_POOL_GROUPS, POOL_GROUP, POOL_GROUP), lambda b, i: (layer, 0, 0, 0)),
            pl.BlockSpec((None, 1, POOL_WIDTH), lambda b, i: (layer, 0, 0)),
        ],
        out_specs=pl.BlockSpec((tm, POOL_WIDTH), lambda b, i: (b * tiles + i, 0)),
        out_shape=jax.ShapeDtypeStruct((st.rows, POOL_WIDTH), BF16),
        scratch_shapes=[pltpu.VMEM((tm + HALO, POOL_WIDTH), F32)],
        compiler_params=_params("arbitrary", "arbitrary"),
        name="pool_mix",
    )(u, hist, w_pool, pool_scale)


def _lambda(lam_ref, lam_init):
    lam = lam_ref[...]
    s1 = jnp.sum(lam[0:1] * lam[1:2], axis=-1, keepdims=True)
    s2 = jnp.sum(lam[2:3] * lam[3:4], axis=-1, keepdims=True)
    return jnp.exp(s1) - jnp.exp(s2) + lam_init


def _online_softmax_step(s, v, m_ref, l_ref, acc_ref):
    m_prev = m_ref[...]
    m_new = jnp.maximum(m_prev, jnp.max(s, axis=-1, keepdims=True))
    alpha = jnp.exp2(m_prev - m_new)
    p = jnp.exp2(s - m_new)
    l_ref[...] = alpha * l_ref[...] + jnp.sum(p, axis=-1, keepdims=True)
    acc_ref[...] = alpha * acc_ref[...] + jnp.dot(p.astype(BF16), v, preferred_element_type=F32)
    m_ref[...] = m_new


def _sub_ln(o1, o2, lam, g, lam_init, axis=-1):
    a = o1 - lam * o2
    return (_rms(a, SUBLN_EPS, axis) * g) * (1.0 - lam_init)


_NT = (((1,), (1,)), ((), ()))


def _prompt_attn_kernel(q_ref, k_ref, v_ref, lam_ref, g_ref, o_ref,
                        qz_ref, vt_ref, sa_ref, sb_ref, m_ref, l_ref, acc_ref, *, lam_init):
    i = pl.program_id(2)

    @pl.when(i == 0)
    def _():
        for t in range(vt_ref.shape[0]):
            vt_ref[t] = v_ref[t * TK:(t + 1) * TK, :].astype(F32).T.astype(BF16)

    q = q_ref[...]
    lane = lax.broadcasted_iota(jnp.int32, (TQ, V_DIM), 1)
    zero = jnp.zeros_like(q)
    qz_ref[0:TQ, :] = jnp.where(lane < HEAD_DIM, q, zero)
    qz_ref[TQ:, :] = jnp.where(lane >= HEAD_DIM, q, zero)
    m_ref[...] = jnp.full_like(m_ref, -jnp.inf)
    l_ref[...] = jnp.zeros_like(l_ref)
    acc_ref[...] = jnp.zeros_like(acc_ref)

    def scores(t, s_ref):
        start = pl.multiple_of(t * TK, TK)
        s_ref[...] = lax.dot_general(k_ref[pl.ds(start, TK), :], qz_ref[...], _NT,
                                     preferred_element_type=F32)

    def consume(t, s_ref, diag=None):
        q_lo = 0 if diag is None else diag * TK
        width = TQ - q_lo
        col_sets = (slice(q_lo, TQ), slice(TQ + q_lo, 2 * TQ))

        def gather(ref):
            if q_lo == 0:
                return ref[...]
            return jnp.concatenate([ref[:, c] for c in col_sets], axis=1)

        def scatter(ref, val):
            if q_lo == 0:
                ref[...] = val
            else:
                for n, c in enumerate(col_sets):
                    ref[:, c] = val[:, n * width:(n + 1) * width]

        s = gather(s_ref)
        if diag is not None:
            key = q_lo + lax.broadcasted_iota(jnp.int32, (TK, 2 * width), 0)
            col = lax.broadcasted_iota(jnp.int32, (TK, 2 * width), 1)
            query = q_lo + jnp.where(col >= width, col - width, col)
            s = jnp.where((key // CHUNK) <= (query // CHUNK), s, -jnp.inf)
        m_prev = gather(m_ref)
        m_new = jnp.maximum(m_prev, jnp.max(s, axis=0, keepdims=True))
        alpha = jnp.exp2(m_prev - m_new)
        p = jnp.exp2(s - m_new)
        scatter(l_ref, alpha * gather(l_ref) + jnp.sum(p, axis=0, keepdims=True))
        scatter(acc_ref, alpha * gather(acc_ref) + jnp.dot(vt_ref[t], p.astype(BF16),
                                                           preferred_element_type=F32))
        scatter(m_ref, m_new)

    scores(0, sa_ref)

    def pair(p, carry):
        t = 2 * p
        scores(t + 1, sb_ref)
        consume(t, sa_ref)
        scores(t + 2, sa_ref)
        consume(t + 1, sb_ref)
        return carry

    lax.fori_loop(0, i, pair, 0)
    first_diag = DIAG_TILES * i
    scores(first_diag + 1, sb_ref)
    consume(first_diag, sa_ref, diag=0)
    consume(first_diag + 1, sb_ref, diag=1)

    o = acc_ref[...] / l_ref[...]
    y = _sub_ln(o[:, 0:TQ], o[:, TQ:], _lambda(lam_ref, lam_init), g_ref[...], lam_init, axis=0)
    o_ref[...] = y.T.astype(BF16)


def _prompt_attn(st, layer, q, k, v, lam, subln_g, lam_init):
    assert DIAG_TILES == 2
    nq = st.seq // TQ
    return pl.pallas_call(
        functools.partial(_prompt_attn_kernel, lam_init=lam_init),
        grid=(st.batch, N_HEADS, nq),
        in_specs=[
            pl.BlockSpec((TQ, V_DIM), lambda b, h, i: (b * nq + i, h)),
            pl.BlockSpec((st.seq, V_DIM), lambda b, h, i: (b, h)),
            pl.BlockSpec((st.seq, V_DIM), lambda b, h, i: (b, h)),
            pl.BlockSpec((None, 4, HEAD_DIM), lambda b, h, i: (layer, 0, 0)),
            pl.BlockSpec((None, V_DIM, 1), lambda b, h, i: (layer, 0, 0)),
        ],
        out_specs=pl.BlockSpec((TQ, V_DIM), lambda b, h, i: (b * nq + i, h)),
        out_shape=jax.ShapeDtypeStruct((st.rows, ATTN_WIDTH), BF16),
        scratch_shapes=[
            pltpu.VMEM((2 * TQ, V_DIM), BF16),
            pltpu.VMEM((st.seq // TK, V_DIM, TK), BF16),
            pltpu.VMEM((TK, 2 * TQ), F32),
            pltpu.VMEM((TK, 2 * TQ), F32),
            pltpu.VMEM((1, 2 * TQ), F32),
            pltpu.VMEM((1, 2 * TQ), F32),
            pltpu.VMEM((V_DIM, 2 * TQ), F32),
        ],
        compiler_params=_params("arbitrary", "arbitrary", "arbitrary"),
        name="prompt_attn",
    )(q, k, v, lam, subln_g.reshape(DEPTH, V_DIM, 1))


def _decode_attn_kernel(q_ref, kc_ref, vc_ref, kn_ref, vn_ref, lam_ref, g_ref, o_ref,
                        qbd_ref, vbf_ref, m_ref, l_ref, acc_ref, *, lam_init):
    t = pl.program_id(1)
    n_score_heads = 2 * N_HEADS

    @pl.when(t == 0)
    def _():
        q = q_ref[...]
        col = lax.broadcasted_iota(jnp.int32, (DEC_SEQ, ATTN_WIDTH), 1)
        zero = jnp.zeros_like(q)
        for h in range(n_score_heads):
            qbd_ref[h * DEC_SEQ:(h + 1) * DEC_SEQ, :] = jnp.where(col // HEAD_DIM == h, q, zero)
        m_ref[...] = jnp.full_like(m_ref, -jnp.inf)
        l_ref[...] = jnp.zeros_like(l_ref)
        acc_ref[...] = jnp.zeros_like(acc_ref)

    s = jnp.dot(qbd_ref[...], kc_ref[...].astype(BF16), preferred_element_type=F32)
    for h in range(N_HEADS):
        vbf_ref[:, h * V_DIM:(h + 1) * V_DIM] = (
            vc_ref[pl.ds(h, TK_DEC, stride=N_HEADS), :].astype(BF16))
    _online_softmax_step(s, vbf_ref[...], m_ref, l_ref, acc_ref)

    @pl.when(t == pl.num_programs(1) - 1)
    def _():
        s_new = lax.dot_general(qbd_ref[...], kn_ref[...], _NT, preferred_element_type=F32)
        _online_softmax_step(s_new, vn_ref[...], m_ref, l_ref, acc_ref)
        o = acc_ref[...] / l_ref[...]
        lam = _lambda(lam_ref, lam_init)
        g = g_ref[...]
        for h in range(N_HEADS):
            cols = slice(h * V_DIM, (h + 1) * V_DIM)
            r0 = 2 * h * DEC_SEQ
            y = _sub_ln(o[r0:r0 + DEC_SEQ, cols], o[r0 + DEC_SEQ:r0 + 2 * DEC_SEQ, cols], lam, g, lam_init)
            o_ref[:, cols] = y.astype(BF16)


def _decode_attn(st, layer, q, k_new, v_new, cache_k, cache_v, lam, subln_g, lam_init):
    rows_bd = 2 * N_HEADS * DEC_SEQ
    tok = pl.BlockSpec((DEC_SEQ, ATTN_WIDTH), lambda b, t: (b, 0))
    tiles = PAST_LEN // TK_DEC
    cache_k_spec = pl.BlockSpec((None, ATTN_WIDTH, TK_DEC), lambda b, t: (layer * st.batch + b, 0, t))
    cache_v_spec = pl.BlockSpec((TK_DEC * N_HEADS, V_DIM),
                                lambda b, t: ((layer * st.batch + b) * tiles + t, 0))
    return pl.pallas_call(
        functools.partial(_decode_attn_kernel, lam_init=lam_init),
        grid=(st.batch, tiles),
        in_specs=[
            tok, cache_k_spec, cache_v_spec, tok, tok,
            pl.BlockSpec((None, 4, HEAD_DIM), lambda b, t: (layer, 0, 0)),
            pl.BlockSpec((None, 1, V_DIM), lambda b, t: (layer, 0, 0)),
        ],
        out_specs=tok,
        out_shape=jax.ShapeDtypeStruct((st.rows, ATTN_WIDTH), BF16),
        scratch_shapes=[
            pltpu.VMEM((rows_bd, ATTN_WIDTH), BF16),
            pltpu.VMEM((TK_DEC, ATTN_WIDTH), BF16),
            pltpu.VMEM((rows_bd, 1), F32),
            pltpu.VMEM((rows_bd, 1), F32),
            pltpu.VMEM((rows_bd, ATTN_WIDTH), F32),
        ],
        compiler_params=_params("arbitrary", "arbitrary"),
        name="decode_attn",
    )(q, cache_k, cache_v, k_new, v_new, lam, subln_g)


def _out_proj_kernel(x_ref, pool_ref, att_ref, w_ref, gate_ref, g_ref, sc_ref, sh_ref,
                     xo_ref, h_ref):
    mixed = (jnp.dot(pool_ref[...], w_ref[0:POOL_WIDTH, :], preferred_element_type=F32)
             + jnp.dot(att_ref[...], w_ref[POOL_WIDTH:, :], preferred_element_type=F32))
    x = x_ref[...] + gate_ref[...] * mixed
    xo_ref[...] = x
    y = _rms(x, NORM_EPS) * g_ref[...]
    h_ref[...] = (y * (1.0 + sc_ref[...]) + sh_ref[...]).astype(BF16)


def _out_proj(st, layer, x, pool_out, att, mod, w_out, norm_g):
    m = st.rows
    tm = TM
    row = lambda i, j: (i, 0)
    return pl.pallas_call(
        _out_proj_kernel,
        grid=(m // tm, 1),
        in_specs=[
            pl.BlockSpec((tm, D_MODEL), row),
            pl.BlockSpec((tm, POOL_WIDTH), row),
            pl.BlockSpec((tm, ATTN_WIDTH), row),
            pl.BlockSpec((None, D_MODEL, D_MODEL), lambda i, j: (layer, 0, 0),
                         pipeline_mode=pl.Buffered(1)),
            _mod_spec(st, layer, MOD_GATE_A, tm),
            pl.BlockSpec((None, 1, D_MODEL), lambda i, j: (layer, 0, 0)),
            _mod_spec(st, layer, MOD_SCALE_F, tm),
            _mod_spec(st, layer, MOD_SHIFT_F, tm),
        ],
        out_specs=[pl.BlockSpec((tm, D_MODEL), row), pl.BlockSpec((tm, D_MODEL), row)],
        out_shape=[jax.ShapeDtypeStruct((m, D_MODEL), F32), jax.ShapeDtypeStruct((m, D_MODEL), BF16)],
        compiler_params=_params("arbitrary", "arbitrary"),
        name="out_proj",
    )(x, pool_out, att, w_out, mod, norm_g, mod, mod)


def _gate_up_kernel(h_ref, wg_ref, wu_ref, o_ref):
    h = h_ref[...]
    g = jnp.dot(h, wg_ref[...], preferred_element_type=F32)
    u = jnp.dot(h, wu_ref[...], preferred_element_type=F32)
    o_ref[...] = (g * jax.nn.sigmoid(g) * u).astype(BF16)


def _gate_up(st, layer, h, w_gate, w_up):
    m = st.rows
    tm = min(TM_FFN, m)
    w_spec = pl.BlockSpec((None, D_MODEL, TF), lambda i, j: (layer, 0, j))
    return pl.pallas_call(
        _gate_up_kernel,
        grid=(m // tm, D_FF // TF),
        in_specs=[pl.BlockSpec((tm, D_MODEL), lambda i, j: (i, 0)), w_spec, w_spec],
        out_specs=pl.BlockSpec((tm, TF), lambda i, j: (i, j)),
        out_shape=jax.ShapeDtypeStruct((m, D_FF), BF16),
        compiler_params=_params("arbitrary", "arbitrary"),
        name="ffn_gate_up",
    )(h, w_gate, w_up)


def _down_kernel(a_ref, w_ref, x_ref, gate_ref, o_ref):
    y = jnp.dot(a_ref[...], w_ref[...], preferred_element_type=F32)
    o_ref[...] = x_ref[...] + gate_ref[...] * y


def _down(st, layer, act, x, mod, w_down):
    m = st.rows
    tm = min(TM_FFN, m)
    return pl.pallas_call(
        _down_kernel,
        grid=(m // tm, D_MODEL // TN_DOWN),
        in_specs=[
            pl.BlockSpec((tm, D_FF), lambda i, j: (i, 0)),
            pl.BlockSpec((None, D_FF, TN_DOWN), lambda i, j: (layer, 0, j)),
            pl.BlockSpec((tm, TN_DOWN), lambda i, j: (i, j)),
            _mod_spec(st, layer, MOD_GATE_F, tm, TN_DOWN),
        ],
        out_specs=pl.BlockSpec((tm, TN_DOWN), lambda i, j: (i, j)),
        out_shape=jax.ShapeDtypeStruct((m, D_MODEL), F32),
        compiler_params=_params("arbitrary", "arbitrary"),
        name="ffn_down",
    )(act, w_down, x, mod)


def _final_norm_kernel(x_ref, g_ref, o_ref):
    o_ref[...] = _rms(x_ref[...], NORM_EPS) * g_ref[...]


def _final_norm(x, g):
    m = x.shape[0]
    return pl.pallas_call(
        _final_norm_kernel,
        grid=(m // TM,),
        in_specs=[pl.BlockSpec((TM, D_MODEL), lambda i: (i, 0)), pl.BlockSpec((1, D_MODEL), lambda i: (0, 0))],
        out_specs=pl.BlockSpec((TM, D_MODEL), lambda i: (i, 0)),
        out_shape=jax.ShapeDtypeStruct((m, D_MODEL), F32),
        compiler_params=_params("arbitrary"),
        name="final_norm",
    )(x, g)


def _rope_tables(st):
    half = HEAD_DIM // 2
    inv = 1.0 / (ROPE_THETA ** (jnp.arange(half, dtype=F32) * (2.0 / HEAD_DIM)))
    pos = st.pos0 + jnp.arange(st.seq)
    ang = pos.astype(F32)[:, None] * inv[None, :]
    cos, sin = jnp.cos(ang), jnp.sin(ang)
    reps = V7X_LANES // HEAD_DIM
    cos_t = jnp.tile(jnp.concatenate([cos, cos], axis=-1), (1, reps))
    sin_t = jnp.tile(jnp.concatenate([-sin, sin], axis=-1), (1, reps))
    if st.seq < TM:
        cos_t = jnp.tile(cos_t, (TM // st.seq, 1))
        sin_t = jnp.tile(sin_t, (TM // st.seq, 1))
    return cos_t, sin_t


def _run_stream(st, x, mod, hist, caches, p):
    m = st.rows
    x = x.reshape(m, D_MODEL)
    cos_tab, sin_tab = _rope_tables(st)
    kv_stacks, us = None, []
    for l in range(DEPTH):
        lam_init = 0.8 - 0.6 * math.exp(-0.3 * l)
        u, q, k_stack, kb, v_stack, vb = _in_proj(st, l, x, mod, p["norm_mix_g"], p["w_in"],
                                                  cos_tab, sin_tab, kv_stacks)
        kv_stacks = (k_stack, v_stack)
        pool_out = _pool(st, l, u, hist[l], p["w_pool"], p["pool_scale"])
        if caches is None:
            att = _prompt_attn(st, l, q, kb, vb, p["lam"], p["subln_g"], lam_init)
        else:
            att = _decode_attn(st, l, q, kb, vb, caches[0], caches[1], p["lam"], p["subln_g"], lam_init)
        x, h = _out_proj(st, l, x, pool_out, att, mod, p["w_out"], p["norm_ffn_g"])
        act = _gate_up(st, l, h, p["w_gate"], p["w_up"])
        x = _down(st, l, act, x, mod, p["w_down"])
        us.append(u)
    y = _final_norm(x, p["final_g"]).reshape(st.batch, st.seq, D_MODEL)
    k_stack, v_stack = kv_stacks
    if st.k_transposed:
        k_out = k_stack.reshape(DEPTH, st.batch, 2 * N_HEADS, HEAD_DIM, st.seq)
        k_out = jnp.transpose(k_out, (0, 1, 4, 2, 3))
    else:
        k_out = k_stack.reshape(DEPTH, st.batch, st.seq, 2 * N_HEADS, HEAD_DIM)
    v_out = v_stack.reshape(DEPTH, st.batch, st.seq, N_HEADS, V_DIM)
    u_all = jnp.stack(us).reshape(DEPTH, st.batch, st.seq, POOL_WIDTH)
    return y, k_out, v_out, u_all[:, :, st.seq - POOL_HIST:]


def kernel(x_prompt, x_sample, c_prompt, c_sample, cache_k, cache_v, state_pool, w_mod, b_mod,
           norm_mix_g, w_in, w_pool, pool_scale, lam_q1, lam_k1, lam_q2, lam_k2, subln_g, w_out,
           norm_ffn_g, w_gate, w_up, w_down, final_g):
    p = {
        "norm_mix_g": norm_mix_g.reshape(DEPTH, 1, D_MODEL),
        "norm_ffn_g": norm_ffn_g.reshape(DEPTH, 1, D_MODEL),
        "final_g": final_g.reshape(1, D_MODEL),
        "w_in": w_in.astype(BF16),
        "w_pool": w_pool.astype(BF16),
        "pool_scale": pool_scale.reshape(DEPTH, 1, POOL_WIDTH),
        "lam": jnp.stack([lam_q1, lam_k1, lam_q2, lam_k2], axis=1),
        "subln_g": subln_g.reshape(DEPTH, 1, V_DIM),
        "w_out": w_out.astype(BF16),
        "w_gate": w_gate.astype(BF16),
        "w_up": w_up.astype(BF16),
        "w_down": w_down.astype(BF16),
    }
    c_all = jnp.concatenate([c_prompt, c_sample], axis=0)
    c_all = jnp.pad(c_all, ((0, -c_all.shape[0] % 8), (0, 0)))
    mod = _modulation(c_all, w_mod, b_mod)[:, :BATCH + DEC_BATCH]
    mod_prompt = mod[:, :BATCH].reshape(DEPTH * BATCH, 1, N_MOD * D_MODEL)
    mod_sample = jnp.repeat(mod[:, BATCH:], DEC_SEQ, axis=1)

    hist_prompt = jnp.zeros((DEPTH, BATCH, HALO, POOL_WIDTH), F32)
    hist_sample = jnp.pad(state_pool, ((0, 0), (0, 0), (HALO - POOL_HIST, 0), (0, 0)))
    caches = (jnp.transpose(cache_k, (0, 1, 3, 4, 2)).reshape(DEPTH * DEC_BATCH, ATTN_WIDTH, PAST_LEN),
              cache_v.reshape(-1, V_DIM))

    y_p, k_p, v_p, pool_p = _run_stream(PROMPT, x_prompt, mod_prompt, hist_prompt, None, p)
    y_s, k_s, v_s, pool_s = _run_stream(SAMPLE, x_sample, mod_sample, hist_sample, caches, p)
    return (y_p, y_s, k_p, v_p, pool_p, k_s, v_s, pool_s)
```

```python
import functools
import math
from typing import NamedTuple

import jax
import jax.numpy as jnp
from jax import lax
from jax.experimental import pallas as pl
from jax.experimental.pallas import tpu as pltpu

F32 = jnp.float32
BF16 = jnp.bfloat16

D_MODEL = 2048
BATCH = 4
SEQ = 4096
DEPTH = 4
DEC_BATCH = 32
DEC_SEQ = 16
PAST_LEN = 2048
CHUNK = 64
POOL_WIDTH = D_MODEL // 2
N_POOL_GROUPS = 4
POOL_GROUP = POOL_WIDTH // N_POOL_GROUPS
POOL_WINDOWS = (2, 4, 8, 16)
POOL_HIST = max(POOL_WINDOWS) - 1
HALO = POOL_HIST + 1
ATTN_WIDTH = D_MODEL - POOL_WIDTH
HEAD_DIM = 64
N_HEADS = ATTN_WIDTH // (2 * HEAD_DIM)
V_DIM = 2 * HEAD_DIM
IN_COLS = POOL_WIDTH + 3 * ATTN_WIDTH
IN_SECTIONS = IN_COLS // ATTN_WIDTH
D_FF = -(-8 * D_MODEL // (3 * 256)) * 256
ROPE_THETA = 10000.0
NORM_EPS = 1e-6
SUBLN_EPS = 1e-5
N_MOD = 6
MOD_SHIFT_A, MOD_SCALE_A, MOD_GATE_A, MOD_SHIFT_F, MOD_SCALE_F, MOD_GATE_F = range(N_MOD)

V7X_VMEM_BYTES = 64 * 1024 * 1024
V7X_LANES = 128
VMEM_LIMIT = V7X_VMEM_BYTES * 7 // 8

TM = 512
TM_FFN = 1024
TN_MOD = 1024
TF = 512
TN_DOWN = 512
TQ = 1024
TK = 512
DIAG_TILES = TQ // TK
SCORE_SCALE_LOG2 = HEAD_DIM ** -0.5 * math.log2(math.e)
TK_DEC = 512


class Stream(NamedTuple):
    batch: int
    seq: int
    pos0: int
    mod_rows: int
    k_transposed: bool

    @property
    def rows(self):
        return self.batch * self.seq


PROMPT = Stream(BATCH, SEQ, 0, 1, True)
SAMPLE = Stream(DEC_BATCH, DEC_SEQ, PAST_LEN, DEC_BATCH * DEC_SEQ, False)


def _params(*semantics, flags=None):
    return pltpu.CompilerParams(dimension_semantics=semantics, vmem_limit_bytes=VMEM_LIMIT, flags=flags)


def _rms(x, eps, axis=-1):
    return x * lax.rsqrt(jnp.mean(x * x, axis=axis, keepdims=True) + eps)


def _mod_spec(st, layer, which, tm, tn=D_MODEL):
    per_chunk = D_MODEL // tn
    if st.mod_rows == 1:
        tiles_per_batch = st.seq // tm
        return pl.BlockSpec(
            (None, 1, tn),
            lambda i, j: (layer * st.batch + i // tiles_per_batch, 0, which * per_chunk + j % per_chunk))
    return pl.BlockSpec((None, tm, tn), lambda i, j: (layer, i, which * per_chunk + j % per_chunk))


def _mod_kernel(c_ref, w_ref, b_ref, o_ref):
    c = c_ref[...]
    cs = (c * jax.nn.sigmoid(c)).astype(BF16)
    o_ref[...] = jnp.dot(cs, w_ref[...].astype(BF16), preferred_element_type=F32) + b_ref[...]


def _modulation(c_all, w_mod, b_mod):
    nb = c_all.shape[0]
    n = N_MOD * D_MODEL
    return pl.pallas_call(
        _mod_kernel,
        grid=(DEPTH, n // TN_MOD),
        in_specs=[
            pl.BlockSpec((nb, D_MODEL), lambda l, j: (0, 0)),
            pl.BlockSpec((None, D_MODEL, TN_MOD), lambda l, j: (l, 0, j)),
            pl.BlockSpec((None, 1, TN_MOD), lambda l, j: (l, 0, j)),
        ],
        out_specs=pl.BlockSpec((None, nb, TN_MOD), lambda l, j: (l, 0, j)),
        out_shape=jax.ShapeDtypeStruct((DEPTH, nb, n), F32),
        compiler_params=_params("arbitrary", "arbitrary"),
        name="modulation",
    )(c_all, w_mod, b_mod.reshape(DEPTH, 1, n))


def _rope_store(acc, cos_ref, sin_ref, scale, out_refs, transposed_ref=None):
    tm = acc.shape[0]
    lane = lax.broadcasted_iota(jnp.int32, (tm, V7X_LANES), 1)
    first_half = (lane % HEAD_DIM) < (HEAD_DIM // 2)
    cos = cos_ref[...]
    sin = sin_ref[...]
    for c in range(ATTN_WIDTH // V7X_LANES):
        cols = slice(c * V7X_LANES, (c + 1) * V7X_LANES)
        xs = acc[:, cols]
        partner = jnp.where(first_half,
                            pltpu.roll(xs, V7X_LANES - HEAD_DIM // 2, 1),
                            pltpu.roll(xs, HEAD_DIM // 2, 1))
        y = xs * cos + partner * sin
        if scale != 1.0:
            y = y * scale
        for r in out_refs:
            r[:, cols] = y.astype(r.dtype)
        if transposed_ref is not None:
            transposed_ref[cols, :] = y.T


def _in_proj_kernel(x_ref, g_ref, sc_ref, sh_ref, w_ref, cos_ref, sin_ref, *rest,
                    k_transposed, tiles):
    u_ref, q_ref, k_ref, kb_ref, v_ref, vb_ref, h_cur, h_next, acc_a, acc_b = rest[-10:]
    s = pl.program_id(0)
    section = s % IN_SECTIONS
    last = IN_SECTIONS * tiles

    def normalise(dst):
        y = _rms(x_ref[...], NORM_EPS) * g_ref[...]
        dst[...] = (y * (1.0 + sc_ref[...]) + sh_ref[...]).astype(BF16)

    def multiply(dst):
        dst[...] = jnp.dot(h_cur[...], w_ref[...], preferred_element_type=F32)

    def finish_v(acc):
        v_ref[...] = acc[...]
        vb_ref[...] = acc[...].astype(BF16)

    @pl.when(s == 0)
    def _():
        normalise(h_cur)
        multiply(acc_a)

    @pl.when((section == 0) & (s > 0) & (s < last))
    def _():
        h_cur[...] = h_next[...]
        multiply(acc_a)
        finish_v(acc_b)

    @pl.when(section == 1)
    def _():
        multiply(acc_b)
        u_ref[...] = acc_a[...]
        normalise(h_next)

    @pl.when(section == 2)
    def _():
        multiply(acc_a)
        _rope_store(acc_b, cos_ref, sin_ref, SCORE_SCALE_LOG2, (q_ref,))

    @pl.when(section == 3)
    def _():
        multiply(acc_b)
        if k_transposed:
            _rope_store(acc_a, cos_ref, sin_ref, 1.0, (kb_ref,), k_ref)
        else:
            _rope_store(acc_a, cos_ref, sin_ref, 1.0, (k_ref, kb_ref))

    @pl.when(s == last)
    def _():
        finish_v(acc_b)


def _in_proj(st, layer, x, mod, norm_g, w_in, cos_tab, sin_tab, kv_stacks):
    m = st.rows
    tm = TM
    tiles = m // tm
    tab_tiles = cos_tab.shape[0] // tm
    steps = IN_SECTIONS * tiles + 1
    done = lambda s: jnp.maximum(s - 1, 0) // IN_SECTIONS
    normed = lambda s: jnp.minimum((s + IN_SECTIONS - 1) // IN_SECTIONS, tiles - 1)
    out_spec = pl.BlockSpec((tm, ATTN_WIDTH), lambda s: (done(s), 0))
    stack_spec = pl.BlockSpec((tm, ATTN_WIDTH), lambda s: (layer * tiles + done(s), 0))
    sds = lambda dt: jax.ShapeDtypeStruct((m, ATTN_WIDTH), dt)
    stack_sds = jax.ShapeDtypeStruct((DEPTH * m, ATTN_WIDTH), F32)
    out_specs = [out_spec, out_spec, stack_spec, out_spec, stack_spec, out_spec]
    out_shape = [sds(F32), sds(BF16), stack_sds, sds(BF16), stack_sds, sds(BF16)]
    if st.k_transposed:
        tiles_per_batch = st.seq // tm
        out_specs[2] = pl.BlockSpec(
            (None, ATTN_WIDTH, tm),
            lambda s: (layer * st.batch + done(s) // tiles_per_batch, 0, done(s) % tiles_per_batch))
        out_shape[2] = jax.ShapeDtypeStruct((DEPTH * st.batch, ATTN_WIDTH, st.seq), F32)
    if st.mod_rows == 1:
        tiles_per_batch = st.seq // tm
        mod_spec = lambda which: pl.BlockSpec(
            (None, 1, D_MODEL), lambda s: (layer * st.batch + normed(s) // tiles_per_batch, 0, which))
    else:
        mod_spec = lambda which: pl.BlockSpec((None, tm, D_MODEL), lambda s: (layer, normed(s), which),
                                              pipeline_mode=pl.Buffered(1))
    in_specs = [
        pl.BlockSpec((tm, D_MODEL), lambda s: (normed(s), 0)),
        pl.BlockSpec((None, 1, D_MODEL), lambda s: (layer, 0, 0)),
        mod_spec(MOD_SCALE_A),
        mod_spec(MOD_SHIFT_A),
        pl.BlockSpec((None, D_MODEL, ATTN_WIDTH),
                     lambda s: (layer, 0, jnp.minimum(s, steps - 2) % IN_SECTIONS)),
        pl.BlockSpec((tm, V7X_LANES), lambda s: (done(s) % tab_tiles, 0)),
        pl.BlockSpec((tm, V7X_LANES), lambda s: (done(s) % tab_tiles, 0)),
    ]
    args = [x, norm_g, mod, mod, w_in, cos_tab, sin_tab]
    aliases = {}
    if kv_stacks is not None:
        aliases = {len(args): 2, len(args) + 1: 4}
        in_specs += [pl.BlockSpec(memory_space=pl.ANY)] * 2
        args += list(kv_stacks)
    return pl.pallas_call(
        functools.partial(_in_proj_kernel, k_transposed=st.k_transposed, tiles=tiles),
        grid=(steps,),
        in_specs=in_specs,
        out_specs=out_specs,
        out_shape=out_shape,
        input_output_aliases=aliases,
        scratch_shapes=[pltpu.VMEM((tm, D_MODEL), BF16), pltpu.VMEM((tm, D_MODEL), BF16),
                        pltpu.VMEM((tm, ATTN_WIDTH), F32), pltpu.VMEM((tm, ATTN_WIDTH), F32)],
        compiler_params=_params("arbitrary"),
        name="in_proj",
    )(*args)


def _pool_kernel(u_ref, hist_ref, wp_ref, ps_ref, o_ref, ext_ref, *, tm, pos0):
    i = pl.program_id(1)

    @pl.when(i == 0)
    def _():
        ext_ref[0:HALO, :] = hist_ref[...]

    @pl.when(i > 0)
    def _():
        ext_ref[0:HALO, :] = ext_ref[tm:tm + HALO, :]

    ext_ref[HALO:, :] = u_ref[...]
    pos = pos0 + i * tm + lax.broadcasted_iota(jnp.int32, (tm, 1), 0)
    for g, w in enumerate(POOL_WINDOWS):
        cols = slice(g * POOL_GROUP, (g + 1) * POOL_GROUP)
        cur = u_ref[:, cols]
        s = cur
        for back in range(1, w):
            s = s + ext_ref[HALO - back:HALO - back + tm, cols]
        count = jnp.minimum(w, pos + 1).astype(F32)
        d = (s / count - cur).astype(BF16)
        y = jnp.dot(d, wp_ref[g], preferred_element_type=F32) * ps_ref[:, cols]
        o_ref[:, cols] = y.astype(BF16)


def _pool(st, layer, u, hist, w_pool, pool_scale):
    tm = min(TM, st.seq)
    tiles = st.seq // tm
    return pl.pallas_call(
        functools.partial(_pool_kernel, tm=tm, pos0=st.pos0),
        grid=(st.batch, tiles),
        in_specs=[
            pl.BlockSpec((tm, POOL_WIDTH), lambda b, i: (b * tiles + i, 0)),
            pl.BlockSpec((None, HALO, POOL_WIDTH), lambda b, i: (b, 0, 0)),
            pl.BlockSpec((None, N_POOL_GROUPS, POOL_GROUP, POOL_GROUP), lambda b, i: (layer, 0, 0, 0)),
            pl.BlockSpec((None, 1, POOL_WIDTH), lambda b, i: (layer, 0, 0)),
        ],
        out_specs=pl.BlockSpec((tm, POOL_WIDTH), lambda b, i: (b * tiles + i, 0)),
        out_shape=jax.ShapeDtypeStruct((st.rows, POOL_WIDTH), BF16),
        scratch_shapes=[pltpu.VMEM((tm + HALO, POOL_WIDTH), F32)],
        compiler_params=_params("arbitrary", "arbitrary"),
        name="pool_mix",
    )(u, hist, w_pool, pool_scale)


def _lambda(lam_ref, lam_init):
    lam = lam_ref[...]
    s1 = jnp.sum(lam[0:1] * lam[1:2], axis=-1, keepdims=True)
    s2 = jnp.sum(lam[2:3] * lam[3:4], axis=-1, keepdims=True)
    return jnp.exp(s1) - jnp.exp(s2) + lam_init


def _online_softmax_step(s, v, m_ref, l_ref, acc_ref):
    m_prev = m_ref[...]
    m_new = jnp.maximum(m_prev, jnp.max(s, axis=-1, keepdims=True))
    alpha = jnp.exp2(m_prev - m_new)
    p = jnp.exp2(s - m_new)
    l_ref[...] = alpha * l_ref[...] + jnp.sum(p, axis=-1, keepdims=True)
    acc_ref[...] = alpha * acc_ref[...] + jnp.dot(p.astype(BF16), v, preferred_element_type=F32)
    m_ref[...] = m_new


def _sub_ln(o1, o2, lam, g, lam_init, axis=-1):
    a = o1 - lam * o2
    return (_rms(a, SUBLN_EPS, axis) * g) * (1.0 - lam_init)


_NT = (((1,), (1,)), ((), ()))


def _prompt_attn_kernel(q_ref, k_ref, v_ref, lam_ref, g_ref, o_ref,
                        qz_ref, vt_ref, sa_ref, sb_ref, m_ref, l_ref, acc_ref, *, lam_init):
    i = pl.program_id(2)

    @pl.when(i == 0)
    def _():
        for t in range(vt_ref.shape[0]):
            vt_ref[t] = v_ref[t * TK:(t + 1) * TK, :].astype(F32).T.astype(BF16)

    q = q_ref[...]
    lane = lax.broadcasted_iota(jnp.int32, (TQ, V_DIM), 1)
    zero = jnp.zeros_like(q)
    qz_ref[0:TQ, :] = jnp.where(lane < HEAD_DIM, q, zero)
    qz_ref[TQ:, :] = jnp.where(lane >= HEAD_DIM, q, zero)
    m_ref[...] = jnp.full_like(m_ref, -jnp.inf)
    l_ref[...] = jnp.zeros_like(l_ref)
    acc_ref[...] = jnp.zeros_like(acc_ref)

    def scores(t, s_ref):
        start = pl.multiple_of(t * TK, TK)
        s_ref[...] = lax.dot_general(k_ref[pl.ds(start, TK), :], qz_ref[...], _NT,
                                     preferred_element_type=F32)

    def consume(t, s_ref, diag=None):
        q_lo = 0 if diag is None else diag * TK
        width = TQ - q_lo
        col_sets = (slice(q_lo, TQ), slice(TQ + q_lo, 2 * TQ))

        def gather(ref):
            if q_lo == 0:
                return ref[...]
            return jnp.concatenate([ref[:, c] for c in col_sets], axis=1)

        def scatter(ref, val):
            if q_lo == 0:
                ref[...] = val
            else:
                for n, c in enumerate(col_sets):
                    ref[:, c] = val[:, n * width:(n + 1) * width]

        s = gather(s_ref)
        if diag is not None:
            key = q_lo + lax.broadcasted_iota(jnp.int32, (TK, 2 * width), 0)
            col = lax.broadcasted_iota(jnp.int32, (TK, 2 * width), 1)
            query = q_lo + jnp.where(col >= width, col - width, col)
            s = jnp.where((key // CHUNK) <= (query // CHUNK), s, -jnp.inf)
        m_prev = gather(m_ref)
        m_new = jnp.maximum(m_prev, jnp.max(s, axis=0, keepdims=True))
        alpha = jnp.exp2(m_prev - m_new)
        p = jnp.exp2(s - m_new)
        scatter(l_ref, alpha * gather(l_ref) + jnp.sum(p, axis=0, keepdims=True))
        scatter(acc_ref, alpha * gather(acc_ref) + jnp.dot(vt_ref[t], p.astype(BF16),
                                                           preferred_element_type=F32))
        scatter(m_ref, m_new)

    scores(0, sa_ref)

    def pair(p, carry):
        t = 2 * p
        scores(t + 1, sb_ref)
        consume(t, sa_ref)
        scores(t + 2, sa_ref)
        consume(t + 1, sb_ref)
        return carry

    lax.fori_loop(0, i, pair, 0)
    first_diag = DIAG_TILES * i
    scores(first_diag + 1, sb_ref)
    consume(first_diag, sa_ref, diag=0)
    consume(first_diag + 1, sb_ref, diag=1)

    o = acc_ref[...] / l_ref[...]
    y = _sub_ln(o[:, 0:TQ], o[:, TQ:], _lambda(lam_ref, lam_init), g_ref[...], lam_init, axis=0)
    o_ref[...] = y.T.astype(BF16)


def _prompt_attn(st, layer, q, k, v, lam, subln_g, lam_init):
    assert DIAG_TILES == 2
    nq = st.seq // TQ
    return pl.pallas_call(
        functools.partial(_prompt_attn_kernel, lam_init=lam_init),
        grid=(st.batch, N_HEADS, nq),
        in_specs=[
            pl.BlockSpec((TQ, V_DIM), lambda b, h, i: (b * nq + i, h)),
            pl.BlockSpec((st.seq, V_DIM), lambda b, h, i: (b, h)),
            pl.BlockSpec((st.seq, V_DIM), lambda b, h, i: (b, h)),
            pl.BlockSpec((None, 4, HEAD_DIM), lambda b, h, i: (layer, 0, 0)),
            pl.BlockSpec((None, V_DIM, 1), lambda b, h, i: (layer, 0, 0)),
        ],
        out_specs=pl.BlockSpec((TQ, V_DIM), lambda b, h, i: (b * nq + i, h)),
        out_shape=jax.ShapeDtypeStruct((st.rows, ATTN_WIDTH), BF16),
        scratch_shapes=[
            pltpu.VMEM((2 * TQ, V_DIM), BF16),
            pltpu.VMEM((st.seq // TK, V_DIM, TK), BF16),
            pltpu.VMEM((TK, 2 * TQ), F32),
            pltpu.VMEM((TK, 2 * TQ), F32),
            pltpu.VMEM((1, 2 * TQ), F32),
            pltpu.VMEM((1, 2 * TQ), F32),
            pltpu.VMEM((V_DIM, 2 * TQ), F32),
        ],
        compiler_params=_params("arbitrary", "arbitrary", "arbitrary"),
        name="prompt_attn",
    )(q, k, v, lam, subln_g.reshape(DEPTH, V_DIM, 1))


def _decode_attn_kernel(q_ref, kc_ref, vc_ref, kn_ref, vn_ref, lam_ref, g_ref, o_ref,
                        m_ref, l_ref, acc_ref, *, lam_init):
    t = pl.program_id(1)
    n_score_heads = 2 * N_HEADS
    pair_rows = 2 * DEC_SEQ

    @pl.when(t == 0)
    def _():
        m_ref[...] = jnp.full_like(m_ref, -jnp.inf)
        l_ref[...] = jnp.zeros_like(l_ref)
        acc_ref[...] = jnp.zeros_like(acc_ref)

    q = q_ref[...]
    q_heads = [q[:, h * HEAD_DIM:(h + 1) * HEAD_DIM] for h in range(n_score_heads)]

    def update(score_of_head, value_of_head):
        s = jnp.concatenate([score_of_head(h) for h in range(n_score_heads)], axis=0)
        m_prev = m_ref[...]
        m_new = jnp.maximum(m_prev, jnp.max(s, axis=-1, keepdims=True))
        alpha = jnp.exp2(m_prev - m_new)
        p = jnp.exp2(s - m_new)
        l_ref[...] = alpha * l_ref[...] + jnp.sum(p, axis=-1, keepdims=True)
        pb = p.astype(BF16)
        pv = jnp.concatenate(
            [jnp.dot(pb[j * pair_rows:(j + 1) * pair_rows], value_of_head(j), preferred_element_type=F32)
             for j in range(N_HEADS)], axis=0)
        acc_ref[...] = alpha * acc_ref[...] + pv
        m_ref[...] = m_new

    update(lambda h: jnp.dot(q_heads[h], kc_ref[h * HEAD_DIM:(h + 1) * HEAD_DIM, :].astype(BF16),
                             preferred_element_type=F32),
           lambda j: vc_ref[pl.ds(j, TK_DEC, stride=N_HEADS), :].astype(BF16))

    @pl.when(t == pl.num_programs(1) - 1)
    def _():
        update(lambda h: lax.dot_general(q_heads[h], kn_ref[:, h * HEAD_DIM:(h + 1) * HEAD_DIM], _NT,
                                         preferred_element_type=F32),
               lambda j: vn_ref[:, j * V_DIM:(j + 1) * V_DIM])
        o = acc_ref[...] / l_ref[...]
        lam = _lambda(lam_ref, lam_init)
        g = g_ref[...]
        for j in range(N_HEADS):
            r0 = j * pair_rows
            y = _sub_ln(o[r0:r0 + DEC_SEQ], o[r0 + DEC_SEQ:r0 + pair_rows], lam, g, lam_init)
            o_ref[:, j * V_DIM:(j + 1) * V_DIM] = y.astype(BF16)


def _decode_attn(st, layer, q, k_new, v_new, cache_k, cache_v, lam, subln_g, lam_init):
    rows_bd = 2 * N_HEADS * DEC_SEQ
    tok = pl.BlockSpec((DEC_SEQ, ATTN_WIDTH), lambda b, t: (b, 0))
    tiles = PAST_LEN // TK_DEC
    cache_k_spec = pl.BlockSpec((None, ATTN_WIDTH, TK_DEC), lambda b, t: (layer * st.batch + b, 0, t))
    cache_v_spec = pl.BlockSpec((TK_DEC * N_HEADS, V_DIM),
                                lambda b, t: ((layer * st.batch + b) * tiles + t, 0))
    return pl.pallas_call(
        functools.partial(_decode_attn_kernel, lam_init=lam_init),
        grid=(st.batch, tiles),
        in_specs=[
            tok, cache_k_spec, cache_v_spec, tok, tok,
            pl.BlockSpec((None, 4, HEAD_DIM), lambda b, t: (layer, 0, 0)),
            pl.BlockSpec((None, 1, V_DIM), lambda b, t: (layer, 0, 0)),
        ],
        out_specs=tok,
        out_shape=jax.ShapeDtypeStruct((st.rows, ATTN_WIDTH), BF16),
        scratch_shapes=[
            pltpu.VMEM((rows_bd, 1), F32),
            pltpu.VMEM((rows_bd, 1), F32),
            pltpu.VMEM((rows_bd, V_DIM), F32),
        ],
        compiler_params=_params("arbitrary", "arbitrary"),
        name="decode_attn",
    )(q, cache_k, cache_v, k_new, v_new, lam, subln_g)


def _out_proj_kernel(x_ref, pool_ref, att_ref, w_ref, gate_ref, g_ref, sc_ref, sh_ref,
                     xo_ref, h_ref):
    mixed = (jnp.dot(pool_ref[...], w_ref[0:POOL_WIDTH, :], preferred_element_type=F32)
             + jnp.dot(att_ref[...], w_ref[POOL_WIDTH:, :], preferred_element_type=F32))
    x = x_ref[...] + gate_ref[...] * mixed
    xo_ref[...] = x
    y = _rms(x, NORM_EPS) * g_ref[...]
    h_ref[...] = (y * (1.0 + sc_ref[...]) + sh_ref[...]).astype(BF16)


def _out_proj(st, layer, x, pool_out, att, mod, w_out, norm_g):
    m = st.rows
    tm = TM
    row = lambda i, j: (i, 0)
    return pl.pallas_call(
        _out_proj_kernel,
        grid=(m // tm, 1),
        in_specs=[
            pl.BlockSpec((tm, D_MODEL), row),
            pl.BlockSpec((tm, POOL_WIDTH), row),
            pl.BlockSpec((tm, ATTN_WIDTH), row),
            pl.BlockSpec((None, D_MODEL, D_MODEL), lambda i, j: (layer, 0, 0),
                         pipeline_mode=pl.Buffered(1)),
            _mod_spec(st, layer, MOD_GATE_A, tm),
            pl.BlockSpec((None, 1, D_MODEL), lambda i, j: (layer, 0, 0)),
            _mod_spec(st, layer, MOD_SCALE_F, tm),
            _mod_spec(st, layer, MOD_SHIFT_F, tm),
        ],
        out_specs=[pl.BlockSpec((tm, D_MODEL), row), pl.BlockSpec((tm, D_MODEL), row)],
        out_shape=[jax.ShapeDtypeStruct((m, D_MODEL), F32), jax.ShapeDtypeStruct((m, D_MODEL), BF16)],
        compiler_params=_params("arbitrary", "arbitrary"),
        name="out_proj",
    )(x, pool_out, att, w_out, mod, norm_g, mod, mod)


def _gate_up_kernel(h_ref, wg_ref, wu_ref, o_ref):
    h = h_ref[...]
    g = jnp.dot(h, wg_ref[...], preferred_element_type=F32)
    u = jnp.dot(h, wu_ref[...], preferred_element_type=F32)
    o_ref[...] = (g * jax.nn.sigmoid(g) * u).astype(BF16)


def _gate_up(st, layer, h, w_gate, w_up):
    m = st.rows
    tm = min(TM_FFN, m)
    w_spec = pl.BlockSpec((None, D_MODEL, TF), lambda i, j: (layer, 0, j))
    return pl.pallas_call(
        _gate_up_kernel,
        grid=(m // tm, D_FF // TF),
        in_specs=[pl.BlockSpec((tm, D_MODEL), lambda i, j: (i, 0)), w_spec, w_spec],
        out_specs=pl.BlockSpec((tm, TF), lambda i, j: (i, j)),
        out_shape=jax.ShapeDtypeStruct((m, D_FF), BF16),
        compiler_params=_params("arbitrary", "arbitrary"),
        name="ffn_gate_up",
    )(h, w_gate, w_up)


def _down_kernel(a_ref, w_ref, x_ref, gate_ref, o_ref):
    y = jnp.dot(a_ref[...], w_ref[...], preferred_element_type=F32)
    o_ref[...] = x_ref[...] + gate_ref[...] * y


def _down(st, layer, act, x, mod, w_down):
    m = st.rows
    tm = min(TM_FFN, m)
    return pl.pallas_call(
        _down_kernel,
        grid=(m // tm, D_MODEL // TN_DOWN),
        in_specs=[
            pl.BlockSpec((tm, D_FF), lambda i, j: (i, 0)),
            pl.BlockSpec((None, D_FF, TN_DOWN), lambda i, j: (layer, 0, j)),
            pl.BlockSpec((tm, TN_DOWN), lambda i, j: (i, j)),
            _mod_spec(st, layer, MOD_GATE_F, tm, TN_DOWN),
        ],
        out_specs=pl.BlockSpec((tm, TN_DOWN), lambda i, j: (i, j)),
        out_shape=jax.ShapeDtypeStruct((m, D_MODEL), F32),
        compiler_params=_params("arbitrary", "arbitrary"),
        name="ffn_down",
    )(act, w_down, x, mod)


def _final_norm_kernel(x_ref, g_ref, o_ref):
    o_ref[...] = _rms(x_ref[...], NORM_EPS) * g_ref[...]


def _final_norm(x, g):
    m = x.shape[0]
    return pl.pallas_call(
        _final_norm_kernel,
        grid=(m // TM,),
        in_specs=[pl.BlockSpec((TM, D_MODEL), lambda i: (i, 0)), pl.BlockSpec((1, D_MODEL), lambda i: (0, 0))],
        out_specs=pl.BlockSpec((TM, D_MODEL), lambda i: (i, 0)),
        out_shape=jax.ShapeDtypeStruct((m, D_MODEL), F32),
        compiler_params=_params("arbitrary"),
        name="final_norm",
    )(x, g)


def _rope_tables(st):
    half = HEAD_DIM // 2
    inv = 1.0 / (ROPE_THETA ** (jnp.arange(half, dtype=F32) * (2.0 / HEAD_DIM)))
    pos = st.pos0 + jnp.arange(st.seq)
    ang = pos.astype(F32)[:, None] * inv[None, :]
    cos, sin = jnp.cos(ang), jnp.sin(ang)
    reps = V7X_LANES // HEAD_DIM
    cos_t = jnp.tile(jnp.concatenate([cos, cos], axis=-1), (1, reps))
    sin_t = jnp.tile(jnp.concatenate([-sin, sin], axis=-1), (1, reps))
    if st.seq < TM:
        cos_t = jnp.tile(cos_t, (TM // st.seq, 1))
        sin_t = jnp.tile(sin_t, (TM // st.seq, 1))
    return cos_t, sin_t


def _run_stream(st, x, mod, hist, caches, p):
    m = st.rows
    x = x.reshape(m, D_MODEL)
    cos_tab, sin_tab = _rope_tables(st)
    kv_stacks, us = None, []
    for l in range(DEPTH):
        lam_init = 0.8 - 0.6 * math.exp(-0.3 * l)
        u, q, k_stack, kb, v_stack, vb = _in_proj(st, l, x, mod, p["norm_mix_g"], p["w_in"],
                                                  cos_tab, sin_tab, kv_stacks)
        kv_stacks = (k_stack, v_stack)
        pool_out = _pool(st, l, u, hist[l], p["w_pool"], p["pool_scale"])
        if caches is None:
            att = _prompt_attn(st, l, q, kb, vb, p["lam"], p["subln_g"], lam_init)
        else:
            att = _decode_attn(st, l, q, kb, vb, caches[0], caches[1], p["lam"], p["subln_g"], lam_init)
        x, h = _out_proj(st, l, x, pool_out, att, mod, p["w_out"], p["norm_ffn_g"])
        act = _gate_up(st, l, h, p["w_gate"], p["w_up"])
        x = _down(st, l, act, x, mod, p["w_down"])
        us.append(u)
    y = _final_norm(x, p["final_g"]).reshape(st.batch, st.seq, D_MODEL)
    k_stack, v_stack = kv_stacks
    if st.k_transposed:
        k_out = k_stack.reshape(DEPTH, st.batch, 2 * N_HEADS, HEAD_DIM, st.seq)
        k_out = jnp.transpose(k_out, (0, 1, 4, 2, 3))
    else:
        k_out = k_stack.reshape(DEPTH, st.batch, st.seq, 2 * N_HEADS, HEAD_DIM)
    v_out = v_stack.reshape(DEPTH, st.batch, st.seq, N_HEADS, V_DIM)
    u_all = jnp.stack(us).reshape(DEPTH, st.batch, st.seq, POOL_WIDTH)
    return y, k_out, v_out, u_all[:, :, st.seq - POOL_HIST:]


def kernel(x_prompt, x_sample, c_prompt, c_sample, cache_k, cache_v, state_pool, w_mod, b_mod,
           norm_mix_g, w_in, w_pool, pool_scale, lam_q1, lam_k1, lam_q2, lam_k2, subln_g, w_out,
           norm_ffn_g, w_gate, w_up, w_down, final_g):
    p = {
        "norm_mix_g": norm_mix_g.reshape(DEPTH, 1, D_MODEL),
        "norm_ffn_g": norm_ffn_g.reshape(DEPTH, 1, D_MODEL),
        "final_g": final_g.reshape(1, D_MODEL),
        "w_in": w_in.astype(BF16),
        "w_pool": w_pool.astype(BF16),
        "pool_scale": pool_scale.reshape(DEPTH, 1, POOL_WIDTH),
        "lam": jnp.stack([lam_q1, lam_k1, lam_q2, lam_k2], axis=1),
        "subln_g": subln_g.reshape(DEPTH, 1, V_DIM),
        "w_out": w_out.astype(BF16),
        "w_gate": w_gate.astype(BF16),
        "w_up": w_up.astype(BF16),
        "w_down": w_down.astype(BF16),
    }
    c_all = jnp.concatenate([c_prompt, c_sample], axis=0)
    c_all = jnp.pad(c_all, ((0, -c_all.shape[0] % 8), (0, 0)))
    mod = _modulation(c_all, w_mod, b_mod)[:, :BATCH + DEC_BATCH]
    mod_prompt = mod[:, :BATCH].reshape(DEPTH * BATCH, 1, N_MOD * D_MODEL)
    mod_sample = jnp.repeat(mod[:, BATCH:], DEC_SEQ, axis=1)

    hist_prompt = jnp.zeros((DEPTH, BATCH, HALO, POOL_WIDTH), F32)
    hist_sample = jnp.pad(state_pool, ((0, 0), (0, 0), (HALO - POOL_HIST, 0), (0, 0)))
    caches = (jnp.transpose(cache_k, (0, 1, 3, 4, 2)).reshape(DEPTH * DEC_BATCH, ATTN_WIDTH, PAST_LEN),
              cache_v.reshape(-1, V_DIM))

    y_p, k_p, v_p, pool_p = _run_stream(PROMPT, x_prompt, mod_prompt, hist_prompt, None, p)
    y_s, k_s, v_s, pool_s = _run_stream(SAMPLE, x_sample, mod_sample, hist_sample, caches, p)
    return (y_p, y_s, k_p, v_p, pool_p, k_s, v_s, pool_s)
```

```python
import functools
import math
from typing import NamedTuple

import jax
import jax.numpy as jnp
from jax import lax
from jax.experimental import pallas as pl
from jax.experimental.pallas import tpu as pltpu

F32 = jnp.float32
BF16 = jnp.bfloat16

D_MODEL = 2048
BATCH = 4
SEQ = 4096
DEPTH = 4
DEC_BATCH = 32
DEC_SEQ = 16
PAST_LEN = 2048
CHUNK = 64
POOL_WIDTH = D_MODEL // 2
N_POOL_GROUPS = 4
POOL_GROUP = POOL_WIDTH // N_POOL_GROUPS
POOL_WINDOWS = (2, 4, 8, 16)
POOL_HIST = max(POOL_WINDOWS) - 1
HALO = POOL_HIST + 1
ATTN_WIDTH = D_MODEL - POOL_WIDTH
HEAD_DIM = 64
N_HEADS = ATTN_WIDTH // (2 * HEAD_DIM)
V_DIM = 2 * HEAD_DIM
IN_COLS = POOL_WIDTH + 3 * ATTN_WIDTH
IN_SECTIONS = IN_COLS // ATTN_WIDTH
D_FF = -(-8 * D_MODEL // (3 * 256)) * 256
ROPE_THETA = 10000.0
NORM_EPS = 1e-6
SUBLN_EPS = 1e-5
N_MOD = 6
MOD_SHIFT_A, MOD_SCALE_A, MOD_GATE_A, MOD_SHIFT_F, MOD_SCALE_F, MOD_GATE_F = range(N_MOD)

V7X_VMEM_BYTES = 64 * 1024 * 1024
V7X_LANES = 128
VMEM_LIMIT = V7X_VMEM_BYTES * 7 // 8

TM = 512
TM_FFN = 1024
TN_MOD = 1024
TF = 512
TN_DOWN = 512
TQ = 1024
TK = 512
DIAG_TILES = TQ // TK
SCORE_SCALE_LOG2 = HEAD_DIM ** -0.5 * math.log2(math.e)
TK_DEC = 1024


class Stream(NamedTuple):
    batch: int
    seq: int
    pos0: int
    mod_rows: int
    k_transposed: bool

    @property
    def rows(self):
        return self.batch * self.seq


PROMPT = Stream(BATCH, SEQ, 0, 1, True)
SAMPLE = Stream(DEC_BATCH, DEC_SEQ, PAST_LEN, DEC_BATCH * DEC_SEQ, False)


def _params(*semantics, flags=None):
    return pltpu.CompilerParams(dimension_semantics=semantics, vmem_limit_bytes=VMEM_LIMIT, flags=flags)


def _rms(x, eps, axis=-1):
    return x * lax.rsqrt(jnp.mean(x * x, axis=axis, keepdims=True) + eps)


def _mod_spec(st, layer, which, tm, tn=D_MODEL):
    per_chunk = D_MODEL // tn
    if st.mod_rows == 1:
        tiles_per_batch = st.seq // tm
        return pl.BlockSpec(
            (None, 1, tn),
            lambda i, j: (layer * st.batch + i // tiles_per_batch, 0, which * per_chunk + j % per_chunk))
    return pl.BlockSpec((None, tm, tn), lambda i, j: (layer, i, which * per_chunk + j % per_chunk))


def _mod_kernel(c_ref, w_ref, b_ref, o_ref):
    c = c_ref[...]
    cs = (c * jax.nn.sigmoid(c)).astype(BF16)
    o_ref[...] = jnp.dot(cs, w_ref[...].astype(BF16), preferred_element_type=F32) + b_ref[...]


def _modulation(c_all, w_mod, b_mod):
    nb = c_all.shape[0]
    n = N_MOD * D_MODEL
    return pl.pallas_call(
        _mod_kernel,
        grid=(DEPTH, n // TN_MOD),
        in_specs=[
            pl.BlockSpec((nb, D_MODEL), lambda l, j: (0, 0)),
            pl.BlockSpec((None, D_MODEL, TN_MOD), lambda l, j: (l, 0, j)),
            pl.BlockSpec((None, 1, TN_MOD), lambda l, j: (l, 0, j)),
        ],
        out_specs=pl.BlockSpec((None, nb, TN_MOD), lambda l, j: (l, 0, j)),
        out_shape=jax.ShapeDtypeStruct((DEPTH, nb, n), F32),
        compiler_params=_params("arbitrary", "arbitrary"),
        name="modulation",
    )(c_all, w_mod, b_mod.reshape(DEPTH, 1, n))


def _rope_store(acc, cos_ref, sin_ref, scale, out_refs, transposed_ref=None):
    tm = acc.shape[0]
    lane = lax.broadcasted_iota(jnp.int32, (tm, V7X_LANES), 1)
    first_half = (lane % HEAD_DIM) < (HEAD_DIM // 2)
    cos = cos_ref[...]
    sin = sin_ref[...]
    for c in range(ATTN_WIDTH // V7X_LANES):
        cols = slice(c * V7X_LANES, (c + 1) * V7X_LANES)
        xs = acc[:, cols]
        partner = jnp.where(first_half,
                            pltpu.roll(xs, V7X_LANES - HEAD_DIM // 2, 1),
                            pltpu.roll(xs, HEAD_DIM // 2, 1))
        y = xs * cos + partner * sin
        if scale != 1.0:
            y = y * scale
        for r in out_refs:
            r[:, cols] = y.astype(r.dtype)
        if transposed_ref is not None:
            transposed_ref[cols, :] = y.T


def _in_proj_kernel(x_ref, g_ref, sc_ref, sh_ref, w_ref, cos_ref, sin_ref, *rest,
                    k_transposed, tiles):
    u_ref, q_ref, k_ref, kb_ref, v_ref, vb_ref, h_cur, h_next, acc_a, acc_b = rest[-10:]
    s = pl.program_id(0)
    section = s % IN_SECTIONS
    last = IN_SECTIONS * tiles

    def normalise(dst):
        y = _rms(x_ref[...], NORM_EPS) * g_ref[...]
        dst[...] = (y * (1.0 + sc_ref[...]) + sh_ref[...]).astype(BF16)

    def multiply(dst):
        dst[...] = jnp.dot(h_cur[...], w_ref[...], preferred_element_type=F32)

    def finish_v(acc):
        v_ref[...] = acc[...]
        vb_ref[...] = acc[...].astype(BF16)

    @pl.when(s == 0)
    def _():
        normalise(h_cur)
        multiply(acc_a)

    @pl.when((section == 0) & (s > 0) & (s < last))
    def _():
        h_cur[...] = h_next[...]
        multiply(acc_a)
        finish_v(acc_b)

    @pl.when(section == 1)
    def _():
        multiply(acc_b)
        u_ref[...] = acc_a[...]
        normalise(h_next)

    @pl.when(section == 2)
    def _():
        multiply(acc_a)
        _rope_store(acc_b, cos_ref, sin_ref, SCORE_SCALE_LOG2, (q_ref,))

    @pl.when(section == 3)
    def _():
        multiply(acc_b)
        if k_transposed:
            _rope_store(acc_a, cos_ref, sin_ref, 1.0, (kb_ref,), k_ref)
        else:
            _rope_store(acc_a, cos_ref, sin_ref, 1.0, (k_ref, kb_ref))

    @pl.when(s == last)
    def _():
        finish_v(acc_b)


def _in_proj(st, layer, x, mod, norm_g, w_in, cos_tab, sin_tab, kv_stacks):
    m = st.rows
    tm = TM
    tiles = m // tm
    tab_tiles = cos_tab.shape[0] // tm
    steps = IN_SECTIONS * tiles + 1
    done = lambda s: jnp.maximum(s - 1, 0) // IN_SECTIONS
    normed = lambda s: jnp.minimum((s + IN_SECTIONS - 1) // IN_SECTIONS, tiles - 1)
    out_spec = pl.BlockSpec((tm, ATTN_WIDTH), lambda s: (done(s), 0))
    stack_spec = pl.BlockSpec((tm, ATTN_WIDTH), lambda s: (layer * tiles + done(s), 0))
    sds = lambda dt: jax.ShapeDtypeStruct((m, ATTN_WIDTH), dt)
    stack_sds = jax.ShapeDtypeStruct((DEPTH * m, ATTN_WIDTH), F32)
    out_specs = [out_spec, out_spec, stack_spec, out_spec, stack_spec, out_spec]
    out_shape = [sds(F32), sds(BF16), stack_sds, sds(BF16), stack_sds, sds(BF16)]
    if st.k_transposed:
        tiles_per_batch = st.seq // tm
        out_specs[2] = pl.BlockSpec(
            (None, ATTN_WIDTH, tm),
            lambda s: (layer * st.batch + done(s) // tiles_per_batch, 0, done(s) % tiles_per_batch))
        out_shape[2] = jax.ShapeDtypeStruct((DEPTH * st.batch, ATTN_WIDTH, st.seq), F32)
    if st.mod_rows == 1:
        tiles_per_batch = st.seq // tm
        mod_spec = lambda which: pl.BlockSpec(
            (None, 1, D_MODEL), lambda s: (layer * st.batch + normed(s) // tiles_per_batch, 0, which))
    else:
        mod_spec = lambda which: pl.BlockSpec((None, tm, D_MODEL), lambda s: (layer, normed(s), which),
                                              pipeline_mode=pl.Buffered(1))
    in_specs = [
        pl.BlockSpec((tm, D_MODEL), lambda s: (normed(s), 0)),
        pl.BlockSpec((None, 1, D_MODEL), lambda s: (layer, 0, 0)),
        mod_spec(MOD_SCALE_A),
        mod_spec(MOD_SHIFT_A),
        pl.BlockSpec((None, D_MODEL, ATTN_WIDTH),
                     lambda s: (layer, 0, jnp.minimum(s, steps - 2) % IN_SECTIONS)),
        pl.BlockSpec((tm, V7X_LANES), lambda s: (done(s) % tab_tiles, 0)),
        pl.BlockSpec((tm, V7X_LANES), lambda s: (done(s) % tab_tiles, 0)),
    ]
    args = [x, norm_g, mod, mod, w_in, cos_tab, sin_tab]
    aliases = {}
    if kv_stacks is not None:
        aliases = {len(args): 2, len(args) + 1: 4}
        in_specs += [pl.BlockSpec(memory_space=pl.ANY)] * 2
        args += list(kv_stacks)
    return pl.pallas_call(
        functools.partial(_in_proj_kernel, k_transposed=st.k_transposed, tiles=tiles),
        grid=(steps,),
        in_specs=in_specs,
        out_specs=out_specs,
        out_shape=out_shape,
        input_output_aliases=aliases,
        scratch_shapes=[pltpu.VMEM((tm, D_MODEL), BF16), pltpu.VMEM((tm, D_MODEL), BF16),
                        pltpu.VMEM((tm, ATTN_WIDTH), F32), pltpu.VMEM((tm, ATTN_WIDTH), F32)],
        compiler_params=_params("arbitrary"),
        name="in_proj",
    )(*args)


def _pool_kernel(u_ref, hist_ref, wp_ref, ps_ref, o_ref, ext_ref, *, tm, pos0):
    i = pl.program_id(1)

    @pl.when(i == 0)
    def _():
        ext_ref[0:HALO, :] = hist_ref[...]

    @pl.when(i > 0)
    def _():
        ext_ref[0:HALO, :] = ext_ref[tm:tm + HALO, :]

    ext_ref[HALO:, :] = u_ref[...]
    pos = pos0 + i * tm + lax.broadcasted_iota(jnp.int32, (tm, 1), 0)
    for g, w in enumerate(POOL_WINDOWS):
        cols = slice(g * POOL_GROUP, (g + 1) * POOL_GROUP)
        cur = u_ref[:, cols]
        s = cur
        for back in range(1, w):
            s = s + ext_ref[HALO - back:HALO - back + tm, cols]
        count = jnp.minimum(w, pos + 1).astype(F32)
        d = (s / count - cur).astype(BF16)
        y = jnp.dot(d, wp_ref[g], preferred_element_type=F32) * ps_ref[:, cols]
        o_ref[:, cols] = y.astype(BF16)


def _pool(st, layer, u, hist, w_pool, pool_scale):
    tm = min(TM, st.seq)
    tiles = st.seq // tm
    return pl.pallas_call(
        functools.partial(_pool_kernel, tm=tm, pos0=st.pos0),
        grid=(st.batch, tiles),
        in_specs=[
            pl.BlockSpec((tm, POOL_WIDTH), lambda b, i: (b * tiles + i, 0)),
            pl.BlockSpec((None, HALO, POOL_WIDTH), lambda b, i: (b, 0, 0)),
            pl.BlockSpec((None, N_POOL_GROUPS, POOL_GROUP, POOL_GROUP), lambda b, i: (layer, 0, 0, 0)),
            pl.BlockSpec((None, 1, POOL_WIDTH), lambda b, i: (layer, 0, 0)),
        ],
        out_specs=pl.BlockSpec((tm, POOL_WIDTH), lambda b, i: (b * tiles + i, 0)),
        out_shape=jax.ShapeDtypeStruct((st.rows, POOL_WIDTH), BF16),
        scratch_shapes=[pltpu.VMEM((tm + HALO, POOL_WIDTH), F32)],
        compiler_params=_params("arbitrary", "arbitrary"),
        name="pool_mix",
    )(u, hist, w_pool, pool_scale)


def _lambda(lam_ref, lam_init):
    lam = lam_ref[...]
    s1 = jnp.sum(lam[0:1] * lam[1:2], axis=-1, keepdims=True)
    s2 = jnp.sum(lam[2:3] * lam[3:4], axis=-1, keepdims=True)
    return jnp.exp(s1) - jnp.exp(s2) + lam_init


def _online_softmax_step(s, v, m_ref, l_ref, acc_ref):
    m_prev = m_ref[...]
    m_new = jnp.maximum(m_prev, jnp.max(s, axis=-1, keepdims=True))
    alpha = jnp.exp2(m_prev - m_new)
    p = jnp.exp2(s - m_new)
    l_ref[...] = alpha * l_ref[...] + jnp.sum(p, axis=-1, keepdims=True)
    acc_ref[...] = alpha * acc_ref[...] + jnp.dot(p.astype(BF16), v, preferred_element_type=F32)
    m_ref[...] = m_new


def _sub_ln(o1, o2, lam, g, lam_init, axis=-1):
    a = o1 - lam * o2
    return (_rms(a, SUBLN_EPS, axis) * g) * (1.0 - lam_init)


_NT = (((1,), (1,)), ((), ()))


def _prompt_attn_kernel(q_ref, k_ref, v_ref, lam_ref, g_ref, o_ref,
                        qz_ref, vt_ref, sa_ref, sb_ref, m_ref, l_ref, acc_ref, *, lam_init):
    i = pl.program_id(2)

    @pl.when(i == 0)
    def _():
        for t in range(vt_ref.shape[0]):
            vt_ref[t] = v_ref[t * TK:(t + 1) * TK, :].astype(F32).T.astype(BF16)

    q = q_ref[...]
    lane = lax.broadcasted_iota(jnp.int32, (TQ, V_DIM), 1)
    zero = jnp.zeros_like(q)
    qz_ref[0:TQ, :] = jnp.where(lane < HEAD_DIM, q, zero)
    qz_ref[TQ:, :] = jnp.where(lane >= HEAD_DIM, q, zero)
    m_ref[...] = jnp.full_like(m_ref, -jnp.inf)
    l_ref[...] = jnp.zeros_like(l_ref)
    acc_ref[...] = jnp.zeros_like(acc_ref)

    def scores(t, s_ref):
        start = pl.multiple_of(t * TK, TK)
        s_ref[...] = lax.dot_general(k_ref[pl.ds(start, TK), :], qz_ref[...], _NT,
                                     preferred_element_type=F32)

    def consume(t, s_ref, diag=None):
        q_lo = 0 if diag is None else diag * TK
        width = TQ - q_lo
        col_sets = (slice(q_lo, TQ), slice(TQ + q_lo, 2 * TQ))

        def gather(ref):
            if q_lo == 0:
                return ref[...]
            return jnp.concatenate([ref[:, c] for c in col_sets], axis=1)

        def scatter(ref, val):
            if q_lo == 0:
                ref[...] = val
            else:
                for n, c in enumerate(col_sets):
                    ref[:, c] = val[:, n * width:(n + 1) * width]

        s = gather(s_ref)
        if diag is not None:
            key = q_lo + lax.broadcasted_iota(jnp.int32, (TK, 2 * width), 0)
            col = lax.broadcasted_iota(jnp.int32, (TK, 2 * width), 1)
            query = q_lo + jnp.where(col >= width, col - width, col)
            s = jnp.where((key // CHUNK) <= (query // CHUNK), s, -jnp.inf)
        m_prev = gather(m_ref)
        m_new = jnp.maximum(m_prev, jnp.max(s, axis=0, keepdims=True))
        alpha = jnp.exp2(m_prev - m_new)
        p = jnp.exp2(s - m_new)
        scatter(l_ref, alpha * gather(l_ref) + jnp.sum(p, axis=0, keepdims=True))
        scatter(acc_ref, alpha * gather(acc_ref) + jnp.dot(vt_ref[t], p.astype(BF16),
                                                           preferred_element_type=F32))
        scatter(m_ref, m_new)

    scores(0, sa_ref)

    def pair(p, carry):
        t = 2 * p
        scores(t + 1, sb_ref)
        consume(t, sa_ref)
        scores(t + 2, sa_ref)
        consume(t + 1, sb_ref)
        return carry

    lax.fori_loop(0, i, pair, 0)
    first_diag = DIAG_TILES * i
    scores(first_diag + 1, sb_ref)
    consume(first_diag, sa_ref, diag=0)
    consume(first_diag + 1, sb_ref, diag=1)

    o = acc_ref[...] / l_ref[...]
    y = _sub_ln(o[:, 0:TQ], o[:, TQ:], _lambda(lam_ref, lam_init), g_ref[...], lam_init, axis=0)
    o_ref[...] = y.T.astype(BF16)


def _prompt_attn(st, layer, q, k, v, lam, subln_g, lam_init):
    assert DIAG_TILES == 2
    nq = st.seq // TQ
    return pl.pallas_call(
        functools.partial(_prompt_attn_kernel, lam_init=lam_init),
        grid=(st.batch, N_HEADS, nq),
        in_specs=[
            pl.BlockSpec((TQ, V_DIM), lambda b, h, i: (b * nq + i, h)),
            pl.BlockSpec((st.seq, V_DIM), lambda b, h, i: (b, h)),
            pl.BlockSpec((st.seq, V_DIM), lambda b, h, i: (b, h)),
            pl.BlockSpec((None, 4, HEAD_DIM), lambda b, h, i: (layer, 0, 0)),
            pl.BlockSpec((None, V_DIM, 1), lambda b, h, i: (layer, 0, 0)),
        ],
        out_specs=pl.BlockSpec((TQ, V_DIM), lambda b, h, i: (b * nq + i, h)),
        out_shape=jax.ShapeDtypeStruct((st.rows, ATTN_WIDTH), BF16),
        scratch_shapes=[
            pltpu.VMEM((2 * TQ, V_DIM), BF16),
            pltpu.VMEM((st.seq // TK, V_DIM, TK), BF16),
            pltpu.VMEM((TK, 2 * TQ), F32),
            pltpu.VMEM((TK, 2 * TQ), F32),
            pltpu.VMEM((1, 2 * TQ), F32),
            pltpu.VMEM((1, 2 * TQ), F32),
            pltpu.VMEM((V_DIM, 2 * TQ), F32),
        ],
        compiler_params=_params("arbitrary", "arbitrary", "arbitrary"),
        name="prompt_attn",
    )(q, k, v, lam, subln_g.reshape(DEPTH, V_DIM, 1))


def _decode_attn_kernel(q_ref, kc_ref, vc_ref, kn_ref, vn_ref, lam_ref, g_ref, o_ref,
                        m_ref, l_ref, acc_ref, *, lam_init):
    t = pl.program_id(1)
    n_score_heads = 2 * N_HEADS
    pair_rows = 2 * DEC_SEQ

    @pl.when(t == 0)
    def _():
        m_ref[...] = jnp.full_like(m_ref, -jnp.inf)
        l_ref[...] = jnp.zeros_like(l_ref)
        acc_ref[...] = jnp.zeros_like(acc_ref)

    q = q_ref[...]
    q_heads = [q[:, h * HEAD_DIM:(h + 1) * HEAD_DIM] for h in range(n_score_heads)]

    def update(score_of_head, value_of_head):
        s = jnp.concatenate([score_of_head(h) for h in range(n_score_heads)], axis=0)
        m_prev = m_ref[...]
        m_new = jnp.maximum(m_prev, jnp.max(s, axis=-1, keepdims=True))
        alpha = jnp.exp2(m_prev - m_new)
        p = jnp.exp2(s - m_new)
        l_ref[...] = alpha * l_ref[...] + jnp.sum(p, axis=-1, keepdims=True)
        pb = p.astype(BF16)
        pv = jnp.concatenate(
            [jnp.dot(pb[j * pair_rows:(j + 1) * pair_rows], value_of_head(j), preferred_element_type=F32)
             for j in range(N_HEADS)], axis=0)
        acc_ref[...] = alpha * acc_ref[...] + pv
        m_ref[...] = m_new

    update(lambda h: jnp.dot(q_heads[h], kc_ref[h * HEAD_DIM:(h + 1) * HEAD_DIM, :].astype(BF16),
                             preferred_element_type=F32),
           lambda j: vc_ref[pl.ds(j, TK_DEC, stride=N_HEADS), :].astype(BF16))

    @pl.when(t == pl.num_programs(1) - 1)
    def _():
        update(lambda h: lax.dot_general(q_heads[h], kn_ref[:, h * HEAD_DIM:(h + 1) * HEAD_DIM], _NT,
                                         preferred_element_type=F32),
               lambda j: vn_ref[:, j * V_DIM:(j + 1) * V_DIM])
        o = acc_ref[...] / l_ref[...]
        lam = _lambda(lam_ref, lam_init)
        g = g_ref[...]
        for j in range(N_HEADS):
            r0 = j * pair_rows
            y = _sub_ln(o[r0:r0 + DEC_SEQ], o[r0 + DEC_SEQ:r0 + pair_rows], lam, g, lam_init)
            o_ref[:, j * V_DIM:(j + 1) * V_DIM] = y.astype(BF16)


def _decode_attn(st, layer, q, k_new, v_new, cache_k, cache_v, lam, subln_g, lam_init):
    rows_bd = 2 * N_HEADS * DEC_SEQ
    tok = pl.BlockSpec((DEC_SEQ, ATTN_WIDTH), lambda b, t: (b, 0))
    tiles = PAST_LEN // TK_DEC
    cache_k_spec = pl.BlockSpec((None, ATTN_WIDTH, TK_DEC), lambda b, t: (layer * st.batch + b, 0, t))
    cache_v_spec = pl.BlockSpec((TK_DEC * N_HEADS, V_DIM),
                                lambda b, t: ((layer * st.batch + b) * tiles + t, 0))
    return pl.pallas_call(
        functools.partial(_decode_attn_kernel, lam_init=lam_init),
        grid=(st.batch, tiles),
        in_specs=[
            tok, cache_k_spec, cache_v_spec, tok, tok,
            pl.BlockSpec((None, 4, HEAD_DIM), lambda b, t: (layer, 0, 0)),
            pl.BlockSpec((None, 1, V_DIM), lambda b, t: (layer, 0, 0)),
        ],
        out_specs=tok,
        out_shape=jax.ShapeDtypeStruct((st.rows, ATTN_WIDTH), BF16),
        scratch_shapes=[
            pltpu.VMEM((rows_bd, 1), F32),
            pltpu.VMEM((rows_bd, 1), F32),
            pltpu.VMEM((rows_bd, V_DIM), F32),
        ],
        compiler_params=_params("arbitrary", "arbitrary"),
        name="decode_attn",
    )(q, cache_k, cache_v, k_new, v_new, lam, subln_g)


def _out_proj_kernel(x_ref, pool_ref, att_ref, w_ref, gate_ref, g_ref, sc_ref, sh_ref,
                     xo_ref, h_ref):
    mixed = (jnp.dot(pool_ref[...], w_ref[0:POOL_WIDTH, :], preferred_element_type=F32)
             + jnp.dot(att_ref[...], w_ref[POOL_WIDTH:, :], preferred_element_type=F32))
    x = x_ref[...] + gate_ref[...] * mixed
    xo_ref[...] = x
    y = _rms(x, NORM_EPS) * g_ref[...]
    h_ref[...] = (y * (1.0 + sc_ref[...]) + sh_ref[...]).astype(BF16)


def _out_proj(st, layer, x, pool_out, att, mod, w_out, norm_g):
    m = st.rows
    tm = TM
    row = lambda i, j: (i, 0)
    return pl.pallas_call(
        _out_proj_kernel,
        grid=(m // tm, 1),
        in_specs=[
            pl.BlockSpec((tm, D_MODEL), row),
            pl.BlockSpec((tm, POOL_WIDTH), row),
            pl.BlockSpec((tm, ATTN_WIDTH), row),
            pl.BlockSpec((None, D_MODEL, D_MODEL), lambda i, j: (layer, 0, 0),
                         pipeline_mode=pl.Buffered(1)),
            _mod_spec(st, layer, MOD_GATE_A, tm),
            pl.BlockSpec((None, 1, D_MODEL), lambda i, j: (layer, 0, 0)),
            _mod_spec(st, layer, MOD_SCALE_F, tm),
            _mod_spec(st, layer, MOD_SHIFT_F, tm),
        ],
        out_specs=[pl.BlockSpec((tm, D_MODEL), row), pl.BlockSpec((tm, D_MODEL), row)],
        out_shape=[jax.ShapeDtypeStruct((m, D_MODEL), F32), jax.ShapeDtypeStruct((m, D_MODEL), BF16)],
        compiler_params=_params("arbitrary", "arbitrary"),
        name="out_proj",
    )(x, pool_out, att, w_out, mod, norm_g, mod, mod)


def _gate_up_kernel(h_ref, wg_ref, wu_ref, o_ref):
    h = h_ref[...]
    g = jnp.dot(h, wg_ref[...], preferred_element_type=F32)
    u = jnp.dot(h, wu_ref[...], preferred_element_type=F32)
    o_ref[...] = (g * jax.nn.sigmoid(g) * u).astype(BF16)


def _gate_up(st, layer, h, w_gate, w_up):
    m = st.rows
    tm = min(TM_FFN, m)
    w_spec = pl.BlockSpec((None, D_MODEL, TF), lambda i, j: (layer, 0, j))
    return pl.pallas_call(
        _gate_up_kernel,
        grid=(m // tm, D_FF // TF),
        in_specs=[pl.BlockSpec((tm, D_MODEL), lambda i, j: (i, 0)), w_spec, w_spec],
        out_specs=pl.BlockSpec((tm, TF), lambda i, j: (i, j)),
        out_shape=jax.ShapeDtypeStruct((m, D_FF), BF16),
        compiler_params=_params("arbitrary", "arbitrary"),
        name="ffn_gate_up",
    )(h, w_gate, w_up)


def _down_kernel(a_ref, w_ref, x_ref, gate_ref, o_ref):
    y = jnp.dot(a_ref[...], w_ref[...], preferred_element_type=F32)
    o_ref[...] = x_ref[...] + gate_ref[...] * y


def _down(st, layer, act, x, mod, w_down):
    m = st.rows
    tm = min(TM_FFN, m)
    return pl.pallas_call(
        _down_kernel,
        grid=(m // tm, D_MODEL // TN_DOWN),
        in_specs=[
            pl.BlockSpec((tm, D_FF), lambda i, j: (i, 0)),
            pl.BlockSpec((None, D_FF, TN_DOWN), lambda i, j: (layer, 0, j)),
            pl.BlockSpec((tm, TN_DOWN), lambda i, j: (i, j)),
            _mod_spec(st, layer, MOD_GATE_F, tm, TN_DOWN),
        ],
        out_specs=pl.BlockSpec((tm, TN_DOWN), lambda i, j: (i, j)),
        out_shape=jax.ShapeDtypeStruct((m, D_MODEL), F32),
        compiler_params=_params("arbitrary", "arbitrary"),
        name="ffn_down",
    )(act, w_down, x, mod)


def _final_norm_kernel(x_ref, g_ref, o_ref):
    o_ref[...] = _rms(x_ref[...], NORM_EPS) * g_ref[...]


def _final_norm(x, g):
    m = x.shape[0]
    return pl.pallas_call(
        _final_norm_kernel,
        grid=(m // TM,),
        in_specs=[pl.BlockSpec((TM, D_MODEL), lambda i: (i, 0)), pl.BlockSpec((1, D_MODEL), lambda i: (0, 0))],
        out_specs=pl.BlockSpec((TM, D_MODEL), lambda i: (i, 0)),
        out_shape=jax.ShapeDtypeStruct((m, D_MODEL), F32),
        compiler_params=_params("arbitrary"),
        name="final_norm",
    )(x, g)


def _rope_tables(st):
    half = HEAD_DIM // 2
    inv = 1.0 / (ROPE_THETA ** (jnp.arange(half, dtype=F32) * (2.0 / HEAD_DIM)))
    pos = st.pos0 + jnp.arange(st.seq)
    ang = pos.astype(F32)[:, None] * inv[None, :]
    cos, sin = jnp.cos(ang), jnp.sin(ang)
    reps = V7X_LANES // HEAD_DIM
    cos_t = jnp.tile(jnp.concatenate([cos, cos], axis=-1), (1, reps))
    sin_t = jnp.tile(jnp.concatenate([-sin, sin], axis=-1), (1, reps))
    if st.seq < TM:
        cos_t = jnp.tile(cos_t, (TM // st.seq, 1))
        sin_t = jnp.tile(sin_t, (TM // st.seq, 1))
    return cos_t, sin_t


def _run_stream(st, x, mod, hist, caches, p):
    m = st.rows
    x = x.reshape(m, D_MODEL)
    cos_tab, sin_tab = _rope_tables(st)
    kv_stacks, us = None, []
    for l in range(DEPTH):
        lam_init = 0.8 - 0.6 * math.exp(-0.3 * l)
        u, q, k_stack, kb, v_stack, vb = _in_proj(st, l, x, mod, p["norm_mix_g"], p["w_in"],
                                                  cos_tab, sin_tab, kv_stacks)
        kv_stacks = (k_stack, v_stack)
        pool_out = _pool(st, l, u, hist[l], p["w_pool"], p["pool_scale"])
        if caches is None:
            att = _prompt_attn(st, l, q, kb, vb, p["lam"], p["subln_g"], lam_init)
        else:
            att = _decode_attn(st, l, q, kb, vb, caches[0], caches[1], p["lam"], p["subln_g"], lam_init)
        x, h = _out_proj(st, l, x, pool_out, att, mod, p["w_out"], p["norm_ffn_g"])
        act = _gate_up(st, l, h, p["w_gate"], p["w_up"])
        x = _down(st, l, act, x, mod, p["w_down"])
        us.append(u)
    y = _final_norm(x, p["final_g"]).reshape(st.batch, st.seq, D_MODEL)
    k_stack, v_stack = kv_stacks
    if st.k_transposed:
        k_out = k_stack.reshape(DEPTH, st.batch, 2 * N_HEADS, HEAD_DIM, st.seq)
        k_out = jnp.transpose(k_out, (0, 1, 4, 2, 3))
    else:
        k_out = k_stack.reshape(DEPTH, st.batch, st.seq, 2 * N_HEADS, HEAD_DIM)
    v_out = v_stack.reshape(DEPTH, st.batch, st.seq, N_HEADS, V_DIM)
    u_all = jnp.stack(us).reshape(DEPTH, st.batch, st.seq, POOL_WIDTH)
    return y, k_out, v_out, u_all[:, :, st.seq - POOL_HIST:]


def kernel(x_prompt, x_sample, c_prompt, c_sample, cache_k, cache_v, state_pool, w_mod, b_mod,
           norm_mix_g, w_in, w_pool, pool_scale, lam_q1, lam_k1, lam_q2, lam_k2, subln_g, w_out,
           norm_ffn_g, w_gate, w_up, w_down, final_g):
    p = {
        "norm_mix_g": norm_mix_g.reshape(DEPTH, 1, D_MODEL),
        "norm_ffn_g": norm_ffn_g.reshape(DEPTH, 1, D_MODEL),
        "final_g": final_g.reshape(1, D_MODEL),
        "w_in": w_in.astype(BF16),
        "w_pool": w_pool.astype(BF16),
        "pool_scale": pool_scale.reshape(DEPTH, 1, POOL_WIDTH),
        "lam": jnp.stack([lam_q1, lam_k1, lam_q2, lam_k2], axis=1),
        "subln_g": subln_g.reshape(DEPTH, 1, V_DIM),
        "w_out": w_out.astype(BF16),
        "w_gate": w_gate.astype(BF16),
        "w_up": w_up.astype(BF16),
        "w_down": w_down.astype(BF16),
    }
    c_all = jnp.concatenate([c_prompt, c_sample], axis=0)
    c_all = jnp.pad(c_all, ((0, -c_all.shape[0] % 8), (0, 0)))
    mod = _modulation(c_all, w_mod, b_mod)[:, :BATCH + DEC_BATCH]
    mod_prompt = mod[:, :BATCH].reshape(DEPTH * BATCH, 1, N_MOD * D_MODEL)
    mod_sample = jnp.repeat(mod[:, BATCH:], DEC_SEQ, axis=1)

    hist_prompt = jnp.zeros((DEPTH, BATCH, HALO, POOL_WIDTH), F32)
    hist_sample = jnp.pad(state_pool, ((0, 0), (0, 0), (HALO - POOL_HIST, 0), (0, 0)))
    caches = (jnp.transpose(cache_k, (0, 1, 3, 4, 2)).reshape(DEPTH * DEC_BATCH, ATTN_WIDTH, PAST_LEN),
              cache_v.reshape(-1, V_DIM))

    y_p, k_p, v_p, pool_p = _run_stream(PROMPT, x_prompt, mod_prompt, hist_prompt, None, p)
    y_s, k_s, v_s, pool_s = _run_stream(SAMPLE, x_sample, mod_sample, hist_sample, caches, p)
    return (y_p, y_s, k_p, v_p, pool_p, k_s, v_s, pool_s)
```

```python
import functools
import math
from typing import NamedTuple

import jax
import jax.numpy as jnp
from jax import lax
from jax.experimental import pallas as pl
from jax.experimental.pallas import tpu as pltpu

F32 = jnp.float32
BF16 = jnp.bfloat16

D_MODEL = 2048
BATCH = 4
SEQ = 4096
DEPTH = 4
DEC_BATCH = 32
DEC_SEQ = 16
PAST_LEN = 2048
CHUNK = 64
POOL_WIDTH = D_MODEL // 2
N_POOL_GROUPS = 4
POOL_GROUP = POOL_WIDTH // N_POOL_GROUPS
POOL_WINDOWS = (2, 4, 8, 16)
POOL_HIST = max(POOL_WINDOWS) - 1
HALO = POOL_HIST + 1
ATTN_WIDTH = D_MODEL - POOL_WIDTH
HEAD_DIM = 64
N_HEADS = ATTN_WIDTH // (2 * HEAD_DIM)
V_DIM = 2 * HEAD_DIM
IN_COLS = POOL_WIDTH + 3 * ATTN_WIDTH
IN_SECTIONS = IN_COLS // ATTN_WIDTH
D_FF = -(-8 * D_MODEL // (3 * 256)) * 256
ROPE_THETA = 10000.0
NORM_EPS = 1e-6
SUBLN_EPS = 1e-5
N_MOD = 6
MOD_SHIFT_A, MOD_SCALE_A, MOD_GATE_A, MOD_SHIFT_F, MOD_SCALE_F, MOD_GATE_F = range(N_MOD)

V7X_VMEM_BYTES = 64 * 1024 * 1024
V7X_LANES = 128
VMEM_LIMIT = V7X_VMEM_BYTES * 7 // 8

TM = 512
TM_FFN = 1024
TN_MOD = 1024
TF = 512
TN_DOWN = 512
TQ = 1024
TK = 512
DIAG_TILES = TQ // TK
SCORE_SCALE_LOG2 = HEAD_DIM ** -0.5 * math.log2(math.e)
TK_DEC = 1024


class Stream(NamedTuple):
    batch: int
    seq: int
    pos0: int
    mod_rows: int
    k_transposed: bool

    @property
    def rows(self):
        return self.batch * self.seq


PROMPT = Stream(BATCH, SEQ, 0, 1, True)
SAMPLE = Stream(DEC_BATCH, DEC_SEQ, PAST_LEN, DEC_BATCH * DEC_SEQ, False)


def _params(*semantics, flags=None):
    return pltpu.CompilerParams(dimension_semantics=semantics, vmem_limit_bytes=VMEM_LIMIT, flags=flags)


def _rms(x, eps, axis=-1):
    return x * lax.rsqrt(jnp.mean(x * x, axis=axis, keepdims=True) + eps)


def _mod_spec(st, layer, which, tm, tn=D_MODEL):
    per_chunk = D_MODEL // tn
    if st.mod_rows == 1:
        tiles_per_batch = st.seq // tm
        return pl.BlockSpec(
            (None, 1, tn),
            lambda i, j: (layer * st.batch + i // tiles_per_batch, 0, which * per_chunk + j % per_chunk))
    return pl.BlockSpec((None, tm, tn), lambda i, j: (layer, i, which * per_chunk + j % per_chunk))


def _mod_kernel(c_ref, w_ref, b_ref, o_ref):
    c = c_ref[...]
    cs = (c * jax.nn.sigmoid(c)).astype(BF16)
    o_ref[...] = jnp.dot(cs, w_ref[...].astype(BF16), preferred_element_type=F32) + b_ref[...]


def _modulation(c_all, w_mod, b_mod):
    nb = c_all.shape[0]
    n = N_MOD * D_MODEL
    return pl.pallas_call(
        _mod_kernel,
        grid=(DEPTH, n // TN_MOD),
        in_specs=[
            pl.BlockSpec((nb, D_MODEL), lambda l, j: (0, 0)),
            pl.BlockSpec((None, D_MODEL, TN_MOD), lambda l, j: (l, 0, j)),
            pl.BlockSpec((None, 1, TN_MOD), lambda l, j: (l, 0, j)),
        ],
        out_specs=pl.BlockSpec((None, nb, TN_MOD), lambda l, j: (l, 0, j)),
        out_shape=jax.ShapeDtypeStruct((DEPTH, nb, n), F32),
        compiler_params=_params("arbitrary", "arbitrary"),
        name="modulation",
    )(c_all, w_mod, b_mod.reshape(DEPTH, 1, n))


def _rope_store(acc, cos_ref, sin_ref, scale, out_refs, transposed_ref=None):
    tm = acc.shape[0]
    lane = lax.broadcasted_iota(jnp.int32, (tm, V7X_LANES), 1)
    first_half = (lane % HEAD_DIM) < (HEAD_DIM // 2)
    cos = cos_ref[...]
    sin = sin_ref[...]
    for c in range(ATTN_WIDTH // V7X_LANES):
        cols = slice(c * V7X_LANES, (c + 1) * V7X_LANES)
        xs = acc[:, cols]
        partner = jnp.where(first_half,
                            pltpu.roll(xs, V7X_LANES - HEAD_DIM // 2, 1),
                            pltpu.roll(xs, HEAD_DIM // 2, 1))
        y = xs * cos + partner * sin
        if scale != 1.0:
            y = y * scale
        for r in out_refs:
            r[:, cols] = y.astype(r.dtype)
        if transposed_ref is not None:
            transposed_ref[cols, :] = y.T


def _in_proj_kernel(x_ref, g_ref, sc_ref, sh_ref, w_ref, cos_ref, sin_ref, *rest,
                    k_transposed, tiles):
    u_ref, q_ref, k_ref, kb_ref, v_ref, vb_ref, h_cur, h_next, acc_a, acc_b = rest[-10:]
    s = pl.program_id(0)
    section = s % IN_SECTIONS
    last = IN_SECTIONS * tiles

    def normalise(dst):
        y = _rms(x_ref[...], NORM_EPS) * g_ref[...]
        dst[...] = (y * (1.0 + sc_ref[...]) + sh_ref[...]).astype(BF16)

    def multiply(dst):
        dst[...] = jnp.dot(h_cur[...], w_ref[...], preferred_element_type=F32)

    def finish_v(acc):
        for h in range(N_HEADS):
            v_ref[pl.ds(h, acc.shape[0], stride=N_HEADS), :] = acc[:, h * V_DIM:(h + 1) * V_DIM]
        vb_ref[...] = acc[...].astype(BF16)

    @pl.when(s == 0)
    def _():
        normalise(h_cur)
        multiply(acc_a)

    @pl.when((section == 0) & (s > 0) & (s < last))
    def _():
        h_cur[...] = h_next[...]
        multiply(acc_a)
        finish_v(acc_b)

    @pl.when(section == 1)
    def _():
        multiply(acc_b)
        u_ref[...] = acc_a[...]
        normalise(h_next)

    @pl.when(section == 2)
    def _():
        multiply(acc_a)
        _rope_store(acc_b, cos_ref, sin_ref, SCORE_SCALE_LOG2, (q_ref,))

    @pl.when(section == 3)
    def _():
        multiply(acc_b)
        if k_transposed:
            _rope_store(acc_a, cos_ref, sin_ref, 1.0, (kb_ref,), k_ref)
        else:
            _rope_store(acc_a, cos_ref, sin_ref, 1.0, (k_ref, kb_ref))

    @pl.when(s == last)
    def _():
        finish_v(acc_b)


def _in_proj(st, layer, x, mod, norm_g, w_in, cos_tab, sin_tab, kv_stacks):
    m = st.rows
    tm = TM
    tiles = m // tm
    tab_tiles = cos_tab.shape[0] // tm
    steps = IN_SECTIONS * tiles + 1
    done = lambda s: jnp.maximum(s - 1, 0) // IN_SECTIONS
    normed = lambda s: jnp.minimum((s + IN_SECTIONS - 1) // IN_SECTIONS, tiles - 1)
    out_spec = pl.BlockSpec((tm, ATTN_WIDTH), lambda s: (done(s), 0))
    stack_spec = pl.BlockSpec((tm, ATTN_WIDTH), lambda s: (layer * tiles + done(s), 0))
    sds = lambda dt: jax.ShapeDtypeStruct((m, ATTN_WIDTH), dt)
    stack_sds = jax.ShapeDtypeStruct((DEPTH * m, ATTN_WIDTH), F32)
    v_stack_spec = pl.BlockSpec((tm * N_HEADS, V_DIM), lambda s: (layer * tiles + done(s), 0))
    v_stack_sds = jax.ShapeDtypeStruct((DEPTH * m * N_HEADS, V_DIM), F32)
    out_specs = [out_spec, out_spec, stack_spec, out_spec, v_stack_spec, out_spec]
    out_shape = [sds(F32), sds(BF16), stack_sds, sds(BF16), v_stack_sds, sds(BF16)]
    if st.k_transposed:
        tiles_per_batch = st.seq // tm
        out_specs[2] = pl.BlockSpec(
            (None, ATTN_WIDTH, tm),
            lambda s: (layer * st.batch + done(s) // tiles_per_batch, 0, done(s) % tiles_per_batch))
        out_shape[2] = jax.ShapeDtypeStruct((DEPTH * st.batch, ATTN_WIDTH, st.seq), F32)
    if st.mod_rows == 1:
        tiles_per_batch = st.seq // tm
        mod_spec = lambda which: pl.BlockSpec(
            (None, 1, D_MODEL), lambda s: (layer * st.batch + normed(s) // tiles_per_batch, 0, which))
    else:
        mod_spec = lambda which: pl.BlockSpec((None, tm, D_MODEL), lambda s: (layer, normed(s), which),
                                              pipeline_mode=pl.Buffered(1))
    in_specs = [
        pl.BlockSpec((tm, D_MODEL), lambda s: (normed(s), 0)),
        pl.BlockSpec((None, 1, D_MODEL), lambda s: (layer, 0, 0)),
        mod_spec(MOD_SCALE_A),
        mod_spec(MOD_SHIFT_A),
        pl.BlockSpec((None, D_MODEL, ATTN_WIDTH),
                     lambda s: (layer, 0, jnp.minimum(s, steps - 2) % IN_SECTIONS)),
        pl.BlockSpec((tm, V7X_LANES), lambda s: (done(s) % tab_tiles, 0)),
        pl.BlockSpec((tm, V7X_LANES), lambda s: (done(s) % tab_tiles, 0)),
    ]
    args = [x, norm_g, mod, mod, w_in, cos_tab, sin_tab]
    aliases = {}
    if kv_stacks is not None:
        aliases = {len(args): 2, len(args) + 1: 4}
        in_specs += [pl.BlockSpec(memory_space=pl.ANY)] * 2
        args += list(kv_stacks)
    return pl.pallas_call(
        functools.partial(_in_proj_kernel, k_transposed=st.k_transposed, tiles=tiles),
        grid=(steps,),
        in_specs=in_specs,
        out_specs=out_specs,
        out_shape=out_shape,
        input_output_aliases=aliases,
        scratch_shapes=[pltpu.VMEM((tm, D_MODEL), BF16), pltpu.VMEM((tm, D_MODEL), BF16),
                        pltpu.VMEM((tm, ATTN_WIDTH), F32), pltpu.VMEM((tm, ATTN_WIDTH), F32)],
        compiler_params=_params("arbitrary"),
        name="in_proj",
    )(*args)


def _pool_kernel(u_ref, hist_ref, wp_ref, ps_ref, o_ref, tail_ref, ext_ref, *, tm, pos0):
    i = pl.program_id(1)

    @pl.when(i == 0)
    def _():
        ext_ref[0:HALO, :] = hist_ref[...]

    @pl.when(i > 0)
    def _():
        ext_ref[0:HALO, :] = ext_ref[tm:tm + HALO, :]

    ext_ref[HALO:, :] = u_ref[...]

    @pl.when(i == pl.num_programs(1) - 1)
    def _():
        tail_ref[...] = ext_ref[tm:tm + HALO, :]

    pos = pos0 + i * tm + lax.broadcasted_iota(jnp.int32, (tm, 1), 0)
    for g, w in enumerate(POOL_WINDOWS):
        cols = slice(g * POOL_GROUP, (g + 1) * POOL_GROUP)
        cur = u_ref[:, cols]
        s = cur
        for back in range(1, w):
            s = s + ext_ref[HALO - back:HALO - back + tm, cols]
        count = jnp.minimum(w, pos + 1).astype(F32)
        d = (s / count - cur).astype(BF16)
        y = jnp.dot(d, wp_ref[g], preferred_element_type=F32) * ps_ref[:, cols]
        o_ref[:, cols] = y.astype(BF16)


def _pool(st, layer, u, hist, w_pool, pool_scale):
    tm = min(TM, st.seq)
    tiles = st.seq // tm
    return pl.pallas_call(
        functools.partial(_pool_kernel, tm=tm, pos0=st.pos0),
        grid=(st.batch, tiles),
        in_specs=[
            pl.BlockSpec((tm, POOL_WIDTH), lambda b, i: (b * tiles + i, 0)),
            pl.BlockSpec((None, HALO, POOL_WIDTH), lambda b, i: (b, 0, 0)),
            pl.BlockSpec((None, N_POOL_GROUPS, POOL_GROUP, POOL_GROUP), lambda b, i: (layer, 0, 0, 0)),
            pl.BlockSpec((None, 1, POOL_WIDTH), lambda b, i: (layer, 0, 0)),
        ],
        out_specs=[pl.BlockSpec((tm, POOL_WIDTH), lambda b, i: (b * tiles + i, 0)),
                   pl.BlockSpec((None, HALO, POOL_WIDTH), lambda b, i: (b, 0, 0))],
        out_shape=[jax.ShapeDtypeStruct((st.rows, POOL_WIDTH), BF16),
                   jax.ShapeDtypeStruct((st.batch, HALO, POOL_WIDTH), F32)],
        scratch_shapes=[pltpu.VMEM((tm + HALO, POOL_WIDTH), F32)],
        compiler_params=_params("arbitrary", "arbitrary"),
        name="pool_mix",
    )(u, hist, w_pool, pool_scale)


def _lambda(lam_ref, lam_init):
    lam = lam_ref[...]
    s1 = jnp.sum(lam[0:1] * lam[1:2], axis=-1, keepdims=True)
    s2 = jnp.sum(lam[2:3] * lam[3:4], axis=-1, keepdims=True)
    return jnp.exp(s1) - jnp.exp(s2) + lam_init


def _online_softmax_step(s, v, m_ref, l_ref, acc_ref):
    m_prev = m_ref[...]
    m_new = jnp.maximum(m_prev, jnp.max(s, axis=-1, keepdims=True))
    alpha = jnp.exp2(m_prev - m_new)
    p = jnp.exp2(s - m_new)
    l_ref[...] = alpha * l_ref[...] + jnp.sum(p, axis=-1, keepdims=True)
    acc_ref[...] = alpha * acc_ref[...] + jnp.dot(p.astype(BF16), v, preferred_element_type=F32)
    m_ref[...] = m_new


def _sub_ln(o1, o2, lam, g, lam_init, axis=-1):
    a = o1 - lam * o2
    return (_rms(a, SUBLN_EPS, axis) * g) * (1.0 - lam_init)


_NT = (((1,), (1,)), ((), ()))


def _prompt_attn_kernel(q_ref, k_ref, v_ref, lam_ref, g_ref, o_ref,
                        qz_ref, vt_ref, sa_ref, sb_ref, m_ref, l_ref, acc_ref, *, lam_init):
    i = pl.program_id(2)

    @pl.when(i == 0)
    def _():
        for t in range(vt_ref.shape[0]):
            vt_ref[t] = v_ref[t * TK:(t + 1) * TK, :].astype(F32).T.astype(BF16)

    q = q_ref[...]
    lane = lax.broadcasted_iota(jnp.int32, (TQ, V_DIM), 1)
    zero = jnp.zeros_like(q)
    qz_ref[0:TQ, :] = jnp.where(lane < HEAD_DIM, q, zero)
    qz_ref[TQ:, :] = jnp.where(lane >= HEAD_DIM, q, zero)
    m_ref[...] = jnp.full_like(m_ref, -jnp.inf)
    l_ref[...] = jnp.zeros_like(l_ref)
    acc_ref[...] = jnp.zeros_like(acc_ref)

    def scores(t, s_ref):
        start = pl.multiple_of(t * TK, TK)
        s_ref[...] = lax.dot_general(k_ref[pl.ds(start, TK), :], qz_ref[...], _NT,
                                     preferred_element_type=F32)

    def consume(t, s_ref, diag=None):
        q_lo = 0 if diag is None else diag * TK
        width = TQ - q_lo
        col_sets = (slice(q_lo, TQ), slice(TQ + q_lo, 2 * TQ))

        def gather(ref):
            if q_lo == 0:
                return ref[...]
            return jnp.concatenate([ref[:, c] for c in col_sets], axis=1)

        def scatter(ref, val):
            if q_lo == 0:
                ref[...] = val
            else:
                for n, c in enumerate(col_sets):
                    ref[:, c] = val[:, n * width:(n + 1) * width]

        s = gather(s_ref)
        if diag is not None:
            key = q_lo + lax.broadcasted_iota(jnp.int32, (TK, 2 * width), 0)
            col = lax.broadcasted_iota(jnp.int32, (TK, 2 * width), 1)
            query = q_lo + jnp.where(col >= width, col - width, col)
            s = jnp.where((key // CHUNK) <= (query // CHUNK), s, -jnp.inf)
        m_prev = gather(m_ref)
        m_new = jnp.maximum(m_prev, jnp.max(s, axis=0, keepdims=True))
        alpha = jnp.exp2(m_prev - m_new)
        p = jnp.exp2(s - m_new)
        scatter(l_ref, alpha * gather(l_ref) + jnp.sum(p, axis=0, keepdims=True))
        scatter(acc_ref, alpha * gather(acc_ref) + jnp.dot(vt_ref[t], p.astype(BF16),
                                                           preferred_element_type=F32))
        scatter(m_ref, m_new)

    scores(0, sa_ref)

    def pair(p, carry):
        t = 2 * p
        scores(t + 1, sb_ref)
        consume(t, sa_ref)
        scores(t + 2, sa_ref)
        consume(t + 1, sb_ref)
        return carry

    lax.fori_loop(0, i, pair, 0)
    first_diag = DIAG_TILES * i
    scores(first_diag + 1, sb_ref)
    consume(first_diag, sa_ref, diag=0)
    consume(first_diag + 1, sb_ref, diag=1)

    o = acc_ref[...] / l_ref[...]
    y = _sub_ln(o[:, 0:TQ], o[:, TQ:], _lambda(lam_ref, lam_init), g_ref[...], lam_init, axis=0)
    o_ref[...] = y.T.astype(BF16)


def _prompt_attn(st, layer, q, k, v, lam, subln_g, lam_init):
    assert DIAG_TILES == 2
    nq = st.seq // TQ
    return pl.pallas_call(
        functools.partial(_prompt_attn_kernel, lam_init=lam_init),
        grid=(st.batch, N_HEADS, nq),
        in_specs=[
            pl.BlockSpec((TQ, V_DIM), lambda b, h, i: (b * nq + i, h)),
            pl.BlockSpec((st.seq, V_DIM), lambda b, h, i: (b, h)),
            pl.BlockSpec((st.seq, V_DIM), lambda b, h, i: (b, h)),
            pl.BlockSpec((None, 4, HEAD_DIM), lambda b, h, i: (layer, 0, 0)),
            pl.BlockSpec((None, V_DIM, 1), lambda b, h, i: (layer, 0, 0)),
        ],
        out_specs=pl.BlockSpec((TQ, V_DIM), lambda b, h, i: (b * nq + i, h)),
        out_shape=jax.ShapeDtypeStruct((st.rows, ATTN_WIDTH), BF16),
        scratch_shapes=[
            pltpu.VMEM((2 * TQ, V_DIM), BF16),
            pltpu.VMEM((st.seq // TK, V_DIM, TK), BF16),
            pltpu.VMEM((TK, 2 * TQ), F32),
            pltpu.VMEM((TK, 2 * TQ), F32),
            pltpu.VMEM((1, 2 * TQ), F32),
            pltpu.VMEM((1, 2 * TQ), F32),
            pltpu.VMEM((V_DIM, 2 * TQ), F32),
        ],
        compiler_params=_params("arbitrary", "arbitrary", "arbitrary"),
        name="prompt_attn",
    )(q, k, v, lam, subln_g.reshape(DEPTH, V_DIM, 1))


def _decode_attn_kernel(q_ref, kc_ref, vc_ref, kn_ref, vn_ref, lam_ref, g_ref, o_ref,
                        m_ref, l_ref, acc_ref, *, lam_init):
    t = pl.program_id(1)
    n_score_heads = 2 * N_HEADS
    pair_rows = 2 * DEC_SEQ

    @pl.when(t == 0)
    def _():
        m_ref[...] = jnp.full_like(m_ref, -jnp.inf)
        l_ref[...] = jnp.zeros_like(l_ref)
        acc_ref[...] = jnp.zeros_like(acc_ref)

    q = q_ref[...]
    q_heads = [q[:, h * HEAD_DIM:(h + 1) * HEAD_DIM] for h in range(n_score_heads)]

    def update(score_of_head, value_of_head):
        s = jnp.concatenate([score_of_head(h) for h in range(n_score_heads)], axis=0)
        m_prev = m_ref[...]
        m_new = jnp.maximum(m_prev, jnp.max(s, axis=-1, keepdims=True))
        alpha = jnp.exp2(m_prev - m_new)
        p = jnp.exp2(s - m_new)
        l_ref[...] = alpha * l_ref[...] + jnp.sum(p, axis=-1, keepdims=True)
        pb = p.astype(BF16)
        pv = jnp.concatenate(
            [jnp.dot(pb[j * pair_rows:(j + 1) * pair_rows], value_of_head(j), preferred_element_type=F32)
             for j in range(N_HEADS)], axis=0)
        acc_ref[...] = alpha * acc_ref[...] + pv
        m_ref[...] = m_new

    update(lambda h: jnp.dot(q_heads[h], kc_ref[h * HEAD_DIM:(h + 1) * HEAD_DIM, :].astype(BF16),
                             preferred_element_type=F32),
           lambda j: vc_ref[pl.ds(j, TK_DEC, stride=N_HEADS), :].astype(BF16))

    @pl.when(t == pl.num_programs(1) - 1)
    def _():
        update(lambda h: lax.dot_general(q_heads[h], kn_ref[:, h * HEAD_DIM:(h + 1) * HEAD_DIM], _NT,
                                         preferred_element_type=F32),
               lambda j: vn_ref[:, j * V_DIM:(j + 1) * V_DIM])
        o = acc_ref[...] / l_ref[...]
        lam = _lambda(lam_ref, lam_init)
        g = g_ref[...]
        for j in range(N_HEADS):
            r0 = j * pair_rows
            y = _sub_ln(o[r0:r0 + DEC_SEQ], o[r0 + DEC_SEQ:r0 + pair_rows], lam, g, lam_init)
            o_ref[:, j * V_DIM:(j + 1) * V_DIM] = y.astype(BF16)


def _decode_attn(st, layer, q, k_new, v_new, cache_k, cache_v, lam, subln_g, lam_init):
    rows_bd = 2 * N_HEADS * DEC_SEQ
    tok = pl.BlockSpec((DEC_SEQ, ATTN_WIDTH), lambda b, t: (b, 0))
    tiles = PAST_LEN // TK_DEC
    cache_k_spec = pl.BlockSpec((None, ATTN_WIDTH, TK_DEC), lambda b, t: (layer * st.batch + b, 0, t))
    cache_v_spec = pl.BlockSpec((TK_DEC * N_HEADS, V_DIM),
                                lambda b, t: ((layer * st.batch + b) * tiles + t, 0))
    return pl.pallas_call(
        functools.partial(_decode_attn_kernel, lam_init=lam_init),
        grid=(st.batch, tiles),
        in_specs=[
            tok, cache_k_spec, cache_v_spec, tok, tok,
            pl.BlockSpec((None, 4, HEAD_DIM), lambda b, t: (layer, 0, 0)),
            pl.BlockSpec((None, 1, V_DIM), lambda b, t: (layer, 0, 0)),
        ],
        out_specs=tok,
        out_shape=jax.ShapeDtypeStruct((st.rows, ATTN_WIDTH), BF16),
        scratch_shapes=[
            pltpu.VMEM((rows_bd, 1), F32),
            pltpu.VMEM((rows_bd, 1), F32),
            pltpu.VMEM((rows_bd, V_DIM), F32),
        ],
        compiler_params=_params("arbitrary", "arbitrary"),
        name="decode_attn",
    )(q, cache_k, cache_v, k_new, v_new, lam, subln_g)


def _out_proj_kernel(x_ref, pool_ref, att_ref, w_ref, gate_ref, g_ref, sc_ref, sh_ref,
                     xo_ref, h_ref):
    mixed = (jnp.dot(pool_ref[...], w_ref[0:POOL_WIDTH, :], preferred_element_type=F32)
             + jnp.dot(att_ref[...], w_ref[POOL_WIDTH:, :], preferred_element_type=F32))
    x = x_ref[...] + gate_ref[...] * mixed
    xo_ref[...] = x
    y = _rms(x, NORM_EPS) * g_ref[...]
    h_ref[...] = (y * (1.0 + sc_ref[...]) + sh_ref[...]).astype(BF16)


def _out_proj(st, layer, x, pool_out, att, mod, w_out, norm_g):
    m = st.rows
    tm = TM
    row = lambda i, j: (i, 0)
    return pl.pallas_call(
        _out_proj_kernel,
        grid=(m // tm, 1),
        in_specs=[
            pl.BlockSpec((tm, D_MODEL), row),
            pl.BlockSpec((tm, POOL_WIDTH), row),
            pl.BlockSpec((tm, ATTN_WIDTH), row),
            pl.BlockSpec((None, D_MODEL, D_MODEL), lambda i, j: (layer, 0, 0),
                         pipeline_mode=pl.Buffered(1)),
            _mod_spec(st, layer, MOD_GATE_A, tm),
            pl.BlockSpec((None, 1, D_MODEL), lambda i, j: (layer, 0, 0)),
            _mod_spec(st, layer, MOD_SCALE_F, tm),
            _mod_spec(st, layer, MOD_SHIFT_F, tm),
        ],
        out_specs=[pl.BlockSpec((tm, D_MODEL), row), pl.BlockSpec((tm, D_MODEL), row)],
        out_shape=[jax.ShapeDtypeStruct((m, D_MODEL), F32), jax.ShapeDtypeStruct((m, D_MODEL), BF16)],
        compiler_params=_params("arbitrary", "arbitrary"),
        name="out_proj",
    )(x, pool_out, att, w_out, mod, norm_g, mod, mod)


def _gate_up_kernel(h_ref, wg_ref, wu_ref, o_ref):
    h = h_ref[...]
    g = jnp.dot(h, wg_ref[...], preferred_element_type=F32)
    u = jnp.dot(h, wu_ref[...], preferred_element_type=F32)
    o_ref[...] = (g * jax.nn.sigmoid(g) * u).astype(BF16)


def _gate_up(st, layer, h, w_gate, w_up):
    m = st.rows
    tm = min(TM_FFN, m)
    w_spec = pl.BlockSpec((None, D_MODEL, TF), lambda i, j: (layer, 0, j))
    return pl.pallas_call(
        _gate_up_kernel,
        grid=(m // tm, D_FF // TF),
        in_specs=[pl.BlockSpec((tm, D_MODEL), lambda i, j: (i, 0)), w_spec, w_spec],
        out_specs=pl.BlockSpec((tm, TF), lambda i, j: (i, j)),
        out_shape=jax.ShapeDtypeStruct((m, D_FF), BF16),
        compiler_params=_params("arbitrary", "arbitrary"),
        name="ffn_gate_up",
    )(h, w_gate, w_up)


def _down_kernel(a_ref, w_ref, x_ref, gate_ref, o_ref):
    y = jnp.dot(a_ref[...], w_ref[...], preferred_element_type=F32)
    o_ref[...] = x_ref[...] + gate_ref[...] * y


def _down(st, layer, act, x, mod, w_down):
    m = st.rows
    tm = min(TM_FFN, m)
    return pl.pallas_call(
        _down_kernel,
        grid=(m // tm, D_MODEL // TN_DOWN),
        in_specs=[
            pl.BlockSpec((tm, D_FF), lambda i, j: (i, 0)),
            pl.BlockSpec((None, D_FF, TN_DOWN), lambda i, j: (layer, 0, j)),
            pl.BlockSpec((tm, TN_DOWN), lambda i, j: (i, j)),
            _mod_spec(st, layer, MOD_GATE_F, tm, TN_DOWN),
        ],
        out_specs=pl.BlockSpec((tm, TN_DOWN), lambda i, j: (i, j)),
        out_shape=jax.ShapeDtypeStruct((m, D_MODEL), F32),
        compiler_params=_params("arbitrary", "arbitrary"),
        name="ffn_down",
    )(act, w_down, x, mod)


def _final_norm_kernel(x_ref, g_ref, o_ref):
    o_ref[...] = _rms(x_ref[...], NORM_EPS) * g_ref[...]


def _final_norm(x, g):
    m = x.shape[0]
    return pl.pallas_call(
        _final_norm_kernel,
        grid=(m // TM,),
        in_specs=[pl.BlockSpec((TM, D_MODEL), lambda i: (i, 0)), pl.BlockSpec((1, D_MODEL), lambda i: (0, 0))],
        out_specs=pl.BlockSpec((TM, D_MODEL), lambda i: (i, 0)),
        out_shape=jax.ShapeDtypeStruct((m, D_MODEL), F32),
        compiler_params=_params("arbitrary"),
        name="final_norm",
    )(x, g)


def _rope_tables(st):
    half = HEAD_DIM // 2
    inv = 1.0 / (ROPE_THETA ** (jnp.arange(half, dtype=F32) * (2.0 / HEAD_DIM)))
    pos = st.pos0 + jnp.arange(st.seq)
    ang = pos.astype(F32)[:, None] * inv[None, :]
    cos, sin = jnp.cos(ang), jnp.sin(ang)
    reps = V7X_LANES // HEAD_DIM
    cos_t = jnp.tile(jnp.concatenate([cos, cos], axis=-1), (1, reps))
    sin_t = jnp.tile(jnp.concatenate([-sin, sin], axis=-1), (1, reps))
    if st.seq < TM:
        cos_t = jnp.tile(cos_t, (TM // st.seq, 1))
        sin_t = jnp.tile(sin_t, (TM // st.seq, 1))
    return cos_t, sin_t


def _run_stream(st, x, mod, hist, caches, p):
    m = st.rows
    x = x.reshape(m, D_MODEL)
    cos_tab, sin_tab = _rope_tables(st)
    kv_stacks, tails = None, []
    for l in range(DEPTH):
        lam_init = 0.8 - 0.6 * math.exp(-0.3 * l)
        u, q, k_stack, kb, v_stack, vb = _in_proj(st, l, x, mod, p["norm_mix_g"], p["w_in"],
                                                  cos_tab, sin_tab, kv_stacks)
        kv_stacks = (k_stack, v_stack)
        pool_out, tail = _pool(st, l, u, hist[l], p["w_pool"], p["pool_scale"])
        tails.append(tail)
        if caches is None:
            att = _prompt_attn(st, l, q, kb, vb, p["lam"], p["subln_g"], lam_init)
        else:
            att = _decode_attn(st, l, q, kb, vb, caches[0], caches[1], p["lam"], p["subln_g"], lam_init)
        x, h = _out_proj(st, l, x, pool_out, att, mod, p["w_out"], p["norm_ffn_g"])
        act = _gate_up(st, l, h, p["w_gate"], p["w_up"])
        x = _down(st, l, act, x, mod, p["w_down"])
    y = _final_norm(x, p["final_g"]).reshape(st.batch, st.seq, D_MODEL)
    k_stack, v_stack = kv_stacks
    if st.k_transposed:
        k_out = k_stack.reshape(DEPTH, st.batch, 2 * N_HEADS, HEAD_DIM, st.seq)
        k_out = jnp.transpose(k_out, (0, 1, 4, 2, 3))
    else:
        k_out = k_stack.reshape(DEPTH, st.batch, st.seq, 2 * N_HEADS, HEAD_DIM)
    v_out = v_stack.reshape(DEPTH, st.batch, st.seq, N_HEADS, V_DIM)
    return y, k_out, v_out, jnp.stack(tails)[:, :, HALO - POOL_HIST:]


def kernel(x_prompt, x_sample, c_prompt, c_sample, cache_k, cache_v, state_pool, w_mod, b_mod,
           norm_mix_g, w_in, w_pool, pool_scale, lam_q1, lam_k1, lam_q2, lam_k2, subln_g, w_out,
           norm_ffn_g, w_gate, w_up, w_down, final_g):
    p = {
        "norm_mix_g": norm_mix_g.reshape(DEPTH, 1, D_MODEL),
        "norm_ffn_g": norm_ffn_g.reshape(DEPTH, 1, D_MODEL),
        "final_g": final_g.reshape(1, D_MODEL),
        "w_in": w_in.astype(BF16),
        "w_pool": w_pool.astype(BF16),
        "pool_scale": pool_scale.reshape(DEPTH, 1, POOL_WIDTH),
        "lam": jnp.stack([lam_q1, lam_k1, lam_q2, lam_k2], axis=1),
        "subln_g": subln_g.reshape(DEPTH, 1, V_DIM),
        "w_out": w_out.astype(BF16),
        "w_gate": w_gate.astype(BF16),
        "w_up": w_up.astype(BF16),
        "w_down": w_down.astype(BF16),
    }
    c_all = jnp.concatenate([c_prompt, c_sample], axis=0)
    c_all = jnp.pad(c_all, ((0, -c_all.shape[0] % 8), (0, 0)))
    mod = _modulation(c_all, w_mod, b_mod)[:, :BATCH + DEC_BATCH]
    mod_prompt = mod[:, :BATCH].reshape(DEPTH * BATCH, 1, N_MOD * D_MODEL)
    mod_sample = jnp.repeat(mod[:, BATCH:], DEC_SEQ, axis=1)

    hist_prompt = jnp.zeros((DEPTH, BATCH, HALO, POOL_WIDTH), F32)
    hist_sample = jnp.pad(state_pool, ((0, 0), (0, 0), (HALO - POOL_HIST, 0), (0, 0)))
    caches = (jnp.transpose(cache_k, (0, 1, 3, 4, 2)).reshape(DEPTH * DEC_BATCH, ATTN_WIDTH, PAST_LEN),
              cache_v.reshape(-1, V_DIM))

    y_p, k_p, v_p, pool_p = _run_stream(PROMPT, x_prompt, mod_prompt, hist_prompt, None, p)
    y_s, k_s, v_s, pool_s = _run_stream(SAMPLE, x_sample, mod_sample, hist_sample, caches, p)
    return (y_p, y_s, k_p, v_p, pool_p, k_s, v_s, pool_s)
```

```python
import functools
import math
from typing import NamedTuple

import jax
import jax.numpy as jnp
from jax import lax
from jax.experimental import pallas as pl
from jax.experimental.pallas import tpu as pltpu

F32 = jnp.float32
BF16 = jnp.bfloat16

D_MODEL = 2048
BATCH = 4
SEQ = 4096
DEPTH = 4
DEC_BATCH = 32
DEC_SEQ = 16
PAST_LEN = 2048
CHUNK = 64
POOL_WIDTH = D_MODEL // 2
N_POOL_GROUPS = 4
POOL_GROUP = POOL_WIDTH // N_POOL_GROUPS
POOL_WINDOWS = (2, 4, 8, 16)
POOL_HIST = max(POOL_WINDOWS) - 1
HALO = POOL_HIST + 1
ATTN_WIDTH = D_MODEL - POOL_WIDTH
HEAD_DIM = 64
N_HEADS = ATTN_WIDTH // (2 * HEAD_DIM)
V_DIM = 2 * HEAD_DIM
IN_COLS = POOL_WIDTH + 3 * ATTN_WIDTH
IN_SECTIONS = IN_COLS // ATTN_WIDTH
D_FF = -(-8 * D_MODEL // (3 * 256)) * 256
ROPE_THETA = 10000.0
NORM_EPS = 1e-6
SUBLN_EPS = 1e-5
N_MOD = 6
MOD_SHIFT_A, MOD_SCALE_A, MOD_GATE_A, MOD_SHIFT_F, MOD_SCALE_F, MOD_GATE_F = range(N_MOD)

V7X_VMEM_BYTES = 64 * 1024 * 1024
V7X_LANES = 128
VMEM_LIMIT = V7X_VMEM_BYTES * 7 // 8

TM = 512
TM_FFN = 1024
TN_MOD = 1024
TF = 512
TN_DOWN = 512
TQ = 1024
TK = 512
DIAG_TILES = TQ // TK
SCORE_SCALE_LOG2 = HEAD_DIM ** -0.5 * math.log2(math.e)
TK_DEC = 1024


class Stream(NamedTuple):
    batch: int
    seq: int
    pos0: int
    mod_rows: int
    k_transposed: bool

    @property
    def rows(self):
        return self.batch * self.seq


PROMPT = Stream(BATCH, SEQ, 0, 1, True)
SAMPLE = Stream(DEC_BATCH, DEC_SEQ, PAST_LEN, DEC_BATCH * DEC_SEQ, False)


def _params(*semantics, flags=None):
    return pltpu.CompilerParams(dimension_semantics=semantics, vmem_limit_bytes=VMEM_LIMIT, flags=flags)


def _rms(x, eps, axis=-1):
    return x * lax.rsqrt(jnp.mean(x * x, axis=axis, keepdims=True) + eps)


def _mod_spec(st, layer, which, tm, tn=D_MODEL):
    per_chunk = D_MODEL // tn
    if st.mod_rows == 1:
        tiles_per_batch = st.seq // tm
        return pl.BlockSpec(
            (None, 1, tn),
            lambda i, j: (layer * st.batch + i // tiles_per_batch, 0, which * per_chunk + j % per_chunk))
    return pl.BlockSpec((None, tm, tn), lambda i, j: (layer, i, which * per_chunk + j % per_chunk))


def _mod_kernel(c_ref, w_ref, b_ref, o_ref):
    c = c_ref[...]
    cs = (c * jax.nn.sigmoid(c)).astype(BF16)
    o_ref[...] = jnp.dot(cs, w_ref[...].astype(BF16), preferred_element_type=F32) + b_ref[...]


def _modulation(c_all, w_mod, b_mod):
    nb = c_all.shape[0]
    n = N_MOD * D_MODEL
    return pl.pallas_call(
        _mod_kernel,
        grid=(DEPTH, n // TN_MOD),
        in_specs=[
            pl.BlockSpec((nb, D_MODEL), lambda l, j: (0, 0)),
            pl.BlockSpec((None, D_MODEL, TN_MOD), lambda l, j: (l, 0, j)),
            pl.BlockSpec((None, 1, TN_MOD), lambda l, j: (l, 0, j)),
        ],
        out_specs=pl.BlockSpec((None, nb, TN_MOD), lambda l, j: (l, 0, j)),
        out_shape=jax.ShapeDtypeStruct((DEPTH, nb, n), F32),
        compiler_params=_params("arbitrary", "arbitrary"),
        name="modulation",
    )(c_all, w_mod, b_mod.reshape(DEPTH, 1, n))


def _rope_store(acc, cos_ref, sin_ref, scale, out_refs, transposed_ref=None):
    tm = acc.shape[0]
    lane = lax.broadcasted_iota(jnp.int32, (tm, V7X_LANES), 1)
    first_half = (lane % HEAD_DIM) < (HEAD_DIM // 2)
    cos = cos_ref[...]
    sin = sin_ref[...]
    for c in range(ATTN_WIDTH // V7X_LANES):
        cols = slice(c * V7X_LANES, (c + 1) * V7X_LANES)
        xs = acc[:, cols]
        partner = jnp.where(first_half,
                            pltpu.roll(xs, V7X_LANES - HEAD_DIM // 2, 1),
                            pltpu.roll(xs, HEAD_DIM // 2, 1))
        y = xs * cos + partner * sin
        if scale != 1.0:
            y = y * scale
        for r in out_refs:
            r[:, cols] = y.astype(r.dtype)
        if transposed_ref is not None:
            transposed_ref[cols, :] = y.T


def _in_proj_kernel(x_ref, g_ref, sc_ref, sh_ref, w_ref, cos_ref, sin_ref, *rest,
                    k_transposed, tiles):
    u_ref, q_ref, k_ref, kb_ref, v_ref, vb_ref, h_cur, h_next, acc_a, acc_b = rest[-10:]
    s = pl.program_id(0)
    section = s % IN_SECTIONS
    last = IN_SECTIONS * tiles

    def normalise(dst):
        y = _rms(x_ref[...], NORM_EPS) * g_ref[...]
        dst[...] = (y * (1.0 + sc_ref[...]) + sh_ref[...]).astype(BF16)

    def multiply(dst):
        dst[...] = jnp.dot(h_cur[...], w_ref[...], preferred_element_type=F32)

    def finish_v(acc):
        for h in range(N_HEADS):
            v_ref[pl.ds(h, acc.shape[0], stride=N_HEADS), :] = acc[:, h * V_DIM:(h + 1) * V_DIM]
        vb_ref[...] = acc[...].astype(BF16)

    @pl.when(s == 0)
    def _():
        normalise(h_cur)
        multiply(acc_a)

    @pl.when((section == 0) & (s > 0) & (s < last))
    def _():
        h_cur[...] = h_next[...]
        multiply(acc_a)
        finish_v(acc_b)

    @pl.when(section == 1)
    def _():
        multiply(acc_b)
        u_ref[...] = acc_a[...]
        normalise(h_next)

    @pl.when(section == 2)
    def _():
        multiply(acc_a)
        _rope_store(acc_b, cos_ref, sin_ref, SCORE_SCALE_LOG2, (q_ref,))

    @pl.when(section == 3)
    def _():
        multiply(acc_b)
        if k_transposed:
            _rope_store(acc_a, cos_ref, sin_ref, 1.0, (kb_ref,), k_ref)
        else:
            _rope_store(acc_a, cos_ref, sin_ref, 1.0, (k_ref, kb_ref))

    @pl.when(s == last)
    def _():
        finish_v(acc_b)


def _in_proj(st, layer, x, mod, norm_g, w_in, cos_tab, sin_tab, kv_stacks):
    m = st.rows
    tm = TM
    tiles = m // tm
    tab_tiles = cos_tab.shape[0] // tm
    steps = IN_SECTIONS * tiles + 1
    done = lambda s: jnp.maximum(s - 1, 0) // IN_SECTIONS
    normed = lambda s: jnp.minimum((s + IN_SECTIONS - 1) // IN_SECTIONS, tiles - 1)
    out_spec = pl.BlockSpec((tm, ATTN_WIDTH), lambda s: (done(s), 0))
    stack_spec = pl.BlockSpec((tm, ATTN_WIDTH), lambda s: (layer * tiles + done(s), 0))
    sds = lambda dt: jax.ShapeDtypeStruct((m, ATTN_WIDTH), dt)
    stack_sds = jax.ShapeDtypeStruct((DEPTH * m, ATTN_WIDTH), F32)
    v_stack_spec = pl.BlockSpec((tm * N_HEADS, V_DIM), lambda s: (layer * tiles + done(s), 0))
    v_stack_sds = jax.ShapeDtypeStruct((DEPTH * m * N_HEADS, V_DIM), F32)
    out_specs = [out_spec, out_spec, stack_spec, out_spec, v_stack_spec, out_spec]
    out_shape = [sds(F32), sds(BF16), stack_sds, sds(BF16), v_stack_sds, sds(BF16)]
    if st.k_transposed:
        tiles_per_batch = st.seq // tm
        out_specs[2] = pl.BlockSpec(
            (None, ATTN_WIDTH, tm),
            lambda s: (layer * st.batch + done(s) // tiles_per_batch, 0, done(s) % tiles_per_batch))
        out_shape[2] = jax.ShapeDtypeStruct((DEPTH * st.batch, ATTN_WIDTH, st.seq), F32)
    if st.mod_rows == 1:
        tiles_per_batch = st.seq // tm
        mod_spec = lambda which: pl.BlockSpec(
            (None, 1, D_MODEL), lambda s: (layer * st.batch + normed(s) // tiles_per_batch, 0, which))
    else:
        mod_spec = lambda which: pl.BlockSpec((None, tm, D_MODEL), lambda s: (layer, normed(s), which),
                                              pipeline_mode=pl.Buffered(1))
    in_specs = [
        pl.BlockSpec((tm, D_MODEL), lambda s: (normed(s), 0)),
        pl.BlockSpec((None, 1, D_MODEL), lambda s: (layer, 0, 0)),
        mod_spec(MOD_SCALE_A),
        mod_spec(MOD_SHIFT_A),
        pl.BlockSpec((None, D_MODEL, ATTN_WIDTH),
                     lambda s: (layer, 0, jnp.minimum(s, steps - 2) % IN_SECTIONS)),
        pl.BlockSpec((tm, V7X_LANES), lambda s: (done(s) % tab_tiles, 0)),
        pl.BlockSpec((tm, V7X_LANES), lambda s: (done(s) % tab_tiles, 0)),
    ]
    args = [x, norm_g, mod, mod, w_in, cos_tab, sin_tab]
    aliases = {}
    if kv_stacks is not None:
        aliases = {len(args): 2, len(args) + 1: 4}
        in_specs += [pl.BlockSpec(memory_space=pl.ANY)] * 2
        args += list(kv_stacks)
    return pl.pallas_call(
        functools.partial(_in_proj_kernel, k_transposed=st.k_transposed, tiles=tiles),
        grid=(steps,),
        in_specs=in_specs,
        out_specs=out_specs,
        out_shape=out_shape,
        input_output_aliases=aliases,
        scratch_shapes=[pltpu.VMEM((tm, D_MODEL), BF16), pltpu.VMEM((tm, D_MODEL), BF16),
                        pltpu.VMEM((tm, ATTN_WIDTH), F32), pltpu.VMEM((tm, ATTN_WIDTH), F32)],
        compiler_params=_params("arbitrary"),
        name="in_proj",
    )(*args)


def _pool_kernel(u_ref, hist_ref, wp_ref, ps_ref, o_ref, tail_ref, ext_ref, *, tm, pos0):
    i = pl.program_id(1)

    @pl.when(i == 0)
    def _():
        ext_ref[0:HALO, :] = hist_ref[...]

    @pl.when(i > 0)
    def _():
        ext_ref[0:HALO, :] = ext_ref[tm:tm + HALO, :]

    ext_ref[HALO:, :] = u_ref[...]

    @pl.when(i == pl.num_programs(1) - 1)
    def _():
        tail_ref[...] = ext_ref[tm:tm + HALO, :]

    pos = pos0 + i * tm + lax.broadcasted_iota(jnp.int32, (tm, 1), 0)
    for g, w in enumerate(POOL_WINDOWS):
        cols = slice(g * POOL_GROUP, (g + 1) * POOL_GROUP)
        cur = u_ref[:, cols]
        s = ext_ref[:, cols]
        shift = 1
        while shift < w:
            s = s + pltpu.roll(s, shift, 0)
            shift *= 2
        s = s[HALO:]
        count = jnp.minimum(w, pos + 1).astype(F32)
        d = (s / count - cur).astype(BF16)
        y = jnp.dot(d, wp_ref[g], preferred_element_type=F32) * ps_ref[:, cols]
        o_ref[:, cols] = y.astype(BF16)


def _pool(st, layer, u, hist, w_pool, pool_scale):
    tm = min(TM, st.seq)
    tiles = st.seq // tm
    return pl.pallas_call(
        functools.partial(_pool_kernel, tm=tm, pos0=st.pos0),
        grid=(st.batch, tiles),
        in_specs=[
            pl.BlockSpec((tm, POOL_WIDTH), lambda b, i: (b * tiles + i, 0)),
            pl.BlockSpec((None, HALO, POOL_WIDTH), lambda b, i: (b, 0, 0)),
            pl.BlockSpec((None, N_POOL_GROUPS, POOL_GROUP, POOL_GROUP), lambda b, i: (layer, 0, 0, 0)),
            pl.BlockSpec((None, 1, POOL_WIDTH), lambda b, i: (layer, 0, 0)),
        ],
        out_specs=[pl.BlockSpec((tm, POOL_WIDTH), lambda b, i: (b * tiles + i, 0)),
                   pl.BlockSpec((None, HALO, POOL_WIDTH), lambda b, i: (b, 0, 0))],
        out_shape=[jax.ShapeDtypeStruct((st.rows, POOL_WIDTH), BF16),
                   jax.ShapeDtypeStruct((st.batch, HALO, POOL_WIDTH), F32)],
        scratch_shapes=[pltpu.VMEM((tm + HALO, POOL_WIDTH), F32)],
        compiler_params=_params("arbitrary", "arbitrary"),
        name="pool_mix",
    )(u, hist, w_pool, pool_scale)


def _lambda(lam_ref, lam_init):
    lam = lam_ref[...]
    s1 = jnp.sum(lam[0:1] * lam[1:2], axis=-1, keepdims=True)
    s2 = jnp.sum(lam[2:3] * lam[3:4], axis=-1, keepdims=True)
    return jnp.exp(s1) - jnp.exp(s2) + lam_init


def _online_softmax_step(s, v, m_ref, l_ref, acc_ref):
    m_prev = m_ref[...]
    m_new = jnp.maximum(m_prev, jnp.max(s, axis=-1, keepdims=True))
    alpha = jnp.exp2(m_prev - m_new)
    p = jnp.exp2(s - m_new)
    l_ref[...] = alpha * l_ref[...] + jnp.sum(p, axis=-1, keepdims=True)
    acc_ref[...] = alpha * acc_ref[...] + jnp.dot(p.astype(BF16), v, preferred_element_type=F32)
    m_ref[...] = m_new


def _sub_ln(o1, o2, lam, g, lam_init, axis=-1):
    a = o1 - lam * o2
    return (_rms(a, SUBLN_EPS, axis) * g) * (1.0 - lam_init)


_NT = (((1,), (1,)), ((), ()))


def _prompt_attn_kernel(q_ref, k_ref, v_ref, lam_ref, g_ref, o_ref,
                        qz_ref, vt_ref, sa_ref, sb_ref, m_ref, l_ref, acc_ref, *, lam_init):
    i = pl.program_id(2)

    @pl.when(i == 0)
    def _():
        for t in range(vt_ref.shape[0]):
            vt_ref[t] = v_ref[t * TK:(t + 1) * TK, :].astype(F32).T.astype(BF16)

    q = q_ref[...]
    lane = lax.broadcasted_iota(jnp.int32, (TQ, V_DIM), 1)
    zero = jnp.zeros_like(q)
    qz_ref[0:TQ, :] = jnp.where(lane < HEAD_DIM, q, zero)
    qz_ref[TQ:, :] = jnp.where(lane >= HEAD_DIM, q, zero)
    m_ref[...] = jnp.full_like(m_ref, -jnp.inf)
    l_ref[...] = jnp.zeros_like(l_ref)
    acc_ref[...] = jnp.zeros_like(acc_ref)

    def scores(t, s_ref):
        start = pl.multiple_of(t * TK, TK)
        s_ref[...] = lax.dot_general(k_ref[pl.ds(start, TK), :], qz_ref[...], _NT,
                                     preferred_element_type=F32)

    def consume(t, s_ref, diag=None):
        q_lo = 0 if diag is None else diag * TK
        width = TQ - q_lo
        col_sets = (slice(q_lo, TQ), slice(TQ + q_lo, 2 * TQ))

        def gather(ref):
            if q_lo == 0:
                return ref[...]
            return jnp.concatenate([ref[:, c] for c in col_sets], axis=1)

        def scatter(ref, val):
            if q_lo == 0:
                ref[...] = val
            else:
                for n, c in enumerate(col_sets):
                    ref[:, c] = val[:, n * width:(n + 1) * width]

        s = gather(s_ref)
        if diag is not None:
            key = q_lo + lax.broadcasted_iota(jnp.int32, (TK, 2 * width), 0)
            col = lax.broadcasted_iota(jnp.int32, (TK, 2 * width), 1)
            query = q_lo + jnp.where(col >= width, col - width, col)
            s = jnp.where((key // CHUNK) <= (query // CHUNK), s, -jnp.inf)
        m_prev = gather(m_ref)
        m_new = jnp.maximum(m_prev, jnp.max(s, axis=0, keepdims=True))
        alpha = jnp.exp2(m_prev - m_new)
        p = jnp.exp2(s - m_new)
        scatter(l_ref, alpha * gather(l_ref) + jnp.sum(p, axis=0, keepdims=True))
        scatter(acc_ref, alpha * gather(acc_ref) + jnp.dot(vt_ref[t], p.astype(BF16),
                                                           preferred_element_type=F32))
        scatter(m_ref, m_new)

    scores(0, sa_ref)

    def pair(p, carry):
        t = 2 * p
        scores(t + 1, sb_ref)
        consume(t, sa_ref)
        scores(t + 2, sa_ref)
        consume(t + 1, sb_ref)
        return carry

    lax.fori_loop(0, (DIAG_TILES // 2) * i, pair, 0)
    first_diag = DIAG_TILES * i
    buffers = (sa_ref, sb_ref)
    for d in range(DIAG_TILES):
        if d + 1 < DIAG_TILES:
            scores(first_diag + d + 1, buffers[(d + 1) % 2])
        consume(first_diag + d, buffers[d % 2], diag=d)

    o = acc_ref[...] / l_ref[...]
    y = _sub_ln(o[:, 0:TQ], o[:, TQ:], _lambda(lam_ref, lam_init), g_ref[...], lam_init, axis=0)
    o_ref[...] = y.T.astype(BF16)


def _prompt_attn(st, layer, q, k, v, lam, subln_g, lam_init):
    assert DIAG_TILES % 2 == 0
    nq = st.seq // TQ
    return pl.pallas_call(
        functools.partial(_prompt_attn_kernel, lam_init=lam_init),
        grid=(st.batch, N_HEADS, nq),
        in_specs=[
            pl.BlockSpec((TQ, V_DIM), lambda b, h, i: (b * nq + i, h)),
            pl.BlockSpec((st.seq, V_DIM), lambda b, h, i: (b, h)),
            pl.BlockSpec((st.seq, V_DIM), lambda b, h, i: (b, h)),
            pl.BlockSpec((None, 4, HEAD_DIM), lambda b, h, i: (layer, 0, 0)),
            pl.BlockSpec((None, V_DIM, 1), lambda b, h, i: (layer, 0, 0)),
        ],
        out_specs=pl.BlockSpec((TQ, V_DIM), lambda b, h, i: (b * nq + i, h)),
        out_shape=jax.ShapeDtypeStruct((st.rows, ATTN_WIDTH), BF16),
        scratch_shapes=[
            pltpu.VMEM((2 * TQ, V_DIM), BF16),
            pltpu.VMEM((st.seq // TK, V_DIM, TK), BF16),
            pltpu.VMEM((TK, 2 * TQ), F32),
            pltpu.VMEM((TK, 2 * TQ), F32),
            pltpu.VMEM((1, 2 * TQ), F32),
            pltpu.VMEM((1, 2 * TQ), F32),
            pltpu.VMEM((V_DIM, 2 * TQ), F32),
        ],
        compiler_params=_params("arbitrary", "arbitrary", "arbitrary"),
        name="prompt_attn",
    )(q, k, v, lam, subln_g.reshape(DEPTH, V_DIM, 1))


def _decode_attn_kernel(q_ref, kc_ref, vc_ref, kn_ref, vn_ref, lam_ref, g_ref, o_ref,
                        m_ref, l_ref, acc_ref, *, lam_init):
    t = pl.program_id(1)
    n_score_heads = 2 * N_HEADS
    pair_rows = 2 * DEC_SEQ

    @pl.when(t == 0)
    def _():
        m_ref[...] = jnp.full_like(m_ref, -jnp.inf)
        l_ref[...] = jnp.zeros_like(l_ref)
        acc_ref[...] = jnp.zeros_like(acc_ref)

    q = q_ref[...]
    q_heads = [q[:, h * HEAD_DIM:(h + 1) * HEAD_DIM] for h in range(n_score_heads)]

    def update(score_of_head, value_of_head):
        s = jnp.concatenate([score_of_head(h) for h in range(n_score_heads)], axis=0)
        m_prev = m_ref[...]
        m_new = jnp.maximum(m_prev, jnp.max(s, axis=-1, keepdims=True))
        alpha = jnp.exp2(m_prev - m_new)
        p = jnp.exp2(s - m_new)
        l_ref[...] = alpha * l_ref[...] + jnp.sum(p, axis=-1, keepdims=True)
        pb = p.astype(BF16)
        pv = jnp.concatenate(
            [jnp.dot(pb[j * pair_rows:(j + 1) * pair_rows], value_of_head(j), preferred_element_type=F32)
             for j in range(N_HEADS)], axis=0)
        acc_ref[...] = alpha * acc_ref[...] + pv
        m_ref[...] = m_new

    update(lambda h: jnp.dot(q_heads[h], kc_ref[h * HEAD_DIM:(h + 1) * HEAD_DIM, :].astype(BF16),
                             preferred_element_type=F32),
           lambda j: vc_ref[pl.ds(j, TK_DEC, stride=N_HEADS), :].astype(BF16))

    @pl.when(t == pl.num_programs(1) - 1)
    def _():
        update(lambda h: lax.dot_general(q_heads[h], kn_ref[:, h * HEAD_DIM:(h + 1) * HEAD_DIM], _NT,
                                         preferred_element_type=F32),
               lambda j: vn_ref[:, j * V_DIM:(j + 1) * V_DIM])
        o = acc_ref[...] / l_ref[...]
        lam = _lambda(lam_ref, lam_init)
        g = g_ref[...]
        for j in range(N_HEADS):
            r0 = j * pair_rows
            y = _sub_ln(o[r0:r0 + DEC_SEQ], o[r0 + DEC_SEQ:r0 + pair_rows], lam, g, lam_init)
            o_ref[:, j * V_DIM:(j + 1) * V_DIM] = y.astype(BF16)


def _decode_attn(st, layer, q, k_new, v_new, cache_k, cache_v, lam, subln_g, lam_init):
    rows_bd = 2 * N_HEADS * DEC_SEQ
    tok = pl.BlockSpec((DEC_SEQ, ATTN_WIDTH), lambda b, t: (b, 0))
    tiles = PAST_LEN // TK_DEC
    cache_k_spec = pl.BlockSpec((None, ATTN_WIDTH, TK_DEC), lambda b, t: (layer * st.batch + b, 0, t))
    cache_v_spec = pl.BlockSpec((TK_DEC * N_HEADS, V_DIM),
                                lambda b, t: ((layer * st.batch + b) * tiles + t, 0))
    return pl.pallas_call(
        functools.partial(_decode_attn_kernel, lam_init=lam_init),
        grid=(st.batch, tiles),
        in_specs=[
            tok, cache_k_spec, cache_v_spec, tok, tok,
            pl.BlockSpec((None, 4, HEAD_DIM), lambda b, t: (layer, 0, 0)),
            pl.BlockSpec((None, 1, V_DIM), lambda b, t: (layer, 0, 0)),
        ],
        out_specs=tok,
        out_shape=jax.ShapeDtypeStruct((st.rows, ATTN_WIDTH), BF16),
        scratch_shapes=[
            pltpu.VMEM((rows_bd, 1), F32),
            pltpu.VMEM((rows_bd, 1), F32),
            pltpu.VMEM((rows_bd, V_DIM), F32),
        ],
        compiler_params=_params("arbitrary", "arbitrary"),
        name="decode_attn",
    )(q, cache_k, cache_v, k_new, v_new, lam, subln_g)


def _out_proj_kernel(x_ref, pool_ref, att_ref, w_ref, gate_ref, g_ref, sc_ref, sh_ref,
                     xo_ref, h_ref):
    mixed = (jnp.dot(pool_ref[...], w_ref[0:POOL_WIDTH, :], preferred_element_type=F32)
             + jnp.dot(att_ref[...], w_ref[POOL_WIDTH:, :], preferred_element_type=F32))
    x = x_ref[...] + gate_ref[...] * mixed
    xo_ref[...] = x
    y = _rms(x, NORM_EPS) * g_ref[...]
    h_ref[...] = (y * (1.0 + sc_ref[...]) + sh_ref[...]).astype(BF16)


def _out_proj(st, layer, x, pool_out, att, mod, w_out, norm_g):
    m = st.rows
    tm = TM
    row = lambda i, j: (i, 0)
    return pl.pallas_call(
        _out_proj_kernel,
        grid=(m // tm, 1),
        in_specs=[
            pl.BlockSpec((tm, D_MODEL), row),
            pl.BlockSpec((tm, POOL_WIDTH), row),
            pl.BlockSpec((tm, ATTN_WIDTH), row),
            pl.BlockSpec((None, D_MODEL, D_MODEL), lambda i, j: (layer, 0, 0),
                         pipeline_mode=pl.Buffered(1)),
            _mod_spec(st, layer, MOD_GATE_A, tm),
            pl.BlockSpec((None, 1, D_MODEL), lambda i, j: (layer, 0, 0)),
            _mod_spec(st, layer, MOD_SCALE_F, tm),
            _mod_spec(st, layer, MOD_SHIFT_F, tm),
        ],
        out_specs=[pl.BlockSpec((tm, D_MODEL), row), pl.BlockSpec((tm, D_MODEL), row)],
        out_shape=[jax.ShapeDtypeStruct((m, D_MODEL), F32), jax.ShapeDtypeStruct((m, D_MODEL), BF16)],
        compiler_params=_params("arbitrary", "arbitrary"),
        name="out_proj",
    )(x, pool_out, att, w_out, mod, norm_g, mod, mod)


def _gate_up_kernel(h_ref, wg_ref, wu_ref, o_ref):
    h = h_ref[...]
    g = jnp.dot(h, wg_ref[...], preferred_element_type=F32)
    u = jnp.dot(h, wu_ref[...], preferred_element_type=F32)
    o_ref[...] = (g * jax.nn.sigmoid(g) * u).astype(BF16)


def _gate_up(st, layer, h, w_gate, w_up):
    m = st.rows
    tm = min(TM_FFN, m)
    w_spec = pl.BlockSpec((None, D_MODEL, TF), lambda i, j: (layer, 0, j))
    return pl.pallas_call(
        _gate_up_kernel,
        grid=(m // tm, D_FF // TF),
        in_specs=[pl.BlockSpec((tm, D_MODEL), lambda i, j: (i, 0)), w_spec, w_spec],
        out_specs=pl.BlockSpec((tm, TF), lambda i, j: (i, j)),
        out_shape=jax.ShapeDtypeStruct((m, D_FF), BF16),
        compiler_params=_params("arbitrary", "arbitrary"),
        name="ffn_gate_up",
    )(h, w_gate, w_up)


def _down_kernel(a_ref, w_ref, x_ref, gate_ref, o_ref):
    y = jnp.dot(a_ref[...], w_ref[...], preferred_element_type=F32)
    o_ref[...] = x_ref[...] + gate_ref[...] * y


def _down(st, layer, act, x, mod, w_down):
    m = st.rows
    tm = min(TM_FFN, m)
    return pl.pallas_call(
        _down_kernel,
        grid=(m // tm, D_MODEL // TN_DOWN),
        in_specs=[
            pl.BlockSpec((tm, D_FF), lambda i, j: (i, 0)),
            pl.BlockSpec((None, D_FF, TN_DOWN), lambda i, j: (layer, 0, j)),
            pl.BlockSpec((tm, TN_DOWN), lambda i, j: (i, j)),
            _mod_spec(st, layer, MOD_GATE_F, tm, TN_DOWN),
        ],
        out_specs=pl.BlockSpec((tm, TN_DOWN), lambda i, j: (i, j)),
        out_shape=jax.ShapeDtypeStruct((m, D_MODEL), F32),
        compiler_params=_params("arbitrary", "arbitrary"),
        name="ffn_down",
    )(act, w_down, x, mod)


def _final_norm_kernel(x_ref, g_ref, o_ref):
    o_ref[...] = _rms(x_ref[...], NORM_EPS) * g_ref[...]


def _final_norm(x, g):
    m = x.shape[0]
    return pl.pallas_call(
        _final_norm_kernel,
        grid=(m // TM,),
        in_specs=[pl.BlockSpec((TM, D_MODEL), lambda i: (i, 0)), pl.BlockSpec((1, D_MODEL), lambda i: (0, 0))],
        out_specs=pl.BlockSpec((TM, D_MODEL), lambda i: (i, 0)),
        out_shape=jax.ShapeDtypeStruct((m, D_MODEL), F32),
        compiler_params=_params("arbitrary"),
        name="final_norm",
    )(x, g)


def _rope_tables(st):
    half = HEAD_DIM // 2
    inv = 1.0 / (ROPE_THETA ** (jnp.arange(half, dtype=F32) * (2.0 / HEAD_DIM)))
    pos = st.pos0 + jnp.arange(st.seq)
    ang = pos.astype(F32)[:, None] * inv[None, :]
    cos, sin = jnp.cos(ang), jnp.sin(ang)
    reps = V7X_LANES // HEAD_DIM
    cos_t = jnp.tile(jnp.concatenate([cos, cos], axis=-1), (1, reps))
    sin_t = jnp.tile(jnp.concatenate([-sin, sin], axis=-1), (1, reps))
    if st.seq < TM:
        cos_t = jnp.tile(cos_t, (TM // st.seq, 1))
        sin_t = jnp.tile(sin_t, (TM // st.seq, 1))
    return cos_t, sin_t


def _run_stream(st, x, mod, hist, caches, p):
    m = st.rows
    x = x.reshape(m, D_MODEL)
    cos_tab, sin_tab = _rope_tables(st)
    kv_stacks, tails = None, []
    for l in range(DEPTH):
        lam_init = 0.8 - 0.6 * math.exp(-0.3 * l)
        u, q, k_stack, kb, v_stack, vb = _in_proj(st, l, x, mod, p["norm_mix_g"], p["w_in"],
                                                  cos_tab, sin_tab, kv_stacks)
        kv_stacks = (k_stack, v_stack)
        pool_out, tail = _pool(st, l, u, hist[l], p["w_pool"], p["pool_scale"])
        tails.append(tail)
        if caches is None:
            att = _prompt_attn(st, l, q, kb, vb, p["lam"], p["subln_g"], lam_init)
        else:
            att = _decode_attn(st, l, q, kb, vb, caches[0], caches[1], p["lam"], p["subln_g"], lam_init)
        x, h = _out_proj(st, l, x, pool_out, att, mod, p["w_out"], p["norm_ffn_g"])
        act = _gate_up(st, l, h, p["w_gate"], p["w_up"])
        x = _down(st, l, act, x, mod, p["w_down"])
    y = _final_norm(x, p["final_g"]).reshape(st.batch, st.seq, D_MODEL)
    k_stack, v_stack = kv_stacks
    if st.k_transposed:
        k_out = k_stack.reshape(DEPTH, st.batch, 2 * N_HEADS, HEAD_DIM, st.seq)
        k_out = jnp.transpose(k_out, (0, 1, 4, 2, 3))
    else:
        k_out = k_stack.reshape(DEPTH, st.batch, st.seq, 2 * N_HEADS, HEAD_DIM)
    v_out = v_stack.reshape(DEPTH, st.batch, st.seq, N_HEADS, V_DIM)
    return y, k_out, v_out, jnp.stack(tails)[:, :, HALO - POOL_HIST:]


def kernel(x_prompt, x_sample, c_prompt, c_sample, cache_k, cache_v, state_pool, w_mod, b_mod,
           norm_mix_g, w_in, w_pool, pool_scale, lam_q1, lam_k1, lam_q2, lam_k2, subln_g, w_out,
           norm_ffn_g, w_gate, w_up, w_down, final_g):
    p = {
        "norm_mix_g": norm_mix_g.reshape(DEPTH, 1, D_MODEL),
        "norm_ffn_g": norm_ffn_g.reshape(DEPTH, 1, D_MODEL),
        "final_g": final_g.reshape(1, D_MODEL),
        "w_in": w_in.astype(BF16),
        "w_pool": w_pool.astype(BF16),
        "pool_scale": pool_scale.reshape(DEPTH, 1, POOL_WIDTH),
        "lam": jnp.stack([lam_q1, lam_k1, lam_q2, lam_k2], axis=1),
        "subln_g": subln_g.reshape(DEPTH, 1, V_DIM),
        "w_out": w_out.astype(BF16),
        "w_gate": w_gate.astype(BF16),
        "w_up": w_up.astype(BF16),
        "w_down": w_down.astype(BF16),
    }
    c_all = jnp.concatenate([c_prompt, c_sample], axis=0)
    c_all = jnp.pad(c_all, ((0, -c_all.shape[0] % 8), (0, 0)))
    mod = _modulation(c_all, w_mod, b_mod)[:, :BATCH + DEC_BATCH]
    mod_prompt = mod[:, :BATCH].reshape(DEPTH * BATCH, 1, N_MOD * D_MODEL)
    mod_sample = jnp.repeat(mod[:, BATCH:], DEC_SEQ, axis=1)

    hist_prompt = jnp.zeros((DEPTH, BATCH, HALO, POOL_WIDTH), F32)
    hist_sample = jnp.pad(state_pool, ((0, 0), (0, 0), (HALO - POOL_HIST, 0), (0, 0)))
    caches = (jnp.transpose(cache_k, (0, 1, 3, 4, 2)).reshape(DEPTH * DEC_BATCH, ATTN_WIDTH, PAST_LEN),
              cache_v.reshape(-1, V_DIM))

    y_p, k_p, v_p, pool_p = _run_stream(PROMPT, x_prompt, mod_prompt, hist_prompt, None, p)
    y_s, k_s, v_s, pool_s = _run_stream(SAMPLE, x_sample, mod_sample, hist_sample, caches, p)
    return (y_p, y_s, k_p, v_p, pool_p, k_s, v_s, pool_s)
```

```python
import functools
import math
from typing import NamedTuple

import jax
import jax.numpy as jnp
from jax import lax
from jax.experimental import pallas as pl
from jax.experimental.pallas import tpu as pltpu

F32 = jnp.float32
BF16 = jnp.bfloat16

D_MODEL = 2048
BATCH = 4
SEQ = 4096
DEPTH = 4
DEC_BATCH = 32
DEC_SEQ = 16
PAST_LEN = 2048
CHUNK = 64
POOL_WIDTH = D_MODEL // 2
N_POOL_GROUPS = 4
POOL_GROUP = POOL_WIDTH // N_POOL_GROUPS
POOL_WINDOWS = (2, 4, 8, 16)
POOL_HIST = max(POOL_WINDOWS) - 1
HALO = POOL_HIST + 1
ATTN_WIDTH = D_MODEL - POOL_WIDTH
HEAD_DIM = 64
N_HEADS = ATTN_WIDTH // (2 * HEAD_DIM)
V_DIM = 2 * HEAD_DIM
IN_COLS = POOL_WIDTH + 3 * ATTN_WIDTH
IN_SECTIONS = IN_COLS // ATTN_WIDTH
D_FF = -(-8 * D_MODEL // (3 * 256)) * 256
ROPE_THETA = 10000.0
NORM_EPS = 1e-6
SUBLN_EPS = 1e-5
N_MOD = 6
MOD_SHIFT_A, MOD_SCALE_A, MOD_GATE_A, MOD_SHIFT_F, MOD_SCALE_F, MOD_GATE_F = range(N_MOD)

V7X_VMEM_BYTES = 64 * 1024 * 1024
V7X_LANES = 128
VMEM_LIMIT = V7X_VMEM_BYTES * 7 // 8

TM = 512
TM_FFN = 1024
TN_MOD = 1024
TF = 512
TN_DOWN = 512
TQ = 1024
TK = 512
DIAG_TILES = TQ // TK
SCORE_SCALE_LOG2 = HEAD_DIM ** -0.5 * math.log2(math.e)
TK_DEC = 2048


class Stream(NamedTuple):
    batch: int
    seq: int
    pos0: int
    mod_rows: int
    k_transposed: bool

    @property
    def rows(self):
        return self.batch * self.seq


PROMPT = Stream(BATCH, SEQ, 0, 1, True)
SAMPLE = Stream(DEC_BATCH, DEC_SEQ, PAST_LEN, DEC_BATCH * DEC_SEQ, False)


def _params(*semantics, flags=None):
    return pltpu.CompilerParams(dimension_semantics=semantics, vmem_limit_bytes=VMEM_LIMIT, flags=flags)


def _rms(x, eps, axis=-1):
    return x * lax.rsqrt(jnp.mean(x * x, axis=axis, keepdims=True) + eps)


def _mod_spec(st, layer, which, tm, tn=D_MODEL):
    per_chunk = D_MODEL // tn
    if st.mod_rows == 1:
        tiles_per_batch = st.seq // tm
        return pl.BlockSpec(
            (None, 1, tn),
            lambda i, j: (layer * st.batch + i // tiles_per_batch, 0, which * per_chunk + j % per_chunk))
    return pl.BlockSpec((None, tm, tn), lambda i, j: (layer, i, which * per_chunk + j % per_chunk))


def _mod_kernel(c_ref, w_ref, b_ref, o_ref):
    c = c_ref[...]
    cs = (c * jax.nn.sigmoid(c)).astype(BF16)
    o_ref[...] = jnp.dot(cs, w_ref[...].astype(BF16), preferred_element_type=F32) + b_ref[...]


def _modulation(c_all, w_mod, b_mod):
    nb = c_all.shape[0]
    n = N_MOD * D_MODEL
    return pl.pallas_call(
        _mod_kernel,
        grid=(DEPTH, n // TN_MOD),
        in_specs=[
            pl.BlockSpec((nb, D_MODEL), lambda l, j: (0, 0)),
            pl.BlockSpec((None, D_MODEL, TN_MOD), lambda l, j: (l, 0, j)),
            pl.BlockSpec((None, 1, TN_MOD), lambda l, j: (l, 0, j)),
        ],
        out_specs=pl.BlockSpec((None, nb, TN_MOD), lambda l, j: (l, 0, j)),
        out_shape=jax.ShapeDtypeStruct((DEPTH, nb, n), F32),
        compiler_params=_params("arbitrary", "arbitrary"),
        name="modulation",
    )(c_all, w_mod, b_mod.reshape(DEPTH, 1, n))


def _rope_store(acc, cos_ref, sin_ref, scale, out_refs, transposed_ref=None):
    tm = acc.shape[0]
    lane = lax.broadcasted_iota(jnp.int32, (tm, V7X_LANES), 1)
    first_half = (lane % HEAD_DIM) < (HEAD_DIM // 2)
    cos = cos_ref[...]
    sin = sin_ref[...]
    for c in range(ATTN_WIDTH // V7X_LANES):
        cols = slice(c * V7X_LANES, (c + 1) * V7X_LANES)
        xs = acc[:, cols]
        partner = jnp.where(first_half,
                            pltpu.roll(xs, V7X_LANES - HEAD_DIM // 2, 1),
                            pltpu.roll(xs, HEAD_DIM // 2, 1))
        y = xs * cos + partner * sin
        if scale != 1.0:
            y = y * scale
        for r in out_refs:
            r[:, cols] = y.astype(r.dtype)
        if transposed_ref is not None:
            transposed_ref[cols, :] = y.T


def _in_proj_kernel(x_ref, g_ref, sc_ref, sh_ref, w_ref, cos_ref, sin_ref, *rest,
                    k_transposed, tiles):
    u_ref, q_ref, k_ref, kb_ref, v_ref, vb_ref, h_cur, h_next, acc_a, acc_b = rest[-10:]
    s = pl.program_id(0)
    section = s % IN_SECTIONS
    last = IN_SECTIONS * tiles

    def normalise(dst):
        y = _rms(x_ref[...], NORM_EPS) * g_ref[...]
        dst[...] = (y * (1.0 + sc_ref[...]) + sh_ref[...]).astype(BF16)

    def multiply(dst):
        dst[...] = jnp.dot(h_cur[...], w_ref[...], preferred_element_type=F32)

    def finish_v(acc):
        for h in range(N_HEADS):
            v_ref[pl.ds(h, acc.shape[0], stride=N_HEADS), :] = acc[:, h * V_DIM:(h + 1) * V_DIM]
        vb_ref[...] = acc[...].astype(BF16)

    @pl.when(s == 0)
    def _():
        normalise(h_cur)
        multiply(acc_a)

    @pl.when((section == 0) & (s > 0) & (s < last))
    def _():
        h_cur[...] = h_next[...]
        multiply(acc_a)
        finish_v(acc_b)

    @pl.when(section == 1)
    def _():
        multiply(acc_b)
        u_ref[...] = acc_a[...]
        normalise(h_next)

    @pl.when(section == 2)
    def _():
        multiply(acc_a)
        _rope_store(acc_b, cos_ref, sin_ref, SCORE_SCALE_LOG2, (q_ref,))

    @pl.when(section == 3)
    def _():
        multiply(acc_b)
        if k_transposed:
            _rope_store(acc_a, cos_ref, sin_ref, 1.0, (kb_ref,), k_ref)
        else:
            _rope_store(acc_a, cos_ref, sin_ref, 1.0, (k_ref, kb_ref))

    @pl.when(s == last)
    def _():
        finish_v(acc_b)


def _in_proj(st, layer, x, mod, norm_g, w_in, cos_tab, sin_tab, kv_stacks):
    m = st.rows
    tm = TM
    tiles = m // tm
    tab_tiles = cos_tab.shape[0] // tm
    steps = IN_SECTIONS * tiles + 1
    done = lambda s: jnp.maximum(s - 1, 0) // IN_SECTIONS
    normed = lambda s: jnp.minimum((s + IN_SECTIONS - 1) // IN_SECTIONS, tiles - 1)
    out_spec = pl.BlockSpec((tm, ATTN_WIDTH), lambda s: (done(s), 0))
    stack_spec = pl.BlockSpec((tm, ATTN_WIDTH), lambda s: (layer * tiles + done(s), 0))
    sds = lambda dt: jax.ShapeDtypeStruct((m, ATTN_WIDTH), dt)
    stack_sds = jax.ShapeDtypeStruct((DEPTH * m, ATTN_WIDTH), F32)
    v_stack_spec = pl.BlockSpec((tm * N_HEADS, V_DIM), lambda s: (layer * tiles + done(s), 0))
    v_stack_sds = jax.ShapeDtypeStruct((DEPTH * m * N_HEADS, V_DIM), F32)
    out_specs = [out_spec, out_spec, stack_spec, out_spec, v_stack_spec, out_spec]
    out_shape = [sds(F32), sds(BF16), stack_sds, sds(BF16), v_stack_sds, sds(BF16)]
    if st.k_transposed:
        tiles_per_batch = st.seq // tm
        out_specs[2] = pl.BlockSpec(
            (None, ATTN_WIDTH, tm),
            lambda s: (layer * st.batch + done(s) // tiles_per_batch, 0, done(s) % tiles_per_batch))
        out_shape[2] = jax.ShapeDtypeStruct((DEPTH * st.batch, ATTN_WIDTH, st.seq), F32)
    if st.mod_rows == 1:
        tiles_per_batch = st.seq // tm
        mod_spec = lambda which: pl.BlockSpec(
            (None, 1, D_MODEL), lambda s: (layer * st.batch + normed(s) // tiles_per_batch, 0, which))
    else:
        mod_spec = lambda which: pl.BlockSpec((None, tm, D_MODEL), lambda s: (layer, normed(s), which),
                                              pipeline_mode=pl.Buffered(1))
    in_specs = [
        pl.BlockSpec((tm, D_MODEL), lambda s: (normed(s), 0)),
        pl.BlockSpec((None, 1, D_MODEL), lambda s: (layer, 0, 0)),
        mod_spec(MOD_SCALE_A),
        mod_spec(MOD_SHIFT_A),
        pl.BlockSpec((None, D_MODEL, ATTN_WIDTH),
                     lambda s: (layer, 0, jnp.minimum(s, steps - 2) % IN_SECTIONS)),
        pl.BlockSpec((tm, V7X_LANES), lambda s: (done(s) % tab_tiles, 0)),
        pl.BlockSpec((tm, V7X_LANES), lambda s: (done(s) % tab_tiles, 0)),
    ]
    args = [x, norm_g, mod, mod, w_in, cos_tab, sin_tab]
    aliases = {}
    if kv_stacks is not None:
        aliases = {len(args): 2, len(args) + 1: 4}
        in_specs += [pl.BlockSpec(memory_space=pl.ANY)] * 2
        args += list(kv_stacks)
    return pl.pallas_call(
        functools.partial(_in_proj_kernel, k_transposed=st.k_transposed, tiles=tiles),
        grid=(steps,),
        in_specs=in_specs,
        out_specs=out_specs,
        out_shape=out_shape,
        input_output_aliases=aliases,
        scratch_shapes=[pltpu.VMEM((tm, D_MODEL), BF16), pltpu.VMEM((tm, D_MODEL), BF16),
                        pltpu.VMEM((tm, ATTN_WIDTH), F32), pltpu.VMEM((tm, ATTN_WIDTH), F32)],
        compiler_params=_params("arbitrary"),
        name="in_proj",
    )(*args)


def _pool_kernel(u_ref, hist_ref, wp_ref, ps_ref, o_ref, tail_ref, ext_ref, *, tm, pos0):
    i = pl.program_id(1)

    @pl.when(i == 0)
    def _():
        ext_ref[0:HALO, :] = hist_ref[...]

    @pl.when(i > 0)
    def _():
        ext_ref[0:HALO, :] = ext_ref[tm:tm + HALO, :]

    ext_ref[HALO:, :] = u_ref[...]

    @pl.when(i == pl.num_programs(1) - 1)
    def _():
        tail_ref[...] = ext_ref[tm:tm + HALO, :]

    pos = pos0 + i * tm + lax.broadcasted_iota(jnp.int32, (tm, 1), 0)
    for g, w in enumerate(POOL_WINDOWS):
        cols = slice(g * POOL_GROUP, (g + 1) * POOL_GROUP)
        cur = u_ref[:, cols]
        s = ext_ref[:, cols]
        shift = 1
        while shift < w:
            s = s + pltpu.roll(s, shift, 0)
            shift *= 2
        s = s[HALO:]
        count = jnp.minimum(w, pos + 1).astype(F32)
        d = (s / count - cur).astype(BF16)
        y = jnp.dot(d, wp_ref[g], preferred_element_type=F32) * ps_ref[:, cols]
        o_ref[:, cols] = y.astype(BF16)


def _pool(st, layer, u, hist, w_pool, pool_scale):
    tm = min(TM, st.seq)
    tiles = st.seq // tm
    return pl.pallas_call(
        functools.partial(_pool_kernel, tm=tm, pos0=st.pos0),
        grid=(st.batch, tiles),
        in_specs=[
            pl.BlockSpec((tm, POOL_WIDTH), lambda b, i: (b * tiles + i, 0)),
            pl.BlockSpec((None, HALO, POOL_WIDTH), lambda b, i: (b, 0, 0)),
            pl.BlockSpec((None, N_POOL_GROUPS, POOL_GROUP, POOL_GROUP), lambda b, i: (layer, 0, 0, 0)),
            pl.BlockSpec((None, 1, POOL_WIDTH), lambda b, i: (layer, 0, 0)),
        ],
        out_specs=[pl.BlockSpec((tm, POOL_WIDTH), lambda b, i: (b * tiles + i, 0)),
                   pl.BlockSpec((None, HALO, POOL_WIDTH), lambda b, i: (b, 0, 0))],
        out_shape=[jax.ShapeDtypeStruct((st.rows, POOL_WIDTH), BF16),
                   jax.ShapeDtypeStruct((st.batch, HALO, POOL_WIDTH), F32)],
        scratch_shapes=[pltpu.VMEM((tm + HALO, POOL_WIDTH), F32)],
        compiler_params=_params("arbitrary", "arbitrary"),
        name="pool_mix",
    )(u, hist, w_pool, pool_scale)


def _lambda(lam_ref, lam_init):
    lam = lam_ref[...]
    s1 = jnp.sum(lam[0:1] * lam[1:2], axis=-1, keepdims=True)
    s2 = jnp.sum(lam[2:3] * lam[3:4], axis=-1, keepdims=True)
    return jnp.exp(s1) - jnp.exp(s2) + lam_init


def _online_softmax_step(s, v, m_ref, l_ref, acc_ref):
    m_prev = m_ref[...]
    m_new = jnp.maximum(m_prev, jnp.max(s, axis=-1, keepdims=True))
    alpha = jnp.exp2(m_prev - m_new)
    p = jnp.exp2(s - m_new)
    l_ref[...] = alpha * l_ref[...] + jnp.sum(p, axis=-1, keepdims=True)
    acc_ref[...] = alpha * acc_ref[...] + jnp.dot(p.astype(BF16), v, preferred_element_type=F32)
    m_ref[...] = m_new


def _sub_ln(o1, o2, lam, g, lam_init, axis=-1):
    a = o1 - lam * o2
    return (_rms(a, SUBLN_EPS, axis) * g) * (1.0 - lam_init)


_NT = (((1,), (1,)), ((), ()))


def _prompt_attn_kernel(q_ref, k_ref, v_ref, lam_ref, g_ref, o_ref,
                        qz_ref, vt_ref, sa_ref, sb_ref, m_ref, l_ref, acc_ref, *, lam_init):
    i = pl.program_id(2)

    @pl.when(i == 0)
    def _():
        for t in range(vt_ref.shape[0]):
            vt_ref[t] = v_ref[t * TK:(t + 1) * TK, :].astype(F32).T.astype(BF16)

    q = q_ref[...]
    lane = lax.broadcasted_iota(jnp.int32, (TQ, V_DIM), 1)
    zero = jnp.zeros_like(q)
    qz_ref[0:TQ, :] = jnp.where(lane < HEAD_DIM, q, zero)
    qz_ref[TQ:, :] = jnp.where(lane >= HEAD_DIM, q, zero)
    m_ref[...] = jnp.full_like(m_ref, -jnp.inf)
    l_ref[...] = jnp.zeros_like(l_ref)
    acc_ref[...] = jnp.zeros_like(acc_ref)

    def scores(t, s_ref):
        start = pl.multiple_of(t * TK, TK)
        s_ref[...] = lax.dot_general(k_ref[pl.ds(start, TK), :], qz_ref[...], _NT,
                                     preferred_element_type=F32)

    def consume(t, s_ref, diag=None):
        q_lo = 0 if diag is None else diag * TK
        width = TQ - q_lo
        col_sets = (slice(q_lo, TQ), slice(TQ + q_lo, 2 * TQ))

        def gather(ref):
            if q_lo == 0:
                return ref[...]
            return jnp.concatenate([ref[:, c] for c in col_sets], axis=1)

        def scatter(ref, val):
            if q_lo == 0:
                ref[...] = val
            else:
                for n, c in enumerate(col_sets):
                    ref[:, c] = val[:, n * width:(n + 1) * width]

        s = gather(s_ref)
        if diag is not None:
            key = q_lo + lax.broadcasted_iota(jnp.int32, (TK, 2 * width), 0)
            col = lax.broadcasted_iota(jnp.int32, (TK, 2 * width), 1)
            query = q_lo + jnp.where(col >= width, col - width, col)
            s = jnp.where((key // CHUNK) <= (query // CHUNK), s, -jnp.inf)
        m_prev = gather(m_ref)
        m_new = jnp.maximum(m_prev, jnp.max(s, axis=0, keepdims=True))
        alpha = jnp.exp2(m_prev - m_new)
        p = jnp.exp2(s - m_new)
        scatter(l_ref, alpha * gather(l_ref) + jnp.sum(p, axis=0, keepdims=True))
        scatter(acc_ref, alpha * gather(acc_ref) + jnp.dot(vt_ref[t], p.astype(BF16),
                                                           preferred_element_type=F32))
        scatter(m_ref, m_new)

    scores(0, sa_ref)

    def pair(p, carry):
        t = 2 * p
        scores(t + 1, sb_ref)
        consume(t, sa_ref)
        scores(t + 2, sa_ref)
        consume(t + 1, sb_ref)
        return carry

    lax.fori_loop(0, (DIAG_TILES // 2) * i, pair, 0)
    first_diag = DIAG_TILES * i
    buffers = (sa_ref, sb_ref)
    for d in range(DIAG_TILES):
        if d + 1 < DIAG_TILES:
            scores(first_diag + d + 1, buffers[(d + 1) % 2])
        consume(first_diag + d, buffers[d % 2], diag=d)

    o = acc_ref[...] / l_ref[...]
    y = _sub_ln(o[:, 0:TQ], o[:, TQ:], _lambda(lam_ref, lam_init), g_ref[...], lam_init, axis=0)
    o_ref[...] = y.T.astype(BF16)


def _prompt_attn(st, layer, q, k, v, lam, subln_g, lam_init):
    assert DIAG_TILES % 2 == 0
    nq = st.seq // TQ
    return pl.pallas_call(
        functools.partial(_prompt_attn_kernel, lam_init=lam_init),
        grid=(st.batch, N_HEADS, nq),
        in_specs=[
            pl.BlockSpec((TQ, V_DIM), lambda b, h, i: (b * nq + i, h)),
            pl.BlockSpec((st.seq, V_DIM), lambda b, h, i: (b, h)),
            pl.BlockSpec((st.seq, V_DIM), lambda b, h, i: (b, h)),
            pl.BlockSpec((None, 4, HEAD_DIM), lambda b, h, i: (layer, 0, 0)),
            pl.BlockSpec((None, V_DIM, 1), lambda b, h, i: (layer, 0, 0)),
        ],
        out_specs=pl.BlockSpec((TQ, V_DIM), lambda b, h, i: (b * nq + i, h)),
        out_shape=jax.ShapeDtypeStruct((st.rows, ATTN_WIDTH), BF16),
        scratch_shapes=[
            pltpu.VMEM((2 * TQ, V_DIM), BF16),
            pltpu.VMEM((st.seq // TK, V_DIM, TK), BF16),
            pltpu.VMEM((TK, 2 * TQ), F32),
            pltpu.VMEM((TK, 2 * TQ), F32),
            pltpu.VMEM((1, 2 * TQ), F32),
            pltpu.VMEM((1, 2 * TQ), F32),
            pltpu.VMEM((V_DIM, 2 * TQ), F32),
        ],
        compiler_params=_params("arbitrary", "arbitrary", "arbitrary"),
        name="prompt_attn",
    )(q, k, v, lam, subln_g.reshape(DEPTH, V_DIM, 1))


def _decode_attn_kernel(q_ref, kc_ref, vc_ref, kn_ref, vn_ref, lam_ref, g_ref, o_ref,
                        m_ref, l_ref, acc_ref, *, lam_init):
    t = pl.program_id(1)
    n_score_heads = 2 * N_HEADS
    pair_rows = 2 * DEC_SEQ

    @pl.when(t == 0)
    def _():
        m_ref[...] = jnp.full_like(m_ref, -jnp.inf)
        l_ref[...] = jnp.zeros_like(l_ref)
        acc_ref[...] = jnp.zeros_like(acc_ref)

    q = q_ref[...]
    q_heads = [q[:, h * HEAD_DIM:(h + 1) * HEAD_DIM] for h in range(n_score_heads)]

    def update(score_of_head, value_of_head):
        s = jnp.concatenate([score_of_head(h) for h in range(n_score_heads)], axis=0)
        m_prev = m_ref[...]
        m_new = jnp.maximum(m_prev, jnp.max(s, axis=-1, keepdims=True))
        alpha = jnp.exp2(m_prev - m_new)
        p = jnp.exp2(s - m_new)
        l_ref[...] = alpha * l_ref[...] + jnp.sum(p, axis=-1, keepdims=True)
        pb = p.astype(BF16)
        pv = jnp.concatenate(
            [jnp.dot(pb[j * pair_rows:(j + 1) * pair_rows], value_of_head(j), preferred_element_type=F32)
             for j in range(N_HEADS)], axis=0)
        acc_ref[...] = alpha * acc_ref[...] + pv
        m_ref[...] = m_new

    update(lambda h: jnp.dot(q_heads[h], kc_ref[h * HEAD_DIM:(h + 1) * HEAD_DIM, :].astype(BF16),
                             preferred_element_type=F32),
           lambda j: vc_ref[pl.ds(j, TK_DEC, stride=N_HEADS), :].astype(BF16))

    @pl.when(t == pl.num_programs(1) - 1)
    def _():
        update(lambda h: lax.dot_general(q_heads[h], kn_ref[:, h * HEAD_DIM:(h + 1) * HEAD_DIM], _NT,
                                         preferred_element_type=F32),
               lambda j: vn_ref[:, j * V_DIM:(j + 1) * V_DIM])
        o = acc_ref[...] / l_ref[...]
        lam = _lambda(lam_ref, lam_init)
        g = g_ref[...]
        for j in range(N_HEADS):
            r0 = j * pair_rows
            y = _sub_ln(o[r0:r0 + DEC_SEQ], o[r0 + DEC_SEQ:r0 + pair_rows], lam, g, lam_init)
            o_ref[:, j * V_DIM:(j + 1) * V_DIM] = y.astype(BF16)


def _decode_attn(st, layer, q, k_new, v_new, cache_k, cache_v, lam, subln_g, lam_init):
    rows_bd = 2 * N_HEADS * DEC_SEQ
    tok = pl.BlockSpec((DEC_SEQ, ATTN_WIDTH), lambda b, t: (b, 0))
    tiles = PAST_LEN // TK_DEC
    cache_k_spec = pl.BlockSpec((None, ATTN_WIDTH, TK_DEC), lambda b, t: (layer * st.batch + b, 0, t))
    cache_v_spec = pl.BlockSpec((TK_DEC * N_HEADS, V_DIM),
                                lambda b, t: ((layer * st.batch + b) * tiles + t, 0))
    return pl.pallas_call(
        functools.partial(_decode_attn_kernel, lam_init=lam_init),
        grid=(st.batch, tiles),
        in_specs=[
            tok, cache_k_spec, cache_v_spec, tok, tok,
            pl.BlockSpec((None, 4, HEAD_DIM), lambda b, t: (layer, 0, 0)),
            pl.BlockSpec((None, 1, V_DIM), lambda b, t: (layer, 0, 0)),
        ],
        out_specs=tok,
        out_shape=jax.ShapeDtypeStruct((st.rows, ATTN_WIDTH), BF16),
        scratch_shapes=[
            pltpu.VMEM((rows_bd, 1), F32),
            pltpu.VMEM((rows_bd, 1), F32),
            pltpu.VMEM((rows_bd, V_DIM), F32),
        ],
        compiler_params=_params("arbitrary", "arbitrary"),
        name="decode_attn",
    )(q, cache_k, cache_v, k_new, v_new, lam, subln_g)


def _out_proj_kernel(x_ref, pool_ref, att_ref, w_ref, gate_ref, g_ref, sc_ref, sh_ref,
                     xo_ref, h_ref):
    mixed = (jnp.dot(pool_ref[...], w_ref[0:POOL_WIDTH, :], preferred_element_type=F32)
             + jnp.dot(att_ref[...], w_ref[POOL_WIDTH:, :], preferred_element_type=F32))
    x = x_ref[...] + gate_ref[...] * mixed
    xo_ref[...] = x
    y = _rms(x, NORM_EPS) * g_ref[...]
    h_ref[...] = (y * (1.0 + sc_ref[...]) + sh_ref[...]).astype(BF16)


def _out_proj(st, layer, x, pool_out, att, mod, w_out, norm_g):
    m = st.rows
    tm = TM
    row = lambda i, j: (i, 0)
    return pl.pallas_call(
        _out_proj_kernel,
        grid=(m // tm, 1),
        in_specs=[
            pl.BlockSpec((tm, D_MODEL), row),
            pl.BlockSpec((tm, POOL_WIDTH), row),
            pl.BlockSpec((tm, ATTN_WIDTH), row),
            pl.BlockSpec((None, D_MODEL, D_MODEL), lambda i, j: (layer, 0, 0),
                         pipeline_mode=pl.Buffered(1)),
            _mod_spec(st, layer, MOD_GATE_A, tm),
            pl.BlockSpec((None, 1, D_MODEL), lambda i, j: (layer, 0, 0)),
            _mod_spec(st, layer, MOD_SCALE_F, tm),
            _mod_spec(st, layer, MOD_SHIFT_F, tm),
        ],
        out_specs=[pl.BlockSpec((tm, D_MODEL), row), pl.BlockSpec((tm, D_MODEL), row)],
        out_shape=[jax.ShapeDtypeStruct((m, D_MODEL), F32), jax.ShapeDtypeStruct((m, D_MODEL), BF16)],
        compiler_params=_params("arbitrary", "arbitrary"),
        name="out_proj",
    )(x, pool_out, att, w_out, mod, norm_g, mod, mod)


def _gate_up_kernel(h_ref, wg_ref, wu_ref, o_ref):
    h = h_ref[...]
    g = jnp.dot(h, wg_ref[...], preferred_element_type=F32)
    u = jnp.dot(h, wu_ref[...], preferred_element_type=F32)
    o_ref[...] = (g * jax.nn.sigmoid(g) * u).astype(BF16)


def _gate_up(st, layer, h, w_gate, w_up):
    m = st.rows
    tm = min(TM_FFN, m)
    w_spec = pl.BlockSpec((None, D_MODEL, TF), lambda i, j: (layer, 0, j))
    return pl.pallas_call(
        _gate_up_kernel,
        grid=(m // tm, D_FF // TF),
        in_specs=[pl.BlockSpec((tm, D_MODEL), lambda i, j: (i, 0)), w_spec, w_spec],
        out_specs=pl.BlockSpec((tm, TF), lambda i, j: (i, j)),
        out_shape=jax.ShapeDtypeStruct((m, D_FF), BF16),
        compiler_params=_params("arbitrary", "arbitrary"),
        name="ffn_gate_up",
    )(h, w_gate, w_up)


def _down_kernel(a_ref, w_ref, x_ref, gate_ref, o_ref):
    y = jnp.dot(a_ref[...], w_ref[...], preferred_element_type=F32)
    o_ref[...] = x_ref[...] + gate_ref[...] * y


def _down(st, layer, act, x, mod, w_down):
    m = st.rows
    tm = min(TM_FFN, m)
    return pl.pallas_call(
        _down_kernel,
        grid=(m // tm, D_MODEL // TN_DOWN),
        in_specs=[
            pl.BlockSpec((tm, D_FF), lambda i, j: (i, 0)),
            pl.BlockSpec((None, D_FF, TN_DOWN), lambda i, j: (layer, 0, j)),
            pl.BlockSpec((tm, TN_DOWN), lambda i, j: (i, j)),
            _mod_spec(st, layer, MOD_GATE_F, tm, TN_DOWN),
        ],
        out_specs=pl.BlockSpec((tm, TN_DOWN), lambda i, j: (i, j)),
        out_shape=jax.ShapeDtypeStruct((m, D_MODEL), F32),
        compiler_params=_params("arbitrary", "arbitrary"),
        name="ffn_down",
    )(act, w_down, x, mod)


def _final_norm_kernel(x_ref, g_ref, o_ref):
    o_ref[...] = _rms(x_ref[...], NORM_EPS) * g_ref[...]


def _final_norm(x, g):
    m = x.shape[0]
    return pl.pallas_call(
        _final_norm_kernel,
        grid=(m // TM,),
        in_specs=[pl.BlockSpec((TM, D_MODEL), lambda i: (i, 0)), pl.BlockSpec((1, D_MODEL), lambda i: (0, 0))],
        out_specs=pl.BlockSpec((TM, D_MODEL), lambda i: (i, 0)),
        out_shape=jax.ShapeDtypeStruct((m, D_MODEL), F32),
        compiler_params=_params("arbitrary"),
        name="final_norm",
    )(x, g)


def _rope_tables(st):
    half = HEAD_DIM // 2
    inv = 1.0 / (ROPE_THETA ** (jnp.arange(half, dtype=F32) * (2.0 / HEAD_DIM)))
    pos = st.pos0 + jnp.arange(st.seq)
    ang = pos.astype(F32)[:, None] * inv[None, :]
    cos, sin = jnp.cos(ang), jnp.sin(ang)
    reps = V7X_LANES // HEAD_DIM
    cos_t = jnp.tile(jnp.concatenate([cos, cos], axis=-1), (1, reps))
    sin_t = jnp.tile(jnp.concatenate([-sin, sin], axis=-1), (1, reps))
    if st.seq < TM:
        cos_t = jnp.tile(cos_t, (TM // st.seq, 1))
        sin_t = jnp.tile(sin_t, (TM // st.seq, 1))
    return cos_t, sin_t


def _run_stream(st, x, mod, hist, caches, p):
    m = st.rows
    x = x.reshape(m, D_MODEL)
    cos_tab, sin_tab = _rope_tables(st)
    kv_stacks, tails = None, []
    for l in range(DEPTH):
        lam_init = 0.8 - 0.6 * math.exp(-0.3 * l)
        u, q, k_stack, kb, v_stack, vb = _in_proj(st, l, x, mod, p["norm_mix_g"], p["w_in"],
                                                  cos_tab, sin_tab, kv_stacks)
        kv_stacks = (k_stack, v_stack)
        pool_out, tail = _pool(st, l, u, hist[l], p["w_pool"], p["pool_scale"])
        tails.append(tail)
        if caches is None:
            att = _prompt_attn(st, l, q, kb, vb, p["lam"], p["subln_g"], lam_init)
        else:
            att = _decode_attn(st, l, q, kb, vb, caches[0], caches[1], p["lam"], p["subln_g"], lam_init)
        x, h = _out_proj(st, l, x, pool_out, att, mod, p["w_out"], p["norm_ffn_g"])
        act = _gate_up(st, l, h, p["w_gate"], p["w_up"])
        x = _down(st, l, act, x, mod, p["w_down"])
    y = _final_norm(x, p["final_g"]).reshape(st.batch, st.seq, D_MODEL)
    k_stack, v_stack = kv_stacks
    if st.k_transposed:
        k_out = k_stack.reshape(DEPTH, st.batch, 2 * N_HEADS, HEAD_DIM, st.seq)
        k_out = jnp.transpose(k_out, (0, 1, 4, 2, 3))
    else:
        k_out = k_stack.reshape(DEPTH, st.batch, st.seq, 2 * N_HEADS, HEAD_DIM)
    v_out = v_stack.reshape(DEPTH, st.batch, st.seq, N_HEADS, V_DIM)
    return y, k_out, v_out, jnp.stack(tails)[:, :, HALO - POOL_HIST:]


def kernel(x_prompt, x_sample, c_prompt, c_sample, cache_k, cache_v, state_pool, w_mod, b_mod,
           norm_mix_g, w_in, w_pool, pool_scale, lam_q1, lam_k1, lam_q2, lam_k2, subln_g, w_out,
           norm_ffn_g, w_gate, w_up, w_down, final_g):
    p = {
        "norm_mix_g": norm_mix_g.reshape(DEPTH, 1, D_MODEL),
        "norm_ffn_g": norm_ffn_g.reshape(DEPTH, 1, D_MODEL),
        "final_g": final_g.reshape(1, D_MODEL),
        "w_in": w_in.astype(BF16),
        "w_pool": w_pool.astype(BF16),
        "pool_scale": pool_scale.reshape(DEPTH, 1, POOL_WIDTH),
        "lam": jnp.stack([lam_q1, lam_k1, lam_q2, lam_k2], axis=1),
        "subln_g": subln_g.reshape(DEPTH, 1, V_DIM),
        "w_out": w_out.astype(BF16),
        "w_gate": w_gate.astype(BF16),
        "w_up": w_up.astype(BF16),
        "w_down": w_down.astype(BF16),
    }
    c_all = jnp.concatenate([c_prompt, c_sample], axis=0)
    c_all = jnp.pad(c_all, ((0, -c_all.shape[0] % 8), (0, 0)))
    mod = _modulation(c_all, w_mod, b_mod)[:, :BATCH + DEC_BATCH]
    mod_prompt = mod[:, :BATCH].reshape(DEPTH * BATCH, 1, N_MOD * D_MODEL)
    mod_sample = jnp.repeat(mod[:, BATCH:], DEC_SEQ, axis=1)

    hist_prompt = jnp.zeros((DEPTH, BATCH, HALO, POOL_WIDTH), F32)
    hist_sample = jnp.pad(state_pool, ((0, 0), (0, 0), (HALO - POOL_HIST, 0), (0, 0)))
    caches = (jnp.transpose(cache_k, (0, 1, 3, 4, 2)).reshape(DEPTH * DEC_BATCH, ATTN_WIDTH, PAST_LEN),
              cache_v.reshape(-1, V_DIM))

    y_p, k_p, v_p, pool_p = _run_stream(PROMPT, x_prompt, mod_prompt, hist_prompt, None, p)
    y_s, k_s, v_s, pool_s = _run_stream(SAMPLE, x_sample, mod_sample, hist_sample, caches, p)
    return (y_p, y_s, k_p, v_p, pool_p, k_s, v_s, pool_s)
```

```python
import functools
import math
from typing import NamedTuple

import jax
import jax.numpy as jnp
from jax import lax
from jax.experimental import pallas as pl
from jax.experimental.pallas import tpu as pltpu

F32 = jnp.float32
BF16 = jnp.bfloat16

D_MODEL = 2048
BATCH = 4
SEQ = 4096
DEPTH = 4
DEC_BATCH = 32
DEC_SEQ = 16
PAST_LEN = 2048
CHUNK = 64
POOL_WIDTH = D_MODEL // 2
N_POOL_GROUPS = 4
POOL_GROUP = POOL_WIDTH // N_POOL_GROUPS
POOL_WINDOWS = (2, 4, 8, 16)
POOL_HIST = max(POOL_WINDOWS) - 1
HALO = POOL_HIST + 1
ATTN_WIDTH = D_MODEL - POOL_WIDTH
HEAD_DIM = 64
N_HEADS = ATTN_WIDTH // (2 * HEAD_DIM)
V_DIM = 2 * HEAD_DIM
IN_COLS = POOL_WIDTH + 3 * ATTN_WIDTH
IN_SECTIONS = IN_COLS // ATTN_WIDTH
D_FF = -(-8 * D_MODEL // (3 * 256)) * 256
ROPE_THETA = 10000.0
NORM_EPS = 1e-6
SUBLN_EPS = 1e-5
N_MOD = 6
MOD_SHIFT_A, MOD_SCALE_A, MOD_GATE_A, MOD_SHIFT_F, MOD_SCALE_F, MOD_GATE_F = range(N_MOD)

V7X_VMEM_BYTES = 64 * 1024 * 1024
V7X_LANES = 128
VMEM_LIMIT = V7X_VMEM_BYTES * 7 // 8

TM = 512
TM_FFN = 1024
TN_MOD = 1024
TF = 512
TF_SINGLE = D_FF // 4
TN_DOWN = 512
TN_DOWN_SINGLE = 1024
TM_POOL = 1024
TQ = 1024
TK = 512
DIAG_TILES = TQ // TK
SCORE_SCALE_LOG2 = HEAD_DIM ** -0.5 * math.log2(math.e)
TK_DEC = 2048


class Stream(NamedTuple):
    batch: int
    seq: int
    pos0: int
    mod_rows: int
    k_transposed: bool

    @property
    def rows(self):
        return self.batch * self.seq


PROMPT = Stream(BATCH, SEQ, 0, 1, True)
SAMPLE = Stream(DEC_BATCH, DEC_SEQ, PAST_LEN, DEC_BATCH * DEC_SEQ, False)


def _params(*semantics, flags=None):
    return pltpu.CompilerParams(dimension_semantics=semantics, vmem_limit_bytes=VMEM_LIMIT, flags=flags)


def _rms(x, eps, axis=-1):
    return x * lax.rsqrt(jnp.mean(x * x, axis=axis, keepdims=True) + eps)


def _mod_spec(st, layer, which, tm, tn=D_MODEL):
    per_chunk = D_MODEL // tn
    if st.mod_rows == 1:
        tiles_per_batch = st.seq // tm
        return pl.BlockSpec(
            (None, 1, tn),
            lambda i, j: (layer * st.batch + i // tiles_per_batch, 0, which * per_chunk + j % per_chunk))
    return pl.BlockSpec((None, tm, tn), lambda i, j: (layer, i, which * per_chunk + j % per_chunk))


def _mod_kernel(c_ref, w_ref, b_ref, o_ref):
    c = c_ref[...]
    cs = (c * jax.nn.sigmoid(c)).astype(BF16)
    o_ref[...] = jnp.dot(cs, w_ref[...].astype(BF16), preferred_element_type=F32) + b_ref[...]


def _modulation(c_all, w_mod, b_mod):
    nb = c_all.shape[0]
    n = N_MOD * D_MODEL
    return pl.pallas_call(
        _mod_kernel,
        grid=(DEPTH, n // TN_MOD),
        in_specs=[
            pl.BlockSpec((nb, D_MODEL), lambda l, j: (0, 0)),
            pl.BlockSpec((None, D_MODEL, TN_MOD), lambda l, j: (l, 0, j)),
            pl.BlockSpec((None, 1, TN_MOD), lambda l, j: (l, 0, j)),
        ],
        out_specs=pl.BlockSpec((None, nb, TN_MOD), lambda l, j: (l, 0, j)),
        out_shape=jax.ShapeDtypeStruct((DEPTH, nb, n), F32),
        compiler_params=_params("arbitrary", "arbitrary"),
        name="modulation",
    )(c_all, w_mod, b_mod.reshape(DEPTH, 1, n))


def _rope_store(acc, cos_ref, sin_ref, scale, out_refs, transposed_ref=None):
    tm = acc.shape[0]
    lane = lax.broadcasted_iota(jnp.int32, (tm, V7X_LANES), 1)
    first_half = (lane % HEAD_DIM) < (HEAD_DIM // 2)
    cos = cos_ref[...]
    sin = sin_ref[...]
    for c in range(ATTN_WIDTH // V7X_LANES):
        cols = slice(c * V7X_LANES, (c + 1) * V7X_LANES)
        xs = acc[:, cols]
        partner = jnp.where(first_half,
                            pltpu.roll(xs, V7X_LANES - HEAD_DIM // 2, 1),
                            pltpu.roll(xs, HEAD_DIM // 2, 1))
        y = xs * cos + partner * sin
        if scale != 1.0:
            y = y * scale
        for r in out_refs:
            r[:, cols] = y.astype(r.dtype)
        if transposed_ref is not None:
            transposed_ref[cols, :] = y.T


def _in_proj_kernel(x_ref, g_ref, sc_ref, sh_ref, w_ref, cos_ref, sin_ref, *rest,
                    k_transposed, tiles):
    u_ref, q_ref, k_ref, kb_ref, v_ref, vb_ref, h_cur, h_next, acc_a, acc_b = rest[-10:]
    s = pl.program_id(0)
    section = s % IN_SECTIONS
    last = IN_SECTIONS * tiles

    def normalise(dst):
        y = _rms(x_ref[...], NORM_EPS) * g_ref[...]
        dst[...] = (y * (1.0 + sc_ref[...]) + sh_ref[...]).astype(BF16)

    def multiply(dst):
        dst[...] = jnp.dot(h_cur[...], w_ref[...], preferred_element_type=F32)

    def finish_v(acc):
        for h in range(N_HEADS):
            v_ref[pl.ds(h, acc.shape[0], stride=N_HEADS), :] = acc[:, h * V_DIM:(h + 1) * V_DIM]
        vb_ref[...] = acc[...].astype(BF16)

    @pl.when(s == 0)
    def _():
        normalise(h_cur)
        multiply(acc_a)

    @pl.when((section == 0) & (s > 0) & (s < last))
    def _():
        h_cur[...] = h_next[...]
        multiply(acc_a)
        finish_v(acc_b)

    @pl.when(section == 1)
    def _():
        multiply(acc_b)
        u_ref[...] = acc_a[...]
        normalise(h_next)

    @pl.when(section == 2)
    def _():
        multiply(acc_a)
        _rope_store(acc_b, cos_ref, sin_ref, SCORE_SCALE_LOG2, (q_ref,))

    @pl.when(section == 3)
    def _():
        multiply(acc_b)
        if k_transposed:
            _rope_store(acc_a, cos_ref, sin_ref, 1.0, (kb_ref,), k_ref)
        else:
            _rope_store(acc_a, cos_ref, sin_ref, 1.0, (k_ref, kb_ref))

    @pl.when(s == last)
    def _():
        finish_v(acc_b)


def _in_proj(st, layer, x, mod, norm_g, w_in, cos_tab, sin_tab, kv_stacks):
    m = st.rows
    tm = TM
    tiles = m // tm
    tab_tiles = cos_tab.shape[0] // tm
    steps = IN_SECTIONS * tiles + 1
    done = lambda s: jnp.maximum(s - 1, 0) // IN_SECTIONS
    normed = lambda s: jnp.minimum((s + IN_SECTIONS - 1) // IN_SECTIONS, tiles - 1)
    out_spec = pl.BlockSpec((tm, ATTN_WIDTH), lambda s: (done(s), 0))
    stack_spec = pl.BlockSpec((tm, ATTN_WIDTH), lambda s: (layer * tiles + done(s), 0))
    sds = lambda dt: jax.ShapeDtypeStruct((m, ATTN_WIDTH), dt)
    stack_sds = jax.ShapeDtypeStruct((DEPTH * m, ATTN_WIDTH), F32)
    v_stack_spec = pl.BlockSpec((tm * N_HEADS, V_DIM), lambda s: (layer * tiles + done(s), 0))
    v_stack_sds = jax.ShapeDtypeStruct((DEPTH * m * N_HEADS, V_DIM), F32)
    out_specs = [out_spec, out_spec, stack_spec, out_spec, v_stack_spec, out_spec]
    out_shape = [sds(F32), sds(BF16), stack_sds, sds(BF16), v_stack_sds, sds(BF16)]
    if st.k_transposed:
        tiles_per_batch = st.seq // tm
        out_specs[2] = pl.BlockSpec(
            (None, ATTN_WIDTH, tm),
            lambda s: (layer * st.batch + done(s) // tiles_per_batch, 0, done(s) % tiles_per_batch))
        out_shape[2] = jax.ShapeDtypeStruct((DEPTH * st.batch, ATTN_WIDTH, st.seq), F32)
    if st.mod_rows == 1:
        tiles_per_batch = st.seq // tm
        mod_spec = lambda which: pl.BlockSpec(
            (None, 1, D_MODEL), lambda s: (layer * st.batch + normed(s) // tiles_per_batch, 0, which))
    else:
        mod_spec = lambda which: pl.BlockSpec((None, tm, D_MODEL), lambda s: (layer, normed(s), which),
                                              pipeline_mode=pl.Buffered(1))
    in_specs = [
        pl.BlockSpec((tm, D_MODEL), lambda s: (normed(s), 0)),
        pl.BlockSpec((None, 1, D_MODEL), lambda s: (layer, 0, 0)),
        mod_spec(MOD_SCALE_A),
        mod_spec(MOD_SHIFT_A),
        pl.BlockSpec((None, D_MODEL, ATTN_WIDTH),
                     lambda s: (layer, 0, jnp.minimum(s, steps - 2) % IN_SECTIONS)),
        pl.BlockSpec((tm, V7X_LANES), lambda s: (done(s) % tab_tiles, 0)),
        pl.BlockSpec((tm, V7X_LANES), lambda s: (done(s) % tab_tiles, 0)),
    ]
    args = [x, norm_g, mod, mod, w_in, cos_tab, sin_tab]
    aliases = {}
    if kv_stacks is not None:
        aliases = {len(args): 2, len(args) + 1: 4}
        in_specs += [pl.BlockSpec(memory_space=pl.ANY)] * 2
        args += list(kv_stacks)
    return pl.pallas_call(
        functools.partial(_in_proj_kernel, k_transposed=st.k_transposed, tiles=tiles),
        grid=(steps,),
        in_specs=in_specs,
        out_specs=out_specs,
        out_shape=out_shape,
        input_output_aliases=aliases,
        scratch_shapes=[pltpu.VMEM((tm, D_MODEL), BF16), pltpu.VMEM((tm, D_MODEL), BF16),
                        pltpu.VMEM((tm, ATTN_WIDTH), F32), pltpu.VMEM((tm, ATTN_WIDTH), F32)],
        compiler_params=_params("arbitrary"),
        name="in_proj",
    )(*args)


def _pool_kernel(u_ref, hist_ref, wp_ref, ps_ref, o_ref, tail_ref, ext_ref, *, tm, pos0):
    i = pl.program_id(1)

    @pl.when(i == 0)
    def _():
        ext_ref[0:HALO, :] = hist_ref[...]

    @pl.when(i > 0)
    def _():
        ext_ref[0:HALO, :] = ext_ref[tm:tm + HALO, :]

    ext_ref[HALO:, :] = u_ref[...]

    @pl.when(i == pl.num_programs(1) - 1)
    def _():
        tail_ref[...] = ext_ref[tm:tm + HALO, :]

    pos = pos0 + i * tm + lax.broadcasted_iota(jnp.int32, (tm, 1), 0)
    for g, w in enumerate(POOL_WINDOWS):
        cols = slice(g * POOL_GROUP, (g + 1) * POOL_GROUP)
        cur = u_ref[:, cols]
        s = ext_ref[:, cols]
        shift = 1
        while shift < w:
            s = s + pltpu.roll(s, shift, 0)
            shift *= 2
        s = s[HALO:]
        count = jnp.minimum(w, pos + 1).astype(F32)
        d = (s / count - cur).astype(BF16)
        y = jnp.dot(d, wp_ref[g], preferred_element_type=F32) * ps_ref[:, cols]
        o_ref[:, cols] = y.astype(BF16)


def _pool(st, layer, u, hist, w_pool, pool_scale):
    tm = min(TM_POOL, st.seq)
    tiles = st.seq // tm
    return pl.pallas_call(
        functools.partial(_pool_kernel, tm=tm, pos0=st.pos0),
        grid=(st.batch, tiles),
        in_specs=[
            pl.BlockSpec((tm, POOL_WIDTH), lambda b, i: (b * tiles + i, 0)),
            pl.BlockSpec((None, HALO, POOL_WIDTH), lambda b, i: (b, 0, 0)),
            pl.BlockSpec((None, N_POOL_GROUPS, POOL_GROUP, POOL_GROUP), lambda b, i: (layer, 0, 0, 0)),
            pl.BlockSpec((None, 1, POOL_WIDTH), lambda b, i: (layer, 0, 0)),
        ],
        out_specs=[pl.BlockSpec((tm, POOL_WIDTH), lambda b, i: (b * tiles + i, 0)),
                   pl.BlockSpec((None, HALO, POOL_WIDTH), lambda b, i: (b, 0, 0))],
        out_shape=[jax.ShapeDtypeStruct((st.rows, POOL_WIDTH), BF16),
                   jax.ShapeDtypeStruct((st.batch, HALO, POOL_WIDTH), F32)],
        scratch_shapes=[pltpu.VMEM((tm + HALO, POOL_WIDTH), F32)],
        compiler_params=_params("arbitrary", "arbitrary"),
        name="pool_mix",
    )(u, hist, w_pool, pool_scale)


def _lambda(lam_ref, lam_init):
    lam = lam_ref[...]
    s1 = jnp.sum(lam[0:1] * lam[1:2], axis=-1, keepdims=True)
    s2 = jnp.sum(lam[2:3] * lam[3:4], axis=-1, keepdims=True)
    return jnp.exp(s1) - jnp.exp(s2) + lam_init


def _online_softmax_step(s, v, m_ref, l_ref, acc_ref):
    m_prev = m_ref[...]
    m_new = jnp.maximum(m_prev, jnp.max(s, axis=-1, keepdims=True))
    alpha = jnp.exp2(m_prev - m_new)
    p = jnp.exp2(s - m_new)
    l_ref[...] = alpha * l_ref[...] + jnp.sum(p, axis=-1, keepdims=True)
    acc_ref[...] = alpha * acc_ref[...] + jnp.dot(p.astype(BF16), v, preferred_element_type=F32)
    m_ref[...] = m_new


def _sub_ln(o1, o2, lam, g, lam_init, axis=-1):
    a = o1 - lam * o2
    return (_rms(a, SUBLN_EPS, axis) * g) * (1.0 - lam_init)


_NT = (((1,), (1,)), ((), ()))


def _prompt_attn_kernel(q_ref, k_ref, v_ref, lam_ref, g_ref, o_ref,
                        qz_ref, vt_ref, sa_ref, sb_ref, m_ref, l_ref, acc_ref, *, lam_init):
    i = pl.program_id(2)

    @pl.when(i == 0)
    def _():
        for t in range(vt_ref.shape[0]):
            vt_ref[t] = v_ref[t * TK:(t + 1) * TK, :].astype(F32).T.astype(BF16)

    q = q_ref[...]
    lane = lax.broadcasted_iota(jnp.int32, (TQ, V_DIM), 1)
    zero = jnp.zeros_like(q)
    qz_ref[0:TQ, :] = jnp.where(lane < HEAD_DIM, q, zero)
    qz_ref[TQ:, :] = jnp.where(lane >= HEAD_DIM, q, zero)
    m_ref[...] = jnp.full_like(m_ref, -jnp.inf)
    l_ref[...] = jnp.zeros_like(l_ref)
    acc_ref[...] = jnp.zeros_like(acc_ref)

    def scores(t, s_ref):
        start = pl.multiple_of(t * TK, TK)
        s_ref[...] = lax.dot_general(k_ref[pl.ds(start, TK), :], qz_ref[...], _NT,
                                     preferred_element_type=F32)

    def consume(t, s_ref, diag=None):
        q_lo = 0 if diag is None else diag * TK
        width = TQ - q_lo
        col_sets = (slice(q_lo, TQ), slice(TQ + q_lo, 2 * TQ))

        def gather(ref):
            if q_lo == 0:
                return ref[...]
            return jnp.concatenate([ref[:, c] for c in col_sets], axis=1)

        def scatter(ref, val):
            if q_lo == 0:
                ref[...] = val
            else:
                for n, c in enumerate(col_sets):
                    ref[:, c] = val[:, n * width:(n + 1) * width]

        s = gather(s_ref)
        if diag is not None:
            key = q_lo + lax.broadcasted_iota(jnp.int32, (TK, 2 * width), 0)
            col = lax.broadcasted_iota(jnp.int32, (TK, 2 * width), 1)
            query = q_lo + jnp.where(col >= width, col - width, col)
            s = jnp.where((key // CHUNK) <= (query // CHUNK), s, -jnp.inf)
        m_prev = gather(m_ref)
        m_new = jnp.maximum(m_prev, jnp.max(s, axis=0, keepdims=True))
        alpha = jnp.exp2(m_prev - m_new)
        p = jnp.exp2(s - m_new)
        scatter(l_ref, alpha * gather(l_ref) + jnp.sum(p, axis=0, keepdims=True))
        scatter(acc_ref, alpha * gather(acc_ref) + jnp.dot(vt_ref[t], p.astype(BF16),
                                                           preferred_element_type=F32))
        scatter(m_ref, m_new)

    scores(0, sa_ref)

    def pair(p, carry):
        t = 2 * p
        scores(t + 1, sb_ref)
        consume(t, sa_ref)
        scores(t + 2, sa_ref)
        consume(t + 1, sb_ref)
        return carry

    lax.fori_loop(0, (DIAG_TILES // 2) * i, pair, 0)
    first_diag = DIAG_TILES * i
    buffers = (sa_ref, sb_ref)
    for d in range(DIAG_TILES):
        if d + 1 < DIAG_TILES:
            scores(first_diag + d + 1, buffers[(d + 1) % 2])
        consume(first_diag + d, buffers[d % 2], diag=d)

    o = acc_ref[...] / l_ref[...]
    y = _sub_ln(o[:, 0:TQ], o[:, TQ:], _lambda(lam_ref, lam_init), g_ref[...], lam_init, axis=0)
    o_ref[...] = y.T.astype(BF16)


def _prompt_attn(st, layer, q, k, v, lam, subln_g, lam_init):
    assert DIAG_TILES % 2 == 0
    nq = st.seq // TQ
    return pl.pallas_call(
        functools.partial(_prompt_attn_kernel, lam_init=lam_init),
        grid=(st.batch, N_HEADS, nq),
        in_specs=[
            pl.BlockSpec((TQ, V_DIM), lambda b, h, i: (b * nq + i, h)),
            pl.BlockSpec((st.seq, V_DIM), lambda b, h, i: (b, h)),
            pl.BlockSpec((st.seq, V_DIM), lambda b, h, i: (b, h)),
            pl.BlockSpec((None, 4, HEAD_DIM), lambda b, h, i: (layer, 0, 0)),
            pl.BlockSpec((None, V_DIM, 1), lambda b, h, i: (layer, 0, 0)),
        ],
        out_specs=pl.BlockSpec((TQ, V_DIM), lambda b, h, i: (b * nq + i, h)),
        out_shape=jax.ShapeDtypeStruct((st.rows, ATTN_WIDTH), BF16),
        scratch_shapes=[
            pltpu.VMEM((2 * TQ, V_DIM), BF16),
            pltpu.VMEM((st.seq // TK, V_DIM, TK), BF16),
            pltpu.VMEM((TK, 2 * TQ), F32),
            pltpu.VMEM((TK, 2 * TQ), F32),
            pltpu.VMEM((1, 2 * TQ), F32),
            pltpu.VMEM((1, 2 * TQ), F32),
            pltpu.VMEM((V_DIM, 2 * TQ), F32),
        ],
        compiler_params=_params("arbitrary", "arbitrary", "arbitrary"),
        name="prompt_attn",
    )(q, k, v, lam, subln_g.reshape(DEPTH, V_DIM, 1))


def _decode_attn_kernel(q_ref, kc_ref, vc_ref, kn_ref, vn_ref, lam_ref, g_ref, o_ref,
                        m_ref, l_ref, acc_ref, *, lam_init):
    t = pl.program_id(1)
    n_score_heads = 2 * N_HEADS
    pair_rows = 2 * DEC_SEQ

    @pl.when(t == 0)
    def _():
        m_ref[...] = jnp.full_like(m_ref, -jnp.inf)
        l_ref[...] = jnp.zeros_like(l_ref)
        acc_ref[...] = jnp.zeros_like(acc_ref)

    q = q_ref[...]
    q_heads = [q[:, h * HEAD_DIM:(h + 1) * HEAD_DIM] for h in range(n_score_heads)]

    def update(score_of_head, value_of_head):
        s = jnp.concatenate([score_of_head(h) for h in range(n_score_heads)], axis=0)
        m_prev = m_ref[...]
        m_new = jnp.maximum(m_prev, jnp.max(s, axis=-1, keepdims=True))
        alpha = jnp.exp2(m_prev - m_new)
        p = jnp.exp2(s - m_new)
        l_ref[...] = alpha * l_ref[...] + jnp.sum(p, axis=-1, keepdims=True)
        pb = p.astype(BF16)
        pv = jnp.concatenate(
            [jnp.dot(pb[j * pair_rows:(j + 1) * pair_rows], value_of_head(j), preferred_element_type=F32)
             for j in range(N_HEADS)], axis=0)
        acc_ref[...] = alpha * acc_ref[...] + pv
        m_ref[...] = m_new

    update(lambda h: jnp.dot(q_heads[h], kc_ref[h * HEAD_DIM:(h + 1) * HEAD_DIM, :].astype(BF16),
                             preferred_element_type=F32),
           lambda j: vc_ref[pl.ds(j, TK_DEC, stride=N_HEADS), :].astype(BF16))

    @pl.when(t == pl.num_programs(1) - 1)
    def _():
        update(lambda h: lax.dot_general(q_heads[h], kn_ref[:, h * HEAD_DIM:(h + 1) * HEAD_DIM], _NT,
                                         preferred_element_type=F32),
               lambda j: vn_ref[:, j * V_DIM:(j + 1) * V_DIM])
        o = acc_ref[...] / l_ref[...]
        lam = _lambda(lam_ref, lam_init)
        g = g_ref[...]
        for j in range(N_HEADS):
            r0 = j * pair_rows
            y = _sub_ln(o[r0:r0 + DEC_SEQ], o[r0 + DEC_SEQ:r0 + pair_rows], lam, g, lam_init)
            o_ref[:, j * V_DIM:(j + 1) * V_DIM] = y.astype(BF16)


def _decode_attn(st, layer, q, k_new, v_new, cache_k, cache_v, lam, subln_g, lam_init):
    rows_bd = 2 * N_HEADS * DEC_SEQ
    tok = pl.BlockSpec((DEC_SEQ, ATTN_WIDTH), lambda b, t: (b, 0))
    tiles = PAST_LEN // TK_DEC
    cache_k_spec = pl.BlockSpec((None, ATTN_WIDTH, TK_DEC), lambda b, t: (layer * st.batch + b, 0, t))
    cache_v_spec = pl.BlockSpec((TK_DEC * N_HEADS, V_DIM),
                                lambda b, t: ((layer * st.batch + b) * tiles + t, 0))
    return pl.pallas_call(
        functools.partial(_decode_attn_kernel, lam_init=lam_init),
        grid=(st.batch, tiles),
        in_specs=[
            tok, cache_k_spec, cache_v_spec, tok, tok,
            pl.BlockSpec((None, 4, HEAD_DIM), lambda b, t: (layer, 0, 0)),
            pl.BlockSpec((None, 1, V_DIM), lambda b, t: (layer, 0, 0)),
        ],
        out_specs=tok,
        out_shape=jax.ShapeDtypeStruct((st.rows, ATTN_WIDTH), BF16),
        scratch_shapes=[
            pltpu.VMEM((rows_bd, 1), F32),
            pltpu.VMEM((rows_bd, 1), F32),
            pltpu.VMEM((rows_bd, V_DIM), F32),
        ],
        compiler_params=_params("arbitrary", "arbitrary"),
        name="decode_attn",
    )(q, cache_k, cache_v, k_new, v_new, lam, subln_g)


def _out_proj_kernel(x_ref, pool_ref, att_ref, w_ref, gate_ref, g_ref, sc_ref, sh_ref,
                     xo_ref, h_ref):
    mixed = (jnp.dot(pool_ref[...], w_ref[0:POOL_WIDTH, :], preferred_element_type=F32)
             + jnp.dot(att_ref[...], w_ref[POOL_WIDTH:, :], preferred_element_type=F32))
    x = x_ref[...] + gate_ref[...] * mixed
    xo_ref[...] = x
    y = _rms(x, NORM_EPS) * g_ref[...]
    h_ref[...] = (y * (1.0 + sc_ref[...]) + sh_ref[...]).astype(BF16)


def _out_proj(st, layer, x, pool_out, att, mod, w_out, norm_g):
    m = st.rows
    tm = TM
    row = lambda i, j: (i, 0)
    return pl.pallas_call(
        _out_proj_kernel,
        grid=(m // tm, 1),
        in_specs=[
            pl.BlockSpec((tm, D_MODEL), row),
            pl.BlockSpec((tm, POOL_WIDTH), row),
            pl.BlockSpec((tm, ATTN_WIDTH), row),
            pl.BlockSpec((None, D_MODEL, D_MODEL), lambda i, j: (layer, 0, 0),
                         pipeline_mode=pl.Buffered(1)),
            _mod_spec(st, layer, MOD_GATE_A, tm),
            pl.BlockSpec((None, 1, D_MODEL), lambda i, j: (layer, 0, 0)),
            _mod_spec(st, layer, MOD_SCALE_F, tm),
            _mod_spec(st, layer, MOD_SHIFT_F, tm),
        ],
        out_specs=[pl.BlockSpec((tm, D_MODEL), row), pl.BlockSpec((tm, D_MODEL), row)],
        out_shape=[jax.ShapeDtypeStruct((m, D_MODEL), F32), jax.ShapeDtypeStruct((m, D_MODEL), BF16)],
        compiler_params=_params("arbitrary", "arbitrary"),
        name="out_proj",
    )(x, pool_out, att, w_out, mod, norm_g, mod, mod)


def _gate_up_kernel(h_ref, wg_ref, wu_ref, o_ref):
    h = h_ref[...]
    g = jnp.dot(h, wg_ref[...], preferred_element_type=F32)
    u = jnp.dot(h, wu_ref[...], preferred_element_type=F32)
    o_ref[...] = (g * jax.nn.sigmoid(g) * u).astype(BF16)


def _gate_up(st, layer, h, w_gate, w_up):
    m = st.rows
    tm = min(TM_FFN, m)
    tf = TF if m > tm else TF_SINGLE
    w_spec = pl.BlockSpec((None, D_MODEL, tf), lambda i, j: (layer, 0, j))
    return pl.pallas_call(
        _gate_up_kernel,
        grid=(m // tm, D_FF // tf),
        in_specs=[pl.BlockSpec((tm, D_MODEL), lambda i, j: (i, 0)), w_spec, w_spec],
        out_specs=pl.BlockSpec((tm, tf), lambda i, j: (i, j)),
        out_shape=jax.ShapeDtypeStruct((m, D_FF), BF16),
        compiler_params=_params("arbitrary", "arbitrary"),
        name="ffn_gate_up",
    )(h, w_gate, w_up)


def _down_kernel(a_ref, w_ref, x_ref, gate_ref, o_ref):
    y = jnp.dot(a_ref[...], w_ref[...], preferred_element_type=F32)
    o_ref[...] = x_ref[...] + gate_ref[...] * y


def _down(st, layer, act, x, mod, w_down):
    m = st.rows
    tm = min(TM_FFN, m)
    tn = TN_DOWN if m > tm else TN_DOWN_SINGLE
    return pl.pallas_call(
        _down_kernel,
        grid=(m // tm, D_MODEL // tn),
        in_specs=[
            pl.BlockSpec((tm, D_FF), lambda i, j: (i, 0)),
            pl.BlockSpec((None, D_FF, tn), lambda i, j: (layer, 0, j)),
            pl.BlockSpec((tm, tn), lambda i, j: (i, j)),
            _mod_spec(st, layer, MOD_GATE_F, tm, tn),
        ],
        out_specs=pl.BlockSpec((tm, tn), lambda i, j: (i, j)),
        out_shape=jax.ShapeDtypeStruct((m, D_MODEL), F32),
        compiler_params=_params("arbitrary", "arbitrary"),
        name="ffn_down",
    )(act, w_down, x, mod)


def _final_norm_kernel(x_ref, g_ref, o_ref):
    o_ref[...] = _rms(x_ref[...], NORM_EPS) * g_ref[...]


def _final_norm(x, g):
    m = x.shape[0]
    return pl.pallas_call(
        _final_norm_kernel,
        grid=(m // TM,),
        in_specs=[pl.BlockSpec((TM, D_MODEL), lambda i: (i, 0)), pl.BlockSpec((1, D_MODEL), lambda i: (0, 0))],
        out_specs=pl.BlockSpec((TM, D_MODEL), lambda i: (i, 0)),
        out_shape=jax.ShapeDtypeStruct((m, D_MODEL), F32),
        compiler_params=_params("arbitrary"),
        name="final_norm",
    )(x, g)


def _rope_tables(st):
    half = HEAD_DIM // 2
    inv = 1.0 / (ROPE_THETA ** (jnp.arange(half, dtype=F32) * (2.0 / HEAD_DIM)))
    pos = st.pos0 + jnp.arange(st.seq)
    ang = pos.astype(F32)[:, None] * inv[None, :]
    cos, sin = jnp.cos(ang), jnp.sin(ang)
    reps = V7X_LANES // HEAD_DIM
    cos_t = jnp.tile(jnp.concatenate([cos, cos], axis=-1), (1, reps))
    sin_t = jnp.tile(jnp.concatenate([-sin, sin], axis=-1), (1, reps))
    if st.seq < TM:
        cos_t = jnp.tile(cos_t, (TM // st.seq, 1))
        sin_t = jnp.tile(sin_t, (TM // st.seq, 1))
    return cos_t, sin_t


def _run_stream(st, x, mod, hist, caches, p):
    m = st.rows
    x = x.reshape(m, D_MODEL)
    cos_tab, sin_tab = _rope_tables(st)
    kv_stacks, tails = None, []
    for l in range(DEPTH):
        lam_init = 0.8 - 0.6 * math.exp(-0.3 * l)
        u, q, k_stack, kb, v_stack, vb = _in_proj(st, l, x, mod, p["norm_mix_g"], p["w_in"],
                                                  cos_tab, sin_tab, kv_stacks)
        kv_stacks = (k_stack, v_stack)
        pool_out, tail = _pool(st, l, u, hist[l], p["w_pool"], p["pool_scale"])
        tails.append(tail)
        if caches is None:
            att = _prompt_attn(st, l, q, kb, vb, p["lam"], p["subln_g"], lam_init)
        else:
            att = _decode_attn(st, l, q, kb, vb, caches[0], caches[1], p["lam"], p["subln_g"], lam_init)
        x, h = _out_proj(st, l, x, pool_out, att, mod, p["w_out"], p["norm_ffn_g"])
        act = _gate_up(st, l, h, p["w_gate"], p["w_up"])
        x = _down(st, l, act, x, mod, p["w_down"])
    y = _final_norm(x, p["final_g"]).reshape(st.batch, st.seq, D_MODEL)
    k_stack, v_stack = kv_stacks
    if st.k_transposed:
        k_out = k_stack.reshape(DEPTH, st.batch, 2 * N_HEADS, HEAD_DIM, st.seq)
        k_out = jnp.transpose(k_out, (0, 1, 4, 2, 3))
    else:
        k_out = k_stack.reshape(DEPTH, st.batch, st.seq, 2 * N_HEADS, HEAD_DIM)
    v_out = v_stack.reshape(DEPTH, st.batch, st.seq, N_HEADS, V_DIM)
    return y, k_out, v_out, jnp.stack(tails)[:, :, HALO - POOL_HIST:]


def kernel(x_prompt, x_sample, c_prompt, c_sample, cache_k, cache_v, state_pool, w_mod, b_mod,
           norm_mix_g, w_in, w_pool, pool_scale, lam_q1, lam_k1, lam_q2, lam_k2, subln_g, w_out,
           norm_ffn_g, w_gate, w_up, w_down, final_g):
    p = {
        "norm_mix_g": norm_mix_g.reshape(DEPTH, 1, D_MODEL),
        "norm_ffn_g": norm_ffn_g.reshape(DEPTH, 1, D_MODEL),
        "final_g": final_g.reshape(1, D_MODEL),
        "w_in": w_in.astype(BF16),
        "w_pool": w_pool.astype(BF16),
        "pool_scale": pool_scale.reshape(DEPTH, 1, POOL_WIDTH),
        "lam": jnp.stack([lam_q1, lam_k1, lam_q2, lam_k2], axis=1),
        "subln_g": subln_g.reshape(DEPTH, 1, V_DIM),
        "w_out": w_out.astype(BF16),
        "w_gate": w_gate.astype(BF16),
        "w_up": w_up.astype(BF16),
        "w_down": w_down.astype(BF16),
    }
    c_all = jnp.concatenate([c_prompt, c_sample], axis=0)
    c_all = jnp.pad(c_all, ((0, -c_all.shape[0] % 8), (0, 0)))
    mod = _modulation(c_all, w_mod, b_mod)[:, :BATCH + DEC_BATCH]
    mod_prompt = mod[:, :BATCH].reshape(DEPTH * BATCH, 1, N_MOD * D_MODEL)
    mod_sample = jnp.repeat(mod[:, BATCH:], DEC_SEQ, axis=1)

    hist_prompt = jnp.zeros((DEPTH, BATCH, HALO, POOL_WIDTH), F32)
    hist_sample = jnp.pad(state_pool, ((0, 0), (0, 0), (HALO - POOL_HIST, 0), (0, 0)))
    caches = (jnp.transpose(cache_k, (0, 1, 3, 4, 2)).reshape(DEPTH * DEC_BATCH, ATTN_WIDTH, PAST_LEN),
              cache_v.reshape(-1, V_DIM))

    y_p, k_p, v_p, pool_p = _run_stream(PROMPT, x_prompt, mod_prompt, hist_prompt, None, p)
    y_s, k_s, v_s, pool_s = _run_stream(SAMPLE, x_sample, mod_sample, hist_sample, caches, p)
    return (y_p, y_s, k_p, v_p, pool_p, k_s, v_s, pool_s)
```

```python
import functools
import math
from typing import NamedTuple

import jax
import jax.numpy as jnp
from jax import lax
from jax.experimental import pallas as pl
from jax.experimental.pallas import tpu as pltpu

F32 = jnp.float32
BF16 = jnp.bfloat16

D_MODEL = 2048
BATCH = 4
SEQ = 4096
DEPTH = 4
DEC_BATCH = 32
DEC_SEQ = 16
PAST_LEN = 2048
CHUNK = 64
POOL_WIDTH = D_MODEL // 2
N_POOL_GROUPS = 4
POOL_GROUP = POOL_WIDTH // N_POOL_GROUPS
POOL_WINDOWS = (2, 4, 8, 16)
POOL_HIST = max(POOL_WINDOWS) - 1
HALO = POOL_HIST + 1
ATTN_WIDTH = D_MODEL - POOL_WIDTH
HEAD_DIM = 64
N_HEADS = ATTN_WIDTH // (2 * HEAD_DIM)
V_DIM = 2 * HEAD_DIM
IN_COLS = POOL_WIDTH + 3 * ATTN_WIDTH
IN_SECTIONS = IN_COLS // ATTN_WIDTH
D_FF = -(-8 * D_MODEL // (3 * 256)) * 256
ROPE_THETA = 10000.0
NORM_EPS = 1e-6
SUBLN_EPS = 1e-5
N_MOD = 6
MOD_SHIFT_A, MOD_SCALE_A, MOD_GATE_A, MOD_SHIFT_F, MOD_SCALE_F, MOD_GATE_F = range(N_MOD)

V7X_VMEM_BYTES = 64 * 1024 * 1024
V7X_LANES = 128
VMEM_LIMIT = V7X_VMEM_BYTES * 7 // 8

TM = 512
TM_FFN = 1024
TN_MOD = 1024
TF = 512
TN_DOWN = 512
TM_POOL = 1024
TQ = 1024
TK = 512
DIAG_TILES = TQ // TK
SCORE_SCALE_LOG2 = HEAD_DIM ** -0.5 * math.log2(math.e)
TK_DEC = 2048


class Stream(NamedTuple):
    batch: int
    seq: int
    pos0: int
    mod_rows: int
    k_transposed: bool

    @property
    def rows(self):
        return self.batch * self.seq


PROMPT = Stream(BATCH, SEQ, 0, 1, True)
SAMPLE = Stream(DEC_BATCH, DEC_SEQ, PAST_LEN, DEC_BATCH * DEC_SEQ, False)


def _params(*semantics, flags=None):
    return pltpu.CompilerParams(dimension_semantics=semantics, vmem_limit_bytes=VMEM_LIMIT, flags=flags)


def _rms(x, eps, axis=-1):
    return x * lax.rsqrt(jnp.mean(x * x, axis=axis, keepdims=True) + eps)


def _mod_spec(st, layer, which, tm, tn=D_MODEL):
    per_chunk = D_MODEL // tn
    if st.mod_rows == 1:
        tiles_per_batch = st.seq // tm
        return pl.BlockSpec(
            (None, 1, tn),
            lambda i, j: (layer * st.batch + i // tiles_per_batch, 0, which * per_chunk + j % per_chunk))
    return pl.BlockSpec((None, tm, tn), lambda i, j: (layer, i, which * per_chunk + j % per_chunk))


def _mod_kernel(c_ref, w_ref, b_ref, o_ref):
    c = c_ref[...]
    cs = (c * jax.nn.sigmoid(c)).astype(BF16)
    o_ref[...] = jnp.dot(cs, w_ref[...].astype(BF16), preferred_element_type=F32) + b_ref[...]


def _modulation(c_all, w_mod, b_mod):
    nb = c_all.shape[0]
    n = N_MOD * D_MODEL
    return pl.pallas_call(
        _mod_kernel,
        grid=(DEPTH, n // TN_MOD),
        in_specs=[
            pl.BlockSpec((nb, D_MODEL), lambda l, j: (0, 0)),
            pl.BlockSpec((None, D_MODEL, TN_MOD), lambda l, j: (l, 0, j)),
            pl.BlockSpec((None, 1, TN_MOD), lambda l, j: (l, 0, j)),
        ],
        out_specs=pl.BlockSpec((None, nb, TN_MOD), lambda l, j: (l, 0, j)),
        out_shape=jax.ShapeDtypeStruct((DEPTH, nb, n), F32),
        compiler_params=_params("arbitrary", "arbitrary"),
        name="modulation",
    )(c_all, w_mod, b_mod.reshape(DEPTH, 1, n))


def _rope_store(acc, cos_ref, sin_ref, scale, out_refs, transposed_ref=None):
    tm = acc.shape[0]
    lane = lax.broadcasted_iota(jnp.int32, (tm, V7X_LANES), 1)
    first_half = (lane % HEAD_DIM) < (HEAD_DIM // 2)
    cos = cos_ref[...]
    sin = sin_ref[...]
    for c in range(ATTN_WIDTH // V7X_LANES):
        cols = slice(c * V7X_LANES, (c + 1) * V7X_LANES)
        xs = acc[:, cols]
        partner = jnp.where(first_half,
                            pltpu.roll(xs, V7X_LANES - HEAD_DIM // 2, 1),
                            pltpu.roll(xs, HEAD_DIM // 2, 1))
        y = xs * cos + partner * sin
        if scale != 1.0:
            y = y * scale
        for r in out_refs:
            r[:, cols] = y.astype(r.dtype)
        if transposed_ref is not None:
            transposed_ref[cols, :] = y.T


def _in_proj_kernel(x_ref, g_ref, sc_ref, sh_ref, w_ref, cos_ref, sin_ref, *rest,
                    k_transposed, tiles):
    u_ref, q_ref, k_ref, kb_ref, v_ref, vb_ref, h_cur, h_next, acc_a, acc_b = rest[-10:]
    s = pl.program_id(0)
    section = s % IN_SECTIONS
    last = IN_SECTIONS * tiles

    def normalise(dst):
        y = _rms(x_ref[...], NORM_EPS) * g_ref[...]
        dst[...] = (y * (1.0 + sc_ref[...]) + sh_ref[...]).astype(BF16)

    def multiply(dst):
        dst[...] = jnp.dot(h_cur[...], w_ref[...], preferred_element_type=F32)

    def finish_v(acc):
        for h in range(N_HEADS):
            v_ref[pl.ds(h, acc.shape[0], stride=N_HEADS), :] = acc[:, h * V_DIM:(h + 1) * V_DIM]
        vb_ref[...] = acc[...].astype(BF16)

    @pl.when(s == 0)
    def _():
        normalise(h_cur)
        multiply(acc_a)

    @pl.when((section == 0) & (s > 0) & (s < last))
    def _():
        h_cur[...] = h_next[...]
        multiply(acc_a)
        finish_v(acc_b)

    @pl.when(section == 1)
    def _():
        multiply(acc_b)
        u_ref[...] = acc_a[...]
        normalise(h_next)

    @pl.when(section == 2)
    def _():
        multiply(acc_a)
        _rope_store(acc_b, cos_ref, sin_ref, SCORE_SCALE_LOG2, (q_ref,))

    @pl.when(section == 3)
    def _():
        multiply(acc_b)
        if k_transposed:
            _rope_store(acc_a, cos_ref, sin_ref, 1.0, (kb_ref,), k_ref)
        else:
            _rope_store(acc_a, cos_ref, sin_ref, 1.0, (k_ref, kb_ref))

    @pl.when(s == last)
    def _():
        finish_v(acc_b)


def _in_proj(st, layer, x, mod, norm_g, w_in, cos_tab, sin_tab, kv_stacks):
    m = st.rows
    tm = TM
    tiles = m // tm
    tab_tiles = cos_tab.shape[0] // tm
    steps = IN_SECTIONS * tiles + 1
    done = lambda s: jnp.maximum(s - 1, 0) // IN_SECTIONS
    normed = lambda s: jnp.minimum((s + IN_SECTIONS - 1) // IN_SECTIONS, tiles - 1)
    out_spec = pl.BlockSpec((tm, ATTN_WIDTH), lambda s: (done(s), 0))
    stack_spec = pl.BlockSpec((tm, ATTN_WIDTH), lambda s: (layer * tiles + done(s), 0))
    sds = lambda dt: jax.ShapeDtypeStruct((m, ATTN_WIDTH), dt)
    stack_sds = jax.ShapeDtypeStruct((DEPTH * m, ATTN_WIDTH), F32)
    v_stack_spec = pl.BlockSpec((tm * N_HEADS, V_DIM), lambda s: (layer * tiles + done(s), 0))
    v_stack_sds = jax.ShapeDtypeStruct((DEPTH * m * N_HEADS, V_DIM), F32)
    out_specs = [out_spec, out_spec, stack_spec, out_spec, v_stack_spec, out_spec]
    out_shape = [sds(F32), sds(BF16), stack_sds, sds(BF16), v_stack_sds, sds(BF16)]
    if st.k_transposed:
        tiles_per_batch = st.seq // tm
        out_specs[2] = pl.BlockSpec(
            (None, ATTN_WIDTH, tm),
            lambda s: (layer * st.batch + done(s) // tiles_per_batch, 0, done(s) % tiles_per_batch))
        out_shape[2] = jax.ShapeDtypeStruct((DEPTH * st.batch, ATTN_WIDTH, st.seq), F32)
    if st.mod_rows == 1:
        tiles_per_batch = st.seq // tm
        mod_spec = lambda which: pl.BlockSpec(
            (None, 1, D_MODEL), lambda s: (layer * st.batch + normed(s) // tiles_per_batch, 0, which))
    else:
        mod_spec = lambda which: pl.BlockSpec((None, tm, D_MODEL), lambda s: (layer, normed(s), which),
                                              pipeline_mode=pl.Buffered(1))
    in_specs = [
        pl.BlockSpec((tm, D_MODEL), lambda s: (normed(s), 0)),
        pl.BlockSpec((None, 1, D_MODEL), lambda s: (layer, 0, 0)),
        mod_spec(MOD_SCALE_A),
        mod_spec(MOD_SHIFT_A),
        pl.BlockSpec((None, D_MODEL, ATTN_WIDTH),
                     lambda s: (layer, 0, jnp.minimum(s, steps - 2) % IN_SECTIONS)),
        pl.BlockSpec((tm, V7X_LANES), lambda s: (done(s) % tab_tiles, 0)),
        pl.BlockSpec((tm, V7X_LANES), lambda s: (done(s) % tab_tiles, 0)),
    ]
    args = [x, norm_g, mod, mod, w_in, cos_tab, sin_tab]
    aliases = {}
    if kv_stacks is not None:
        aliases = {len(args): 2, len(args) + 1: 4}
        in_specs += [pl.BlockSpec(memory_space=pl.ANY)] * 2
        args += list(kv_stacks)
    return pl.pallas_call(
        functools.partial(_in_proj_kernel, k_transposed=st.k_transposed, tiles=tiles),
        grid=(steps,),
        in_specs=in_specs,
        out_specs=out_specs,
        out_shape=out_shape,
        input_output_aliases=aliases,
        scratch_shapes=[pltpu.VMEM((tm, D_MODEL), BF16), pltpu.VMEM((tm, D_MODEL), BF16),
                        pltpu.VMEM((tm, ATTN_WIDTH), F32), pltpu.VMEM((tm, ATTN_WIDTH), F32)],
        compiler_params=_params("arbitrary"),
        name="in_proj",
    )(*args)


def _pool_kernel(u_ref, hist_ref, wp_ref, ps_ref, o_ref, tail_ref, ext_ref, *, tm, pos0):
    i = pl.program_id(1)

    @pl.when(i == 0)
    def _():
        ext_ref[0:HALO, :] = hist_ref[...]

    @pl.when(i > 0)
    def _():
        ext_ref[0:HALO, :] = ext_ref[tm:tm + HALO, :]

    ext_ref[HALO:, :] = u_ref[...]

    @pl.when(i == pl.num_programs(1) - 1)
    def _():
        tail_ref[...] = ext_ref[tm:tm + HALO, :]

    pos = pos0 + i * tm + lax.broadcasted_iota(jnp.int32, (tm, 1), 0)
    for g, w in enumerate(POOL_WINDOWS):
        cols = slice(g * POOL_GROUP, (g + 1) * POOL_GROUP)
        cur = u_ref[:, cols]
        s = ext_ref[:, cols]
        shift = 1
        while shift < w:
            s = s + pltpu.roll(s, shift, 0)
            shift *= 2
        s = s[HALO:]
        count = jnp.minimum(w, pos + 1).astype(F32)
        d = (s / count - cur).astype(BF16)
        y = jnp.dot(d, wp_ref[g], preferred_element_type=F32) * ps_ref[:, cols]
        o_ref[:, cols] = y.astype(BF16)


def _pool(st, layer, u, hist, w_pool, pool_scale):
    tm = min(TM_POOL, st.seq)
    tiles = st.seq // tm
    return pl.pallas_call(
        functools.partial(_pool_kernel, tm=tm, pos0=st.pos0),
        grid=(st.batch, tiles),
        in_specs=[
            pl.BlockSpec((tm, POOL_WIDTH), lambda b, i: (b * tiles + i, 0)),
            pl.BlockSpec((None, HALO, POOL_WIDTH), lambda b, i: (b, 0, 0)),
            pl.BlockSpec((None, N_POOL_GROUPS, POOL_GROUP, POOL_GROUP), lambda b, i: (layer, 0, 0, 0)),
            pl.BlockSpec((None, 1, POOL_WIDTH), lambda b, i: (layer, 0, 0)),
        ],
        out_specs=[pl.BlockSpec((tm, POOL_WIDTH), lambda b, i: (b * tiles + i, 0)),
                   pl.BlockSpec((None, HALO, POOL_WIDTH), lambda b, i: (b, 0, 0))],
        out_shape=[jax.ShapeDtypeStruct((st.rows, POOL_WIDTH), BF16),
                   jax.ShapeDtypeStruct((st.batch, HALO, POOL_WIDTH), F32)],
        scratch_shapes=[pltpu.VMEM((tm + HALO, POOL_WIDTH), F32)],
        compiler_params=_params("arbitrary", "arbitrary"),
        name="pool_mix",
    )(u, hist, w_pool, pool_scale)


def _lambda(lam_ref, lam_init):
    lam = lam_ref[...]
    s1 = jnp.sum(lam[0:1] * lam[1:2], axis=-1, keepdims=True)
    s2 = jnp.sum(lam[2:3] * lam[3:4], axis=-1, keepdims=True)
    return jnp.exp(s1) - jnp.exp(s2) + lam_init


def _online_softmax_step(s, v, m_ref, l_ref, acc_ref):
    m_prev = m_ref[...]
    m_new = jnp.maximum(m_prev, jnp.max(s, axis=-1, keepdims=True))
    alpha = jnp.exp2(m_prev - m_new)
    p = jnp.exp2(s - m_new)
    l_ref[...] = alpha * l_ref[...] + jnp.sum(p, axis=-1, keepdims=True)
    acc_ref[...] = alpha * acc_ref[...] + jnp.dot(p.astype(BF16), v, preferred_element_type=F32)
    m_ref[...] = m_new


def _sub_ln(o1, o2, lam, g, lam_init, axis=-1):
    a = o1 - lam * o2
    return (_rms(a, SUBLN_EPS, axis) * g) * (1.0 - lam_init)


_NT = (((1,), (1,)), ((), ()))


def _prompt_attn_kernel(q_ref, k_ref, v_ref, lam_ref, g_ref, o_ref,
                        qz_ref, vt_ref, sa_ref, sb_ref, m_ref, l_ref, acc_ref, *, lam_init):
    i = pl.program_id(2)

    @pl.when(i == 0)
    def _():
        for t in range(vt_ref.shape[0]):
            vt_ref[t] = v_ref[t * TK:(t + 1) * TK, :].astype(F32).T.astype(BF16)

    q = q_ref[...]
    lane = lax.broadcasted_iota(jnp.int32, (TQ, V_DIM), 1)
    zero = jnp.zeros_like(q)
    qz_ref[0:TQ, :] = jnp.where(lane < HEAD_DIM, q, zero)
    qz_ref[TQ:, :] = jnp.where(lane >= HEAD_DIM, q, zero)
    m_ref[...] = jnp.full_like(m_ref, -jnp.inf)
    l_ref[...] = jnp.zeros_like(l_ref)
    acc_ref[...] = jnp.zeros_like(acc_ref)

    def scores(t, s_ref):
        start = pl.multiple_of(t * TK, TK)
        s_ref[...] = lax.dot_general(k_ref[pl.ds(start, TK), :], qz_ref[...], _NT,
                                     preferred_element_type=F32)

    def consume(t, s_ref, diag=None):
        q_lo = 0 if diag is None else diag * TK
        width = TQ - q_lo
        col_sets = (slice(q_lo, TQ), slice(TQ + q_lo, 2 * TQ))

        def gather(ref):
            if q_lo == 0:
                return ref[...]
            return jnp.concatenate([ref[:, c] for c in col_sets], axis=1)

        def scatter(ref, val):
            if q_lo == 0:
                ref[...] = val
            else:
                for n, c in enumerate(col_sets):
                    ref[:, c] = val[:, n * width:(n + 1) * width]

        s = gather(s_ref)
        if diag is not None:
            key = q_lo + lax.broadcasted_iota(jnp.int32, (TK, 2 * width), 0)
            col = lax.broadcasted_iota(jnp.int32, (TK, 2 * width), 1)
            query = q_lo + jnp.where(col >= width, col - width, col)
            s = jnp.where((key // CHUNK) <= (query // CHUNK), s, -jnp.inf)
        m_prev = gather(m_ref)
        m_new = jnp.maximum(m_prev, jnp.max(s, axis=0, keepdims=True))
        alpha = jnp.exp2(m_prev - m_new)
        p = jnp.exp2(s - m_new)
        scatter(l_ref, alpha * gather(l_ref) + jnp.sum(p, axis=0, keepdims=True))
        scatter(acc_ref, alpha * gather(acc_ref) + jnp.dot(vt_ref[t], p.astype(BF16),
                                                           preferred_element_type=F32))
        scatter(m_ref, m_new)

    scores(0, sa_ref)

    def pair(p, carry):
        t = 2 * p
        scores(t + 1, sb_ref)
        consume(t, sa_ref)
        scores(t + 2, sa_ref)
        consume(t + 1, sb_ref)
        return carry

    lax.fori_loop(0, (DIAG_TILES // 2) * i, pair, 0)
    first_diag = DIAG_TILES * i
    buffers = (sa_ref, sb_ref)
    for d in range(DIAG_TILES):
        if d + 1 < DIAG_TILES:
            scores(first_diag + d + 1, buffers[(d + 1) % 2])
        consume(first_diag + d, buffers[d % 2], diag=d)

    o = acc_ref[...] / l_ref[...]
    y = _sub_ln(o[:, 0:TQ], o[:, TQ:], _lambda(lam_ref, lam_init), g_ref[...], lam_init, axis=0)
    o_ref[...] = y.T.astype(BF16)


def _prompt_attn(st, layer, q, k, v, lam, subln_g, lam_init):
    assert DIAG_TILES % 2 == 0
    nq = st.seq // TQ
    return pl.pallas_call(
        functools.partial(_prompt_attn_kernel, lam_init=lam_init),
        grid=(st.batch, N_HEADS, nq),
        in_specs=[
            pl.BlockSpec((TQ, V_DIM), lambda b, h, i: (b * nq + i, h)),
            pl.BlockSpec((st.seq, V_DIM), lambda b, h, i: (b, h)),
            pl.BlockSpec((st.seq, V_DIM), lambda b, h, i: (b, h)),
            pl.BlockSpec((None, 4, HEAD_DIM), lambda b, h, i: (layer, 0, 0)),
            pl.BlockSpec((None, V_DIM, 1), lambda b, h, i: (layer, 0, 0)),
        ],
        out_specs=pl.BlockSpec((TQ, V_DIM), lambda b, h, i: (b * nq + i, h)),
        out_shape=jax.ShapeDtypeStruct((st.rows, ATTN_WIDTH), BF16),
        scratch_shapes=[
            pltpu.VMEM((2 * TQ, V_DIM), BF16),
            pltpu.VMEM((st.seq // TK, V_DIM, TK), BF16),
            pltpu.VMEM((TK, 2 * TQ), F32),
            pltpu.VMEM((TK, 2 * TQ), F32),
            pltpu.VMEM((1, 2 * TQ), F32),
            pltpu.VMEM((1, 2 * TQ), F32),
            pltpu.VMEM((V_DIM, 2 * TQ), F32),
        ],
        compiler_params=_params("arbitrary", "arbitrary", "arbitrary"),
        name="prompt_attn",
    )(q, k, v, lam, subln_g.reshape(DEPTH, V_DIM, 1))


def _decode_attn_kernel(q_ref, kc_ref, vc_ref, kn_ref, vn_ref, lam_ref, g_ref, o_ref,
                        m_ref, l_ref, acc_ref, *, lam_init):
    t = pl.program_id(1)
    n_score_heads = 2 * N_HEADS
    pair_rows = 2 * DEC_SEQ

    @pl.when(t == 0)
    def _():
        m_ref[...] = jnp.full_like(m_ref, -jnp.inf)
        l_ref[...] = jnp.zeros_like(l_ref)
        acc_ref[...] = jnp.zeros_like(acc_ref)

    q = q_ref[...]
    q_heads = [q[:, h * HEAD_DIM:(h + 1) * HEAD_DIM] for h in range(n_score_heads)]

    def update(score_of_head, value_of_head):
        s = jnp.concatenate([score_of_head(h) for h in range(n_score_heads)], axis=0)
        m_prev = m_ref[...]
        m_new = jnp.maximum(m_prev, jnp.max(s, axis=-1, keepdims=True))
        alpha = jnp.exp2(m_prev - m_new)
        p = jnp.exp2(s - m_new)
        l_ref[...] = alpha * l_ref[...] + jnp.sum(p, axis=-1, keepdims=True)
        pb = p.astype(BF16)
        pv = jnp.concatenate(
            [jnp.dot(pb[j * pair_rows:(j + 1) * pair_rows], value_of_head(j), preferred_element_type=F32)
             for j in range(N_HEADS)], axis=0)
        acc_ref[...] = alpha * acc_ref[...] + pv
        m_ref[...] = m_new

    update(lambda h: jnp.dot(q_heads[h], kc_ref[h * HEAD_DIM:(h + 1) * HEAD_DIM, :].astype(BF16),
                             preferred_element_type=F32),
           lambda j: vc_ref[pl.ds(j, TK_DEC, stride=N_HEADS), :].astype(BF16))

    @pl.when(t == pl.num_programs(1) - 1)
    def _():
        update(lambda h: lax.dot_general(q_heads[h], kn_ref[:, h * HEAD_DIM:(h + 1) * HEAD_DIM], _NT,
                                         preferred_element_type=F32),
               lambda j: vn_ref[:, j * V_DIM:(j + 1) * V_DIM])
        o = acc_ref[...] / l_ref[...]
        lam = _lambda(lam_ref, lam_init)
        g = g_ref[...]
        for j in range(N_HEADS):
            r0 = j * pair_rows
            y = _sub_ln(o[r0:r0 + DEC_SEQ], o[r0 + DEC_SEQ:r0 + pair_rows], lam, g, lam_init)
            o_ref[:, j * V_DIM:(j + 1) * V_DIM] = y.astype(BF16)


def _decode_attn(st, layer, q, k_new, v_new, cache_k, cache_v, lam, subln_g, lam_init):
    rows_bd = 2 * N_HEADS * DEC_SEQ
    tok = pl.BlockSpec((DEC_SEQ, ATTN_WIDTH), lambda b, t: (b, 0))
    tiles = PAST_LEN // TK_DEC
    cache_k_spec = pl.BlockSpec((None, ATTN_WIDTH, TK_DEC), lambda b, t: (layer * st.batch + b, 0, t))
    cache_v_spec = pl.BlockSpec((TK_DEC * N_HEADS, V_DIM),
                                lambda b, t: ((layer * st.batch + b) * tiles + t, 0))
    return pl.pallas_call(
        functools.partial(_decode_attn_kernel, lam_init=lam_init),
        grid=(st.batch, tiles),
        in_specs=[
            tok, cache_k_spec, cache_v_spec, tok, tok,
            pl.BlockSpec((None, 4, HEAD_DIM), lambda b, t: (layer, 0, 0)),
            pl.BlockSpec((None, 1, V_DIM), lambda b, t: (layer, 0, 0)),
        ],
        out_specs=tok,
        out_shape=jax.ShapeDtypeStruct((st.rows, ATTN_WIDTH), BF16),
        scratch_shapes=[
            pltpu.VMEM((rows_bd, 1), F32),
            pltpu.VMEM((rows_bd, 1), F32),
            pltpu.VMEM((rows_bd, V_DIM), F32),
        ],
        compiler_params=_params("arbitrary", "arbitrary"),
        name="decode_attn",
    )(q, cache_k, cache_v, k_new, v_new, lam, subln_g)


def _out_proj_kernel(x_ref, pool_ref, att_ref, w_ref, gate_ref, g_ref, sc_ref, sh_ref,
                     xo_ref, h_ref):
    mixed = (jnp.dot(pool_ref[...], w_ref[0:POOL_WIDTH, :], preferred_element_type=F32)
             + jnp.dot(att_ref[...], w_ref[POOL_WIDTH:, :], preferred_element_type=F32))
    x = x_ref[...] + gate_ref[...] * mixed
    xo_ref[...] = x
    y = _rms(x, NORM_EPS) * g_ref[...]
    h_ref[...] = (y * (1.0 + sc_ref[...]) + sh_ref[...]).astype(BF16)


def _out_proj(st, layer, x, pool_out, att, mod, w_out, norm_g):
    m = st.rows
    tm = TM
    row = lambda i, j: (i, 0)
    return pl.pallas_call(
        _out_proj_kernel,
        grid=(m // tm, 1),
        in_specs=[
            pl.BlockSpec((tm, D_MODEL), row),
            pl.BlockSpec((tm, POOL_WIDTH), row),
            pl.BlockSpec((tm, ATTN_WIDTH), row),
            pl.BlockSpec((None, D_MODEL, D_MODEL), lambda i, j: (layer, 0, 0),
                         pipeline_mode=pl.Buffered(1)),
            _mod_spec(st, layer, MOD_GATE_A, tm),
            pl.BlockSpec((None, 1, D_MODEL), lambda i, j: (layer, 0, 0)),
            _mod_spec(st, layer, MOD_SCALE_F, tm),
            _mod_spec(st, layer, MOD_SHIFT_F, tm),
        ],
        out_specs=[pl.BlockSpec((tm, D_MODEL), row), pl.BlockSpec((tm, D_MODEL), row)],
        out_shape=[jax.ShapeDtypeStruct((m, D_MODEL), F32), jax.ShapeDtypeStruct((m, D_MODEL), BF16)],
        compiler_params=_params("arbitrary", "arbitrary"),
        name="out_proj",
    )(x, pool_out, att, w_out, mod, norm_g, mod, mod)


def _gate_up_kernel(h_ref, wg_ref, wu_ref, o_ref):
    h = h_ref[...]
    g = jnp.dot(h, wg_ref[...], preferred_element_type=F32)
    u = jnp.dot(h, wu_ref[...], preferred_element_type=F32)
    o_ref[...] = (g * jax.nn.sigmoid(g) * u).astype(BF16)


def _gate_up(st, layer, h, w_gate, w_up):
    m = st.rows
    tm = min(TM_FFN, m)
    w_spec = pl.BlockSpec((None, D_MODEL, TF), lambda i, j: (layer, 0, j))
    return pl.pallas_call(
        _gate_up_kernel,
        grid=(m // tm, D_FF // TF),
        in_specs=[pl.BlockSpec((tm, D_MODEL), lambda i, j: (i, 0)), w_spec, w_spec],
        out_specs=pl.BlockSpec((tm, TF), lambda i, j: (i, j)),
        out_shape=jax.ShapeDtypeStruct((m, D_FF), BF16),
        compiler_params=_params("arbitrary", "arbitrary"),
        name="ffn_gate_up",
    )(h, w_gate, w_up)


def _down_kernel(a_ref, w_ref, x_ref, gate_ref, o_ref):
    y = jnp.dot(a_ref[...], w_ref[...], preferred_element_type=F32)
    o_ref[...] = x_ref[...] + gate_ref[...] * y


def _down(st, layer, act, x, mod, w_down):
    m = st.rows
    tm = min(TM_FFN, m)
    return pl.pallas_call(
        _down_kernel,
        grid=(m // tm, D_MODEL // TN_DOWN),
        in_specs=[
            pl.BlockSpec((tm, D_FF), lambda i, j: (i, 0)),
            pl.BlockSpec((None, D_FF, TN_DOWN), lambda i, j: (layer, 0, j)),
            pl.BlockSpec((tm, TN_DOWN), lambda i, j: (i, j)),
            _mod_spec(st, layer, MOD_GATE_F, tm, TN_DOWN),
        ],
        out_specs=pl.BlockSpec((tm, TN_DOWN), lambda i, j: (i, j)),
        out_shape=jax.ShapeDtypeStruct((m, D_MODEL), F32),
        compiler_params=_params("arbitrary", "arbitrary"),
        name="ffn_down",
    )(act, w_down, x, mod)


def _final_norm_kernel(x_ref, g_ref, o_ref):
    o_ref[...] = _rms(x_ref[...], NORM_EPS) * g_ref[...]


def _final_norm(x, g):
    m = x.shape[0]
    return pl.pallas_call(
        _final_norm_kernel,
        grid=(m // TM,),
        in_specs=[pl.BlockSpec((TM, D_MODEL), lambda i: (i, 0)), pl.BlockSpec((1, D_MODEL), lambda i: (0, 0))],
        out_specs=pl.BlockSpec((TM, D_MODEL), lambda i: (i, 0)),
        out_shape=jax.ShapeDtypeStruct((m, D_MODEL), F32),
        compiler_params=_params("arbitrary"),
        name="final_norm",
    )(x, g)


def _rope_tables(st):
    half = HEAD_DIM // 2
    inv = 1.0 / (ROPE_THETA ** (jnp.arange(half, dtype=F32) * (2.0 / HEAD_DIM)))
    pos = st.pos0 + jnp.arange(st.seq)
    ang = pos.astype(F32)[:, None] * inv[None, :]
    cos, sin = jnp.cos(ang), jnp.sin(ang)
    reps = V7X_LANES // HEAD_DIM
    cos_t = jnp.tile(jnp.concatenate([cos, cos], axis=-1), (1, reps))
    sin_t = jnp.tile(jnp.concatenate([-sin, sin], axis=-1), (1, reps))
    if st.seq < TM:
        cos_t = jnp.tile(cos_t, (TM // st.seq, 1))
        sin_t = jnp.tile(sin_t, (TM // st.seq, 1))
    return cos_t, sin_t


def _run_stream(st, x, mod, hist, caches, p):
    m = st.rows
    x = x.reshape(m, D_MODEL)
    cos_tab, sin_tab = _rope_tables(st)
    kv_stacks, tails = None, []
    for l in range(DEPTH):
        lam_init = 0.8 - 0.6 * math.exp(-0.3 * l)
        u, q, k_stack, kb, v_stack, vb = _in_proj(st, l, x, mod, p["norm_mix_g"], p["w_in"],
                                                  cos_tab, sin_tab, kv_stacks)
        kv_stacks = (k_stack, v_stack)
        pool_out, tail = _pool(st, l, u, hist[l], p["w_pool"], p["pool_scale"])
        tails.append(tail)
        if caches is None:
            att = _prompt_attn(st, l, q, kb, vb, p["lam"], p["subln_g"], lam_init)
        else:
            att = _decode_attn(st, l, q, kb, vb, caches[0], caches[1], p["lam"], p["subln_g"], lam_init)
        x, h = _out_proj(st, l, x, pool_out, att, mod, p["w_out"], p["norm_ffn_g"])
        act = _gate_up(st, l, h, p["w_gate"], p["w_up"])
        x = _down(st, l, act, x, mod, p["w_down"])
    y = _final_norm(x, p["final_g"]).reshape(st.batch, st.seq, D_MODEL)
    k_stack, v_stack = kv_stacks
    if st.k_transposed:
        k_out = k_stack.reshape(DEPTH, st.batch, 2 * N_HEADS, HEAD_DIM, st.seq)
        k_out = jnp.transpose(k_out, (0, 1, 4, 2, 3))
    else:
        k_out = k_stack.reshape(DEPTH, st.batch, st.seq, 2 * N_HEADS, HEAD_DIM)
    v_out = v_stack.reshape(DEPTH, st.batch, st.seq, N_HEADS, V_DIM)
    return y, k_out, v_out, jnp.stack(tails)[:, :, HALO - POOL_HIST:]


def kernel(x_prompt, x_sample, c_prompt, c_sample, cache_k, cache_v, state_pool, w_mod, b_mod,
           norm_mix_g, w_in, w_pool, pool_scale, lam_q1, lam_k1, lam_q2, lam_k2, subln_g, w_out,
           norm_ffn_g, w_gate, w_up, w_down, final_g):
    p = {
        "norm_mix_g": norm_mix_g.reshape(DEPTH, 1, D_MODEL),
        "norm_ffn_g": norm_ffn_g.reshape(DEPTH, 1, D_MODEL),
        "final_g": final_g.reshape(1, D_MODEL),
        "w_in": w_in.astype(BF16),
        "w_pool": w_pool.astype(BF16),
        "pool_scale": pool_scale.reshape(DEPTH, 1, POOL_WIDTH),
        "lam": jnp.stack([lam_q1, lam_k1, lam_q2, lam_k2], axis=1),
        "subln_g": subln_g.reshape(DEPTH, 1, V_DIM),
        "w_out": w_out.astype(BF16),
        "w_gate": w_gate.astype(BF16),
        "w_up": w_up.astype(BF16),
        "w_down": w_down.astype(BF16),
    }
    c_all = jnp.concatenate([c_prompt, c_sample], axis=0)
    c_all = jnp.pad(c_all, ((0, -c_all.shape[0] % 8), (0, 0)))
    mod = _modulation(c_all, w_mod, b_mod)[:, :BATCH + DEC_BATCH]
    mod_prompt = mod[:, :BATCH].reshape(DEPTH * BATCH, 1, N_MOD * D_MODEL)
    mod_sample = jnp.repeat(mod[:, BATCH:], DEC_SEQ, axis=1)

    hist_prompt = jnp.zeros((DEPTH, BATCH, HALO, POOL_WIDTH), F32)
    hist_sample = jnp.pad(state_pool, ((0, 0), (0, 0), (HALO - POOL_HIST, 0), (0, 0)))
    caches = (jnp.transpose(cache_k, (0, 1, 3, 4, 2)).reshape(DEPTH * DEC_BATCH, ATTN_WIDTH, PAST_LEN),
              cache_v.reshape(-1, V_DIM))

    y_p, k_p, v_p, pool_p = _run_stream(PROMPT, x_prompt, mod_prompt, hist_prompt, None, p)
    y_s, k_s, v_s, pool_s = _run_stream(SAMPLE, x_sample, mod_sample, hist_sample, caches, p)
    return (y_p, y_s, k_p, v_p, pool_p, k_s, v_s, pool_s)
```

```python
import functools
import math
from typing import NamedTuple

import jax
import jax.numpy as jnp
from jax import lax
from jax.experimental import pallas as pl
from jax.experimental.pallas import tpu as pltpu

F32 = jnp.float32
BF16 = jnp.bfloat16

D_MODEL = 2048
BATCH = 4
SEQ = 4096
DEPTH = 4
DEC_BATCH = 32
DEC_SEQ = 16
PAST_LEN = 2048
CHUNK = 64
POOL_WIDTH = D_MODEL // 2
N_POOL_GROUPS = 4
POOL_GROUP = POOL_WIDTH // N_POOL_GROUPS
POOL_WINDOWS = (2, 4, 8, 16)
POOL_HIST = max(POOL_WINDOWS) - 1
HALO = POOL_HIST + 1
ATTN_WIDTH = D_MODEL - POOL_WIDTH
HEAD_DIM = 64
N_HEADS = ATTN_WIDTH // (2 * HEAD_DIM)
V_DIM = 2 * HEAD_DIM
IN_COLS = POOL_WIDTH + 3 * ATTN_WIDTH
IN_SECTIONS = IN_COLS // ATTN_WIDTH
D_FF = -(-8 * D_MODEL // (3 * 256)) * 256
ROPE_THETA = 10000.0
NORM_EPS = 1e-6
SUBLN_EPS = 1e-5
N_MOD = 6
MOD_SHIFT_A, MOD_SCALE_A, MOD_GATE_A, MOD_SHIFT_F, MOD_SCALE_F, MOD_GATE_F = range(N_MOD)

V7X_VMEM_BYTES = 64 * 1024 * 1024
V7X_LANES = 128
VMEM_LIMIT = V7X_VMEM_BYTES * 7 // 8

TM = 512
TM_FFN = 1024
TN_MOD = 1024
TF = 512
TN_DOWN = 512
TM_POOL = 1024
TQ = 1024
TK = 512
DIAG_TILES = TQ // TK
SCORE_SCALE_LOG2 = HEAD_DIM ** -0.5 * math.log2(math.e)
TK_DEC = 2048


class Stream(NamedTuple):
    batch: int
    seq: int
    pos0: int
    mod_rows: int
    k_transposed: bool

    @property
    def rows(self):
        return self.batch * self.seq


PROMPT = Stream(BATCH, SEQ, 0, 1, True)
SAMPLE = Stream(DEC_BATCH, DEC_SEQ, PAST_LEN, DEC_BATCH * DEC_SEQ, False)


def _params(*semantics, flags=None):
    return pltpu.CompilerParams(dimension_semantics=semantics, vmem_limit_bytes=VMEM_LIMIT, flags=flags)


def _rms(x, eps, axis=-1):
    return x * lax.rsqrt(jnp.mean(x * x, axis=axis, keepdims=True) + eps)


def _mod_spec(st, layer, which, tm, tn=D_MODEL):
    per_chunk = D_MODEL // tn
    if st.mod_rows == 1:
        tiles_per_batch = st.seq // tm
        return pl.BlockSpec(
            (None, 1, tn),
            lambda i, j: (layer * st.batch + i // tiles_per_batch, 0, which * per_chunk + j % per_chunk))
    return pl.BlockSpec((None, tm, tn), lambda i, j: (layer, i, which * per_chunk + j % per_chunk))


def _mod_kernel(c_ref, w_ref, b_ref, o_ref):
    c = c_ref[...]
    cs = (c * jax.nn.sigmoid(c)).astype(BF16)
    o_ref[...] = jnp.dot(cs, w_ref[...].astype(BF16), preferred_element_type=F32) + b_ref[...]


def _modulation(c_all, w_mod, b_mod):
    nb = c_all.shape[0]
    n = N_MOD * D_MODEL
    return pl.pallas_call(
        _mod_kernel,
        grid=(DEPTH, n // TN_MOD),
        in_specs=[
            pl.BlockSpec((nb, D_MODEL), lambda l, j: (0, 0)),
            pl.BlockSpec((None, D_MODEL, TN_MOD), lambda l, j: (l, 0, j)),
            pl.BlockSpec((None, 1, TN_MOD), lambda l, j: (l, 0, j)),
        ],
        out_specs=pl.BlockSpec((None, nb, TN_MOD), lambda l, j: (l, 0, j)),
        out_shape=jax.ShapeDtypeStruct((DEPTH, nb, n), F32),
        compiler_params=_params("arbitrary", "arbitrary"),
        name="modulation",
    )(c_all, w_mod, b_mod.reshape(DEPTH, 1, n))


def _rope_store(acc, cos_ref, sin_ref, scale, out_refs, transposed_ref=None):
    tm = acc.shape[0]
    lane = lax.broadcasted_iota(jnp.int32, (tm, V7X_LANES), 1)
    first_half = (lane % HEAD_DIM) < (HEAD_DIM // 2)
    cos = cos_ref[...]
    sin = sin_ref[...]
    for c in range(ATTN_WIDTH // V7X_LANES):
        cols = slice(c * V7X_LANES, (c + 1) * V7X_LANES)
        xs = acc[:, cols]
        partner = jnp.where(first_half,
                            pltpu.roll(xs, V7X_LANES - HEAD_DIM // 2, 1),
                            pltpu.roll(xs, HEAD_DIM // 2, 1))
        y = xs * cos + partner * sin
        if scale != 1.0:
            y = y * scale
        for r in out_refs:
            r[:, cols] = y.astype(r.dtype)
        if transposed_ref is not None:
            transposed_ref[cols, :] = y.T


def _in_proj_kernel(x_ref, g_ref, sc_ref, sh_ref, w_ref, cos_ref, sin_ref, *rest,
                    k_transposed, tiles):
    u_ref, q_ref, k_ref, kb_ref, v_ref, vb_ref, h_cur, h_next, acc_a, acc_b = rest[-10:]
    s = pl.program_id(0)
    section = s % IN_SECTIONS
    last = IN_SECTIONS * tiles

    def normalise(dst):
        y = _rms(x_ref[...], NORM_EPS) * g_ref[...]
        dst[...] = (y * (1.0 + sc_ref[...]) + sh_ref[...]).astype(BF16)

    def multiply(dst):
        dst[...] = jnp.dot(h_cur[...], w_ref[...], preferred_element_type=F32)

    def finish_v(acc):
        for h in range(N_HEADS):
            v_ref[pl.ds(h, acc.shape[0], stride=N_HEADS), :] = acc[:, h * V_DIM:(h + 1) * V_DIM]
        vb_ref[...] = acc[...].astype(BF16)

    @pl.when(s == 0)
    def _():
        normalise(h_cur)
        multiply(acc_a)

    @pl.when((section == 0) & (s > 0) & (s < last))
    def _():
        h_cur[...] = h_next[...]
        multiply(acc_a)
        finish_v(acc_b)

    @pl.when(section == 1)
    def _():
        multiply(acc_b)
        u_ref[...] = acc_a[...]
        normalise(h_next)

    @pl.when(section == 2)
    def _():
        multiply(acc_a)
        _rope_store(acc_b, cos_ref, sin_ref, SCORE_SCALE_LOG2, (q_ref,))

    @pl.when(section == 3)
    def _():
        multiply(acc_b)
        if k_transposed:
            _rope_store(acc_a, cos_ref, sin_ref, 1.0, (kb_ref,), k_ref)
        else:
            _rope_store(acc_a, cos_ref, sin_ref, 1.0, (k_ref, kb_ref))

    @pl.when(s == last)
    def _():
        finish_v(acc_b)


def _in_proj(st, layer, x, mod, norm_g, w_in, cos_tab, sin_tab, kv_stacks):
    m = st.rows
    tm = TM
    tiles = m // tm
    tab_tiles = cos_tab.shape[0] // tm
    steps = IN_SECTIONS * tiles + 1
    done = lambda s: jnp.maximum(s - 1, 0) // IN_SECTIONS
    normed = lambda s: jnp.minimum((s + IN_SECTIONS - 1) // IN_SECTIONS, tiles - 1)
    out_spec = pl.BlockSpec((tm, ATTN_WIDTH), lambda s: (done(s), 0))
    stack_spec = pl.BlockSpec((tm, ATTN_WIDTH), lambda s: (layer * tiles + done(s), 0))
    sds = lambda dt: jax.ShapeDtypeStruct((m, ATTN_WIDTH), dt)
    stack_sds = jax.ShapeDtypeStruct((DEPTH * m, ATTN_WIDTH), F32)
    v_stack_spec = pl.BlockSpec((tm * N_HEADS, V_DIM), lambda s: (layer * tiles + done(s), 0))
    v_stack_sds = jax.ShapeDtypeStruct((DEPTH * m * N_HEADS, V_DIM), F32)
    out_specs = [out_spec, out_spec, stack_spec, out_spec, v_stack_spec, out_spec]
    out_shape = [sds(F32), sds(BF16), stack_sds, sds(BF16), v_stack_sds, sds(BF16)]
    if st.k_transposed:
        tiles_per_batch = st.seq // tm
        out_specs[2] = pl.BlockSpec(
            (None, ATTN_WIDTH, tm),
            lambda s: (layer * st.batch + done(s) // tiles_per_batch, 0, done(s) % tiles_per_batch))
        out_shape[2] = jax.ShapeDtypeStruct((DEPTH * st.batch, ATTN_WIDTH, st.seq), F32)
    if st.mod_rows == 1:
        tiles_per_batch = st.seq // tm
        mod_spec = lambda which: pl.BlockSpec(
            (None, 1, D_MODEL), lambda s: (layer * st.batch + normed(s) // tiles_per_batch, 0, which))
    else:
        mod_spec = lambda which: pl.BlockSpec((None, tm, D_MODEL), lambda s: (layer, normed(s), which),
                                              pipeline_mode=pl.Buffered(1))
    in_specs = [
        pl.BlockSpec((tm, D_MODEL), lambda s: (normed(s), 0)),
        pl.BlockSpec((None, 1, D_MODEL), lambda s: (layer, 0, 0)),
        mod_spec(MOD_SCALE_A),
        mod_spec(MOD_SHIFT_A),
        pl.BlockSpec((None, D_MODEL, ATTN_WIDTH),
                     lambda s: (layer, 0, jnp.minimum(s, steps - 2) % IN_SECTIONS)),
        pl.BlockSpec((tm, V7X_LANES), lambda s: (done(s) % tab_tiles, 0)),
        pl.BlockSpec((tm, V7X_LANES), lambda s: (done(s) % tab_tiles, 0)),
    ]
    args = [x, norm_g, mod, mod, w_in, cos_tab, sin_tab]
    aliases = {}
    if kv_stacks is not None:
        aliases = {len(args): 2, len(args) + 1: 4}
        in_specs += [pl.BlockSpec(memory_space=pl.ANY)] * 2
        args += list(kv_stacks)
    return pl.pallas_call(
        functools.partial(_in_proj_kernel, k_transposed=st.k_transposed, tiles=tiles),
        grid=(steps,),
        in_specs=in_specs,
        out_specs=out_specs,
        out_shape=out_shape,
        input_output_aliases=aliases,
        scratch_shapes=[pltpu.VMEM((tm, D_MODEL), BF16), pltpu.VMEM((tm, D_MODEL), BF16),
                        pltpu.VMEM((tm, ATTN_WIDTH), F32), pltpu.VMEM((tm, ATTN_WIDTH), F32)],
        compiler_params=_params("arbitrary"),
        name="in_proj",
    )(*args)


def _pool_kernel(u_ref, hist_ref, wp_ref, ps_ref, o_ref, tail_ref, ext_ref, *, tm, pos0):
    i = pl.program_id(1)

    @pl.when(i == 0)
    def _():
        ext_ref[0:HALO, :] = hist_ref[...]

    @pl.when(i > 0)
    def _():
        ext_ref[0:HALO, :] = ext_ref[tm:tm + HALO, :]

    ext_ref[HALO:, :] = u_ref[...]

    @pl.when(i == pl.num_programs(1) - 1)
    def _():
        tail_ref[...] = ext_ref[tm:tm + HALO, :]

    pos = pos0 + i * tm + lax.broadcasted_iota(jnp.int32, (tm, 1), 0)
    for g, w in enumerate(POOL_WINDOWS):
        cols = slice(g * POOL_GROUP, (g + 1) * POOL_GROUP)
        cur = u_ref[:, cols]
        s = ext_ref[:, cols]
        shift = 1
        while shift < w:
            s = s + pltpu.roll(s, shift, 0)
            shift *= 2
        s = s[HALO:]
        count = jnp.minimum(w, pos + 1).astype(F32)
        d = (s / count - cur).astype(BF16)
        y = jnp.dot(d, wp_ref[g], preferred_element_type=F32) * ps_ref[:, cols]
        o_ref[:, cols] = y.astype(BF16)


def _pool(st, layer, u, hist, w_pool, pool_scale):
    tm = min(TM_POOL, st.seq)
    tiles = st.seq // tm
    return pl.pallas_call(
        functools.partial(_pool_kernel, tm=tm, pos0=st.pos0),
        grid=(st.batch, tiles),
        in_specs=[
            pl.BlockSpec((tm, POOL_WIDTH), lambda b, i: (b * tiles + i, 0)),
            pl.BlockSpec((None, HALO, POOL_WIDTH), lambda b, i: (b, 0, 0)),
            pl.BlockSpec((None, N_POOL_GROUPS, POOL_GROUP, POOL_GROUP), lambda b, i: (layer, 0, 0, 0)),
            pl.BlockSpec((None, 1, POOL_WIDTH), lambda b, i: (layer, 0, 0)),
        ],
        out_specs=[pl.BlockSpec((tm, POOL_WIDTH), lambda b, i: (b * tiles + i, 0)),
                   pl.BlockSpec((None, HALO, POOL_WIDTH), lambda b, i: (b, 0, 0))],
        out_shape=[jax.ShapeDtypeStruct((st.rows, POOL_WIDTH), BF16),
                   jax.ShapeDtypeStruct((st.batch, HALO, POOL_WIDTH), F32)],
        scratch_shapes=[pltpu.VMEM((tm + HALO, POOL_WIDTH), F32)],
        compiler_params=_params("arbitrary", "arbitrary"),
        name="pool_mix",
    )(u, hist, w_pool, pool_scale)


def _lambda(lam_ref, lam_init):
    lam = lam_ref[...]
    s1 = jnp.sum(lam[0:1] * lam[1:2], axis=-1, keepdims=True)
    s2 = jnp.sum(lam[2:3] * lam[3:4], axis=-1, keepdims=True)
    return jnp.exp(s1) - jnp.exp(s2) + lam_init


def _online_softmax_step(s, v, m_ref, l_ref, acc_ref):
    m_prev = m_ref[...]
    m_new = jnp.maximum(m_prev, jnp.max(s, axis=-1, keepdims=True))
    alpha = jnp.exp2(m_prev - m_new)
    p = jnp.exp2(s - m_new)
    l_ref[...] = alpha * l_ref[...] + jnp.sum(p, axis=-1, keepdims=True)
    acc_ref[...] = alpha * acc_ref[...] + jnp.dot(p.astype(BF16), v, preferred_element_type=F32)
    m_ref[...] = m_new


def _sub_ln(o1, o2, lam, g, lam_init, axis=-1):
    a = o1 - lam * o2
    return (_rms(a, SUBLN_EPS, axis) * g) * (1.0 - lam_init)


_NT = (((1,), (1,)), ((), ()))


def _prompt_attn_kernel(q_ref, k_ref, v_ref, lam_ref, g_ref, o_ref,
                        qz_ref, vt_ref, sa_ref, sb_ref, m_ref, l_ref, acc_ref, *, lam_init):
    i = pl.program_id(2)

    @pl.when(i == 0)
    def _():
        for t in range(vt_ref.shape[0]):
            vt_ref[t] = v_ref[t * TK:(t + 1) * TK, :].astype(F32).T.astype(BF16)

    q = q_ref[...]
    lane = lax.broadcasted_iota(jnp.int32, (TQ, V_DIM), 1)
    zero = jnp.zeros_like(q)
    qz_ref[0:TQ, :] = jnp.where(lane < HEAD_DIM, q, zero)
    qz_ref[TQ:, :] = jnp.where(lane >= HEAD_DIM, q, zero)
    m_ref[...] = jnp.full_like(m_ref, -jnp.inf)
    l_ref[...] = jnp.zeros_like(l_ref)
    acc_ref[...] = jnp.zeros_like(acc_ref)

    def scores(t, s_ref):
        start = pl.multiple_of(t * TK, TK)
        s_ref[...] = lax.dot_general(k_ref[pl.ds(start, TK), :], qz_ref[...], _NT,
                                     preferred_element_type=F32)

    def consume(t, s_ref, diag=None):
        q_lo = 0 if diag is None else diag * TK
        width = TQ - q_lo
        col_sets = (slice(q_lo, TQ), slice(TQ + q_lo, 2 * TQ))

        def gather(ref):
            if q_lo == 0:
                return ref[...]
            return jnp.concatenate([ref[:, c] for c in col_sets], axis=1)

        def scatter(ref, val):
            if q_lo == 0:
                ref[...] = val
            else:
                for n, c in enumerate(col_sets):
                    ref[:, c] = val[:, n * width:(n + 1) * width]

        s = gather(s_ref)
        if diag is not None:
            key = q_lo + lax.broadcasted_iota(jnp.int32, (TK, 2 * width), 0)
            col = lax.broadcasted_iota(jnp.int32, (TK, 2 * width), 1)
            query = q_lo + jnp.where(col >= width, col - width, col)
            s = jnp.where((key // CHUNK) <= (query // CHUNK), s, -jnp.inf)
        m_prev = gather(m_ref)
        m_new = jnp.maximum(m_prev, jnp.max(s, axis=0, keepdims=True))
        alpha = jnp.exp2(m_prev - m_new)
        p = jnp.exp2(s - m_new)
        scatter(l_ref, alpha * gather(l_ref) + jnp.sum(p, axis=0, keepdims=True))
        scatter(acc_ref, alpha * gather(acc_ref) + jnp.dot(vt_ref[t], p.astype(BF16),
                                                           preferred_element_type=F32))
        scatter(m_ref, m_new)

    scores(0, sa_ref)

    def pair(p, carry):
        t = 2 * p
        scores(t + 1, sb_ref)
        consume(t, sa_ref)
        scores(t + 2, sa_ref)
        consume(t + 1, sb_ref)
        return carry

    lax.fori_loop(0, (DIAG_TILES // 2) * i, pair, 0)
    first_diag = DIAG_TILES * i
    buffers = (sa_ref, sb_ref)
    for d in range(DIAG_TILES):
        if d + 1 < DIAG_TILES:
            scores(first_diag + d + 1, buffers[(d + 1) % 2])
        consume(first_diag + d, buffers[d % 2], diag=d)

    o = acc_ref[...] / l_ref[...]
    y = _sub_ln(o[:, 0:TQ], o[:, TQ:], _lambda(lam_ref, lam_init), g_ref[...], lam_init, axis=0)
    o_ref[...] = y.T.astype(BF16)


def _prompt_attn(st, layer, q, k, v, lam, subln_g, lam_init):
    assert DIAG_TILES % 2 == 0
    nq = st.seq // TQ
    return pl.pallas_call(
        functools.partial(_prompt_attn_kernel, lam_init=lam_init),
        grid=(st.batch, N_HEADS, nq),
        in_specs=[
            pl.BlockSpec((TQ, V_DIM), lambda b, h, i: (b * nq + i, h)),
            pl.BlockSpec((st.seq, V_DIM), lambda b, h, i: (b, h)),
            pl.BlockSpec((st.seq, V_DIM), lambda b, h, i: (b, h)),
            pl.BlockSpec((None, 4, HEAD_DIM), lambda b, h, i: (layer, 0, 0)),
            pl.BlockSpec((None, V_DIM, 1), lambda b, h, i: (layer, 0, 0)),
        ],
        out_specs=pl.BlockSpec((TQ, V_DIM), lambda b, h, i: (b * nq + i, h)),
        out_shape=jax.ShapeDtypeStruct((st.rows, ATTN_WIDTH), BF16),
        scratch_shapes=[
            pltpu.VMEM((2 * TQ, V_DIM), BF16),
            pltpu.VMEM((st.seq // TK, V_DIM, TK), BF16),
            pltpu.VMEM((TK, 2 * TQ), F32),
            pltpu.VMEM((TK, 2 * TQ), F32),
            pltpu.VMEM((1, 2 * TQ), F32),
            pltpu.VMEM((1, 2 * TQ), F32),
            pltpu.VMEM((V_DIM, 2 * TQ), F32),
        ],
        compiler_params=_params("arbitrary", "arbitrary", "arbitrary"),
        name="prompt_attn",
    )(q, k, v, lam, subln_g.reshape(DEPTH, V_DIM, 1))


def _decode_attn_kernel(q_ref, kc_ref, vc_ref, kn_ref, vn_ref, lam_ref, g_ref, o_ref,
                        m_ref, l_ref, acc_ref, *, lam_init):
    t = pl.program_id(1)
    n_score_heads = 2 * N_HEADS
    pair_rows = 2 * DEC_SEQ

    @pl.when(t == 0)
    def _():
        m_ref[...] = jnp.full_like(m_ref, -jnp.inf)
        l_ref[...] = jnp.zeros_like(l_ref)
        acc_ref[...] = jnp.zeros_like(acc_ref)

    q = q_ref[...]
    q_heads = [q[:, h * HEAD_DIM:(h + 1) * HEAD_DIM] for h in range(n_score_heads)]

    def update(score_of_head, value_of_head):
        s = jnp.concatenate([score_of_head(h) for h in range(n_score_heads)], axis=0)
        m_prev = m_ref[...]
        m_new = jnp.maximum(m_prev, jnp.max(s, axis=-1, keepdims=True))
        alpha = jnp.exp2(m_prev - m_new)
        p = jnp.exp2(s - m_new)
        l_ref[...] = alpha * l_ref[...] + jnp.sum(p, axis=-1, keepdims=True)
        pb = p.astype(BF16)
        pv = jnp.concatenate(
            [jnp.dot(pb[j * pair_rows:(j + 1) * pair_rows], value_of_head(j), preferred_element_type=F32)
             for j in range(N_HEADS)], axis=0)
        acc_ref[...] = alpha * acc_ref[...] + pv
        m_ref[...] = m_new

    update(lambda h: jnp.dot(q_heads[h], kc_ref[h * HEAD_DIM:(h + 1) * HEAD_DIM, :].astype(BF16),
                             preferred_element_type=F32),
           lambda j: vc_ref[pl.ds(j, TK_DEC, stride=N_HEADS), :].astype(BF16))

    @pl.when(t == pl.num_programs(1) - 1)
    def _():
        update(lambda h: lax.dot_general(q_heads[h], kn_ref[:, h * HEAD_DIM:(h + 1) * HEAD_DIM], _NT,
                                         preferred_element_type=F32),
               lambda j: vn_ref[:, j * V_DIM:(j + 1) * V_DIM])
        o = acc_ref[...] / l_ref[...]
        lam = _lambda(lam_ref, lam_init)
        g = g_ref[...]
        for j in range(N_HEADS):
            r0 = j * pair_rows
            y = _sub_ln(o[r0:r0 + DEC_SEQ], o[r0 + DEC_SEQ:r0 + pair_rows], lam, g, lam_init)
            o_ref[:, j * V_DIM:(j + 1) * V_DIM] = y.astype(BF16)


def _decode_attn(st, layer, q, k_new, v_new, cache_k, cache_v, lam, subln_g, lam_init):
    rows_bd = 2 * N_HEADS * DEC_SEQ
    tok = pl.BlockSpec((DEC_SEQ, ATTN_WIDTH), lambda b, t: (b, 0))
    tiles = PAST_LEN // TK_DEC
    cache_k_spec = pl.BlockSpec((None, ATTN_WIDTH, TK_DEC), lambda b, t: (layer * st.batch + b, 0, t))
    cache_v_spec = pl.BlockSpec((TK_DEC * N_HEADS, V_DIM),
                                lambda b, t: ((layer * st.batch + b) * tiles + t, 0))
    return pl.pallas_call(
        functools.partial(_decode_attn_kernel, lam_init=lam_init),
        grid=(st.batch, tiles),
        in_specs=[
            tok, cache_k_spec, cache_v_spec, tok, tok,
            pl.BlockSpec((None, 4, HEAD_DIM), lambda b, t: (layer, 0, 0)),
            pl.BlockSpec((None, 1, V_DIM), lambda b, t: (layer, 0, 0)),
        ],
        out_specs=tok,
        out_shape=jax.ShapeDtypeStruct((st.rows, ATTN_WIDTH), BF16),
        scratch_shapes=[
            pltpu.VMEM((rows_bd, 1), F32),
            pltpu.VMEM((rows_bd, 1), F32),
            pltpu.VMEM((rows_bd, V_DIM), F32),
        ],
        compiler_params=_params("arbitrary", "arbitrary"),
        name="decode_attn",
    )(q, cache_k, cache_v, k_new, v_new, lam, subln_g)


def _out_proj_kernel(x_ref, pool_ref, att_ref, w_ref, gate_ref, g_ref, sc_ref, sh_ref,
                     xo_ref, h_ref):
    mixed = (jnp.dot(pool_ref[...], w_ref[0:POOL_WIDTH, :], preferred_element_type=F32)
             + jnp.dot(att_ref[...], w_ref[POOL_WIDTH:, :], preferred_element_type=F32))
    x = x_ref[...] + gate_ref[...] * mixed
    xo_ref[...] = x
    y = _rms(x, NORM_EPS) * g_ref[...]
    h_ref[...] = (y * (1.0 + sc_ref[...]) + sh_ref[...]).astype(BF16)


def _out_proj(st, layer, x, pool_out, att, mod, w_out, norm_g):
    m = st.rows
    tm = TM
    row = lambda i, j: (i, 0)
    return pl.pallas_call(
        _out_proj_kernel,
        grid=(m // tm, 1),
        in_specs=[
            pl.BlockSpec((tm, D_MODEL), row),
            pl.BlockSpec((tm, POOL_WIDTH), row),
            pl.BlockSpec((tm, ATTN_WIDTH), row),
            pl.BlockSpec((None, D_MODEL, D_MODEL), lambda i, j: (layer, 0, 0),
                         pipeline_mode=pl.Buffered(1)),
            _mod_spec(st, layer, MOD_GATE_A, tm),
            pl.BlockSpec((None, 1, D_MODEL), lambda i, j: (layer, 0, 0)),
            _mod_spec(st, layer, MOD_SCALE_F, tm),
            _mod_spec(st, layer, MOD_SHIFT_F, tm),
        ],
        out_specs=[pl.BlockSpec((tm, D_MODEL), row), pl.BlockSpec((tm, D_MODEL), row)],
        out_shape=[jax.ShapeDtypeStruct((m, D_MODEL), F32), jax.ShapeDtypeStruct((m, D_MODEL), BF16)],
        compiler_params=_params("arbitrary", "arbitrary"),
        name="out_proj",
    )(x, pool_out, att, w_out, mod, norm_g, mod, mod)


def _gate_up_kernel(h_ref, wg_ref, wu_ref, o_ref):
    h = h_ref[...]
    g = jnp.dot(h, wg_ref[...], preferred_element_type=F32)
    u = jnp.dot(h, wu_ref[...], preferred_element_type=F32)
    o_ref[...] = (g * jax.nn.sigmoid(g) * u).astype(BF16)


def _gate_up(st, layer, h, w_gate, w_up):
    m = st.rows
    tm = min(TM_FFN, m)
    w_spec = pl.BlockSpec((None, D_MODEL, TF), lambda i, j: (layer, 0, j))
    return pl.pallas_call(
        _gate_up_kernel,
        grid=(m // tm, D_FF // TF),
        in_specs=[pl.BlockSpec((tm, D_MODEL), lambda i, j: (i, 0)), w_spec, w_spec],
        out_specs=pl.BlockSpec((tm, TF), lambda i, j: (i, j)),
        out_shape=jax.ShapeDtypeStruct((m, D_FF), BF16),
        compiler_params=_params("arbitrary", "arbitrary"),
        name="ffn_gate_up",
    )(h, w_gate, w_up)


def _down_kernel(a_ref, w_ref, x_ref, gate_ref, o_ref):
    y = jnp.dot(a_ref[...], w_ref[...], preferred_element_type=F32)
    o_ref[...] = x_ref[...] + gate_ref[...] * y


def _down(st, layer, act, x, mod, w_down):
    m = st.rows
    tm = min(TM_FFN, m)
    return pl.pallas_call(
        _down_kernel,
        grid=(m // tm, D_MODEL // TN_DOWN),
        in_specs=[
            pl.BlockSpec((tm, D_FF), lambda i, j: (i, 0)),
            pl.BlockSpec((None, D_FF, TN_DOWN), lambda i, j: (layer, 0, j)),
            pl.BlockSpec((tm, TN_DOWN), lambda i, j: (i, j)),
            _mod_spec(st, layer, MOD_GATE_F, tm, TN_DOWN),
        ],
        out_specs=pl.BlockSpec((tm, TN_DOWN), lambda i, j: (i, j)),
        out_shape=jax.ShapeDtypeStruct((m, D_MODEL), F32),
        compiler_params=_params("arbitrary", "arbitrary"),
        name="ffn_down",
    )(act, w_down, x, mod)


def _final_norm_kernel(x_ref, g_ref, o_ref):
    o_ref[...] = _rms(x_ref[...], NORM_EPS) * g_ref[...]


def _final_norm(x, g):
    m = x.shape[0]
    return pl.pallas_call(
        _final_norm_kernel,
        grid=(m // TM,),
        in_specs=[pl.BlockSpec((TM, D_MODEL), lambda i: (i, 0)), pl.BlockSpec((1, D_MODEL), lambda i: (0, 0))],
        out_specs=pl.BlockSpec((TM, D_MODEL), lambda i: (i, 0)),
        out_shape=jax.ShapeDtypeStruct((m, D_MODEL), F32),
        compiler_params=_params("arbitrary"),
        name="final_norm",
    )(x, g)


def _rope_tables(st):
    half = HEAD_DIM // 2
    inv = 1.0 / (ROPE_THETA ** (jnp.arange(half, dtype=F32) * (2.0 / HEAD_DIM)))
    pos = st.pos0 + jnp.arange(st.seq)
    ang = pos.astype(F32)[:, None] * inv[None, :]
    cos, sin = jnp.cos(ang), jnp.sin(ang)
    reps = V7X_LANES // HEAD_DIM
    cos_t = jnp.tile(jnp.concatenate([cos, cos], axis=-1), (1, reps))
    sin_t = jnp.tile(jnp.concatenate([-sin, sin], axis=-1), (1, reps))
    if st.seq < TM:
        cos_t = jnp.tile(cos_t, (TM // st.seq, 1))
        sin_t = jnp.tile(sin_t, (TM // st.seq, 1))
    return cos_t, sin_t


def _run_stream(st, x, mod, hist, caches, p):
    m = st.rows
    x = x.reshape(m, D_MODEL)
    cos_tab, sin_tab = _rope_tables(st)
    k_shape = ((DEPTH * st.batch, ATTN_WIDTH, st.seq) if st.k_transposed else (DEPTH * m, ATTN_WIDTH))
    kv_stacks = (jnp.zeros(k_shape, F32), jnp.zeros((DEPTH * m * N_HEADS, V_DIM), F32))
    tails = []
    for l in range(DEPTH):
        lam_init = 0.8 - 0.6 * math.exp(-0.3 * l)
        u, q, k_stack, kb, v_stack, vb = _in_proj(st, l, x, mod, p["norm_mix_g"], p["w_in"],
                                                  cos_tab, sin_tab, kv_stacks)
        kv_stacks = (k_stack, v_stack)
        pool_out, tail = _pool(st, l, u, hist[l], p["w_pool"], p["pool_scale"])
        tails.append(tail)
        if caches is None:
            att = _prompt_attn(st, l, q, kb, vb, p["lam"], p["subln_g"], lam_init)
        else:
            att = _decode_attn(st, l, q, kb, vb, caches[0], caches[1], p["lam"], p["subln_g"], lam_init)
        x, h = _out_proj(st, l, x, pool_out, att, mod, p["w_out"], p["norm_ffn_g"])
        act = _gate_up(st, l, h, p["w_gate"], p["w_up"])
        x = _down(st, l, act, x, mod, p["w_down"])
    y = _final_norm(x, p["final_g"]).reshape(st.batch, st.seq, D_MODEL)
    k_stack, v_stack = kv_stacks
    if st.k_transposed:
        k_out = k_stack.reshape(DEPTH, st.batch, 2 * N_HEADS, HEAD_DIM, st.seq)
        k_out = jnp.transpose(k_out, (0, 1, 4, 2, 3))
    else:
        k_out = k_stack.reshape(DEPTH, st.batch, st.seq, 2 * N_HEADS, HEAD_DIM)
    v_out = v_stack.reshape(DEPTH, st.batch, st.seq, N_HEADS, V_DIM)
    return y, k_out, v_out, jnp.stack(tails)[:, :, HALO - POOL_HIST:]


def kernel(x_prompt, x_sample, c_prompt, c_sample, cache_k, cache_v, state_pool, w_mod, b_mod,
           norm_mix_g, w_in, w_pool, pool_scale, lam_q1, lam_k1, lam_q2, lam_k2, subln_g, w_out,
           norm_ffn_g, w_gate, w_up, w_down, final_g):
    p = {
        "norm_mix_g": norm_mix_g.reshape(DEPTH, 1, D_MODEL),
        "norm_ffn_g": norm_ffn_g.reshape(DEPTH, 1, D_MODEL),
        "final_g": final_g.reshape(1, D_MODEL),
        "w_in": w_in.astype(BF16),
        "w_pool": w_pool.astype(BF16),
        "pool_scale": pool_scale.reshape(DEPTH, 1, POOL_WIDTH),
        "lam": jnp.stack([lam_q1, lam_k1, lam_q2, lam_k2], axis=1),
        "subln_g": subln_g.reshape(DEPTH, 1, V_DIM),
        "w_out": w_out.astype(BF16),
        "w_gate": w_gate.astype(BF16),
        "w_up": w_up.astype(BF16),
        "w_down": w_down.astype(BF16),
    }
    c_all = jnp.concatenate([c_prompt, c_sample], axis=0)
    c_all = jnp.pad(c_all, ((0, -c_all.shape[0] % 8), (0, 0)))
    mod = _modulation(c_all, w_mod, b_mod)[:, :BATCH + DEC_BATCH]
    mod_prompt = mod[:, :BATCH].reshape(DEPTH * BATCH, 1, N_MOD * D_MODEL)
    mod_sample = jnp.repeat(mod[:, BATCH:], DEC_SEQ, axis=1)

    hist_prompt = jnp.zeros((DEPTH, BATCH, HALO, POOL_WIDTH), F32)
    hist_sample = jnp.pad(state_pool, ((0, 0), (0, 0), (HALO - POOL_HIST, 0), (0, 0)))
    caches = (jnp.transpose(cache_k, (0, 1, 3, 4, 2)).reshape(DEPTH * DEC_BATCH, ATTN_WIDTH, PAST_LEN),
              cache_v.reshape(-1, V_DIM))

    y_p, k_p, v_p, pool_p = _run_stream(PROMPT, x_prompt, mod_prompt, hist_prompt, None, p)
    y_s, k_s, v_s, pool_s = _run_stream(SAMPLE, x_sample, mod_sample, hist_sample, caches, p)
    return (y_p, y_s, k_p, v_p, pool_p, k_s, v_s, pool_s)
```

```python
import functools
import math
from typing import NamedTuple

import jax
import jax.numpy as jnp
from jax import lax
from jax.experimental import pallas as pl
from jax.experimental.pallas import tpu as pltpu

F32 = jnp.float32
BF16 = jnp.bfloat16

D_MODEL = 2048
BATCH = 4
SEQ = 4096
DEPTH = 4
DEC_BATCH = 32
DEC_SEQ = 16
PAST_LEN = 2048
CHUNK = 64
POOL_WIDTH = D_MODEL // 2
N_POOL_GROUPS = 4
POOL_GROUP = POOL_WIDTH // N_POOL_GROUPS
POOL_WINDOWS = (2, 4, 8, 16)
POOL_HIST = max(POOL_WINDOWS) - 1
HALO = POOL_HIST + 1
ATTN_WIDTH = D_MODEL - POOL_WIDTH
HEAD_DIM = 64
N_HEADS = ATTN_WIDTH // (2 * HEAD_DIM)
V_DIM = 2 * HEAD_DIM
IN_COLS = POOL_WIDTH + 3 * ATTN_WIDTH
IN_SECTIONS = IN_COLS // ATTN_WIDTH
D_FF = -(-8 * D_MODEL // (3 * 256)) * 256
ROPE_THETA = 10000.0
NORM_EPS = 1e-6
SUBLN_EPS = 1e-5
N_MOD = 6
MOD_SHIFT_A, MOD_SCALE_A, MOD_GATE_A, MOD_SHIFT_F, MOD_SCALE_F, MOD_GATE_F = range(N_MOD)

V7X_VMEM_BYTES = 64 * 1024 * 1024
V7X_LANES = 128
VMEM_LIMIT = V7X_VMEM_BYTES * 7 // 8

TM = 512
TM_FFN = 1024
TN_MOD = 1024
TF = 512
TN_DOWN = 512
TM_POOL = 1024
TQ = 1024
TK = 512
DIAG_TILES = TQ // TK
SCORE_SCALE_LOG2 = HEAD_DIM ** -0.5 * math.log2(math.e)
TK_DEC = 2048


class Stream(NamedTuple):
    batch: int
    seq: int
    pos0: int
    mod_rows: int
    k_transposed: bool

    @property
    def rows(self):
        return self.batch * self.seq


PROMPT = Stream(BATCH, SEQ, 0, 1, True)
SAMPLE = Stream(DEC_BATCH, DEC_SEQ, PAST_LEN, DEC_BATCH * DEC_SEQ, False)


def _params(*semantics, flags=None):
    return pltpu.CompilerParams(dimension_semantics=semantics, vmem_limit_bytes=VMEM_LIMIT, flags=flags)


def _rms(x, eps, axis=-1):
    return x * lax.rsqrt(jnp.mean(x * x, axis=axis, keepdims=True) + eps)


def _mod_spec(st, layer, which, tm, tn=D_MODEL):
    per_chunk = D_MODEL // tn
    if st.mod_rows == 1:
        tiles_per_batch = st.seq // tm
        return pl.BlockSpec(
            (None, 1, tn),
            lambda i, j: (layer * st.batch + i // tiles_per_batch, 0, which * per_chunk + j % per_chunk))
    return pl.BlockSpec((None, tm, tn), lambda i, j: (layer, i, which * per_chunk + j % per_chunk))


def _mod_kernel(c_ref, w_ref, b_ref, o_ref):
    c = c_ref[...]
    cs = (c * jax.nn.sigmoid(c)).astype(BF16)
    o_ref[...] = jnp.dot(cs, w_ref[...].astype(BF16), preferred_element_type=F32) + b_ref[...]


def _modulation(c_all, w_mod, b_mod):
    nb = c_all.shape[0]
    n = N_MOD * D_MODEL
    return pl.pallas_call(
        _mod_kernel,
        grid=(DEPTH, n // TN_MOD),
        in_specs=[
            pl.BlockSpec((nb, D_MODEL), lambda l, j: (0, 0)),
            pl.BlockSpec((None, D_MODEL, TN_MOD), lambda l, j: (l, 0, j)),
            pl.BlockSpec((None, 1, TN_MOD), lambda l, j: (l, 0, j)),
        ],
        out_specs=pl.BlockSpec((None, nb, TN_MOD), lambda l, j: (l, 0, j)),
        out_shape=jax.ShapeDtypeStruct((DEPTH, nb, n), F32),
        compiler_params=_params("arbitrary", "arbitrary"),
        name="modulation",
    )(c_all, w_mod, b_mod.reshape(DEPTH, 1, n))


def _rope_store(acc, cos_ref, sin_ref, scale, out_refs, transposed_ref=None):
    tm = acc.shape[0]
    lane = lax.broadcasted_iota(jnp.int32, (tm, V7X_LANES), 1)
    first_half = (lane % HEAD_DIM) < (HEAD_DIM // 2)
    cos = cos_ref[...]
    sin = sin_ref[...]
    for c in range(ATTN_WIDTH // V7X_LANES):
        cols = slice(c * V7X_LANES, (c + 1) * V7X_LANES)
        xs = acc[:, cols]
        partner = jnp.where(first_half,
                            pltpu.roll(xs, V7X_LANES - HEAD_DIM // 2, 1),
                            pltpu.roll(xs, HEAD_DIM // 2, 1))
        y = xs * cos + partner * sin
        if scale != 1.0:
            y = y * scale
        for r in out_refs:
            r[:, cols] = y.astype(r.dtype)
        if transposed_ref is not None:
            transposed_ref[cols, :] = y.T


def _in_proj_kernel(x_ref, g_ref, sc_ref, sh_ref, w_ref, cos_ref, sin_ref, *rest,
                    k_transposed, tiles):
    u_ref, q_ref, k_ref, kb_ref, v_ref, vb_ref, h_cur, h_next, acc_a, acc_b = rest[-10:]
    s = pl.program_id(0)
    section = s % IN_SECTIONS
    last = IN_SECTIONS * tiles

    def normalise(dst):
        y = _rms(x_ref[...], NORM_EPS) * g_ref[...]
        dst[...] = (y * (1.0 + sc_ref[...]) + sh_ref[...]).astype(BF16)

    def multiply(dst):
        dst[...] = jnp.dot(h_cur[...], w_ref[...], preferred_element_type=F32)

    def finish_v(acc):
        for h in range(N_HEADS):
            v_ref[pl.ds(h, acc.shape[0], stride=N_HEADS), :] = acc[:, h * V_DIM:(h + 1) * V_DIM]
        vb_ref[...] = acc[...].astype(BF16)

    @pl.when(s == 0)
    def _():
        normalise(h_cur)
        multiply(acc_a)

    @pl.when((section == 0) & (s > 0) & (s < last))
    def _():
        h_cur[...] = h_next[...]
        multiply(acc_a)
        finish_v(acc_b)

    @pl.when(section == 1)
    def _():
        multiply(acc_b)
        u_ref[...] = acc_a[...]
        normalise(h_next)

    @pl.when(section == 2)
    def _():
        multiply(acc_a)
        _rope_store(acc_b, cos_ref, sin_ref, SCORE_SCALE_LOG2, (q_ref,))

    @pl.when(section == 3)
    def _():
        multiply(acc_b)
        if k_transposed:
            _rope_store(acc_a, cos_ref, sin_ref, 1.0, (kb_ref,), k_ref)
        else:
            _rope_store(acc_a, cos_ref, sin_ref, 1.0, (k_ref, kb_ref))

    @pl.when(s == last)
    def _():
        finish_v(acc_b)


def _in_proj(st, layer, x, mod, norm_g, w_in, cos_tab, sin_tab, kv_stacks):
    m = st.rows
    tm = TM
    tiles = m // tm
    tab_tiles = cos_tab.shape[0] // tm
    steps = IN_SECTIONS * tiles + 1
    done = lambda s: jnp.maximum(s - 1, 0) // IN_SECTIONS
    normed = lambda s: jnp.minimum((s + IN_SECTIONS - 1) // IN_SECTIONS, tiles - 1)
    out_spec = pl.BlockSpec((tm, ATTN_WIDTH), lambda s: (done(s), 0))
    stack_spec = pl.BlockSpec((tm, ATTN_WIDTH), lambda s: (layer * tiles + done(s), 0))
    sds = lambda dt: jax.ShapeDtypeStruct((m, ATTN_WIDTH), dt)
    stack_sds = jax.ShapeDtypeStruct((DEPTH * m, ATTN_WIDTH), F32)
    v_stack_spec = pl.BlockSpec((tm * N_HEADS, V_DIM), lambda s: (layer * tiles + done(s), 0))
    v_stack_sds = jax.ShapeDtypeStruct((DEPTH * m * N_HEADS, V_DIM), F32)
    out_specs = [out_spec, out_spec, stack_spec, out_spec, v_stack_spec, out_spec]
    out_shape = [sds(F32), sds(BF16), stack_sds, sds(BF16), v_stack_sds, sds(BF16)]
    if st.k_transposed:
        tiles_per_batch = st.seq // tm
        out_specs[2] = pl.BlockSpec(
            (None, ATTN_WIDTH, tm),
            lambda s: (layer * st.batch + done(s) // tiles_per_batch, 0, done(s) % tiles_per_batch))
        out_shape[2] = jax.ShapeDtypeStruct((DEPTH * st.batch, ATTN_WIDTH, st.seq), F32)
    if st.mod_rows == 1:
        tiles_per_batch = st.seq // tm
        mod_spec = lambda which: pl.BlockSpec(
            (None, 1, D_MODEL), lambda s: (layer * st.batch + normed(s) // tiles_per_batch, 0, which))
    else:
        mod_spec = lambda which: pl.BlockSpec((None, tm, D_MODEL), lambda s: (layer, normed(s), which),
                                              pipeline_mode=pl.Buffered(1))
    in_specs = [
        pl.BlockSpec((tm, D_MODEL), lambda s: (normed(s), 0)),
        pl.BlockSpec((None, 1, D_MODEL), lambda s: (layer, 0, 0)),
        mod_spec(MOD_SCALE_A),
        mod_spec(MOD_SHIFT_A),
        pl.BlockSpec((None, D_MODEL, ATTN_WIDTH),
                     lambda s: (layer, 0, jnp.minimum(s, steps - 2) % IN_SECTIONS)),
        pl.BlockSpec((tm, V7X_LANES), lambda s: (done(s) % tab_tiles, 0)),
        pl.BlockSpec((tm, V7X_LANES), lambda s: (done(s) % tab_tiles, 0)),
    ]
    args = [x, norm_g, mod, mod, w_in, cos_tab, sin_tab]
    aliases = {}
    if kv_stacks is not None:
        aliases = {len(args): 2, len(args) + 1: 4}
        in_specs += [pl.BlockSpec(memory_space=pl.ANY)] * 2
        args += list(kv_stacks)
    return pl.pallas_call(
        functools.partial(_in_proj_kernel, k_transposed=st.k_transposed, tiles=tiles),
        grid=(steps,),
        in_specs=in_specs,
        out_specs=out_specs,
        out_shape=out_shape,
        input_output_aliases=aliases,
        scratch_shapes=[pltpu.VMEM((tm, D_MODEL), BF16), pltpu.VMEM((tm, D_MODEL), BF16),
                        pltpu.VMEM((tm, ATTN_WIDTH), F32), pltpu.VMEM((tm, ATTN_WIDTH), F32)],
        compiler_params=_params("arbitrary"),
        name="in_proj",
    )(*args)


def _pool_kernel(u_ref, hist_ref, wp_ref, ps_ref, o_ref, tail_ref, ext_ref, *, tm, pos0):
    i = pl.program_id(1)

    @pl.when(i == 0)
    def _():
        ext_ref[0:HALO, :] = hist_ref[...]

    @pl.when(i > 0)
    def _():
        ext_ref[0:HALO, :] = ext_ref[tm:tm + HALO, :]

    ext_ref[HALO:, :] = u_ref[...]

    @pl.when(i == pl.num_programs(1) - 1)
    def _():
        tail_ref[...] = ext_ref[tm:tm + HALO, :]

    pos = pos0 + i * tm + lax.broadcasted_iota(jnp.int32, (tm, 1), 0)
    for g, w in enumerate(POOL_WINDOWS):
        cols = slice(g * POOL_GROUP, (g + 1) * POOL_GROUP)
        cur = u_ref[:, cols]
        s = ext_ref[:, cols]
        shift = 1
        while shift < w:
            s = s + pltpu.roll(s, shift, 0)
            shift *= 2
        s = s[HALO:]
        count = jnp.minimum(w, pos + 1).astype(F32)
        d = (s / count - cur).astype(BF16)
        y = jnp.dot(d, wp_ref[g], preferred_element_type=F32) * ps_ref[:, cols]
        o_ref[:, cols] = y.astype(BF16)


def _pool(st, layer, u, hist, w_pool, pool_scale):
    tm = min(TM_POOL, st.seq)
    tiles = st.seq // tm
    return pl.pallas_call(
        functools.partial(_pool_kernel, tm=tm, pos0=st.pos0),
        grid=(st.batch, tiles),
        in_specs=[
            pl.BlockSpec((tm, POOL_WIDTH), lambda b, i: (b * tiles + i, 0)),
            pl.BlockSpec((None, HALO, POOL_WIDTH), lambda b, i: (b, 0, 0)),
            pl.BlockSpec((None, N_POOL_GROUPS, POOL_GROUP, POOL_GROUP), lambda b, i: (layer, 0, 0, 0)),
            pl.BlockSpec((None, 1, POOL_WIDTH), lambda b, i: (layer, 0, 0)),
        ],
        out_specs=[pl.BlockSpec((tm, POOL_WIDTH), lambda b, i: (b * tiles + i, 0)),
                   pl.BlockSpec((None, HALO, POOL_WIDTH), lambda b, i: (b, 0, 0))],
        out_shape=[jax.ShapeDtypeStruct((st.rows, POOL_WIDTH), BF16),
                   jax.ShapeDtypeStruct((st.batch, HALO, POOL_WIDTH), F32)],
        scratch_shapes=[pltpu.VMEM((tm + HALO, POOL_WIDTH), F32)],
        compiler_params=_params("arbitrary", "arbitrary"),
        name="pool_mix",
    )(u, hist, w_pool, pool_scale)


def _lambda(lam_ref, lam_init):
    lam = lam_ref[...]
    s1 = jnp.sum(lam[0:1] * lam[1:2], axis=-1, keepdims=True)
    s2 = jnp.sum(lam[2:3] * lam[3:4], axis=-1, keepdims=True)
    return jnp.exp(s1) - jnp.exp(s2) + lam_init


def _online_softmax_step(s, v, m_ref, l_ref, acc_ref):
    m_prev = m_ref[...]
    m_new = jnp.maximum(m_prev, jnp.max(s, axis=-1, keepdims=True))
    alpha = jnp.exp2(m_prev - m_new)
    p = jnp.exp2(s - m_new)
    l_ref[...] = alpha * l_ref[...] + jnp.sum(p, axis=-1, keepdims=True)
    acc_ref[...] = alpha * acc_ref[...] + jnp.dot(p.astype(BF16), v, preferred_element_type=F32)
    m_ref[...] = m_new


def _sub_ln(o1, o2, lam, g, lam_init, axis=-1):
    a = o1 - lam * o2
    return (_rms(a, SUBLN_EPS, axis) * g) * (1.0 - lam_init)


_NT = (((1,), (1,)), ((), ()))


def _prompt_attn_kernel(q_ref, k_ref, v_ref, lam_ref, g_ref, o_ref,
                        qz_ref, vt_ref, sa_ref, sb_ref, m_ref, l_ref, acc_ref, *, lam_init):
    i = pl.program_id(2)

    @pl.when(i == 0)
    def _():
        for t in range(vt_ref.shape[0]):
            vt_ref[t] = v_ref[t * TK:(t + 1) * TK, :].astype(F32).T.astype(BF16)

    q = q_ref[...]
    lane = lax.broadcasted_iota(jnp.int32, (TQ, V_DIM), 1)
    zero = jnp.zeros_like(q)
    qz_ref[0:TQ, :] = jnp.where(lane < HEAD_DIM, q, zero)
    qz_ref[TQ:, :] = jnp.where(lane >= HEAD_DIM, q, zero)
    m_ref[...] = jnp.full_like(m_ref, -jnp.inf)
    l_ref[...] = jnp.zeros_like(l_ref)
    acc_ref[...] = jnp.zeros_like(acc_ref)

    def scores(t, s_ref):
        start = pl.multiple_of(t * TK, TK)
        s_ref[...] = lax.dot_general(k_ref[pl.ds(start, TK), :], qz_ref[...], _NT,
                                     preferred_element_type=F32)

    def consume(t, s_ref, diag=None):
        q_lo = 0 if diag is None else diag * TK
        width = TQ - q_lo
        col_sets = (slice(q_lo, TQ), slice(TQ + q_lo, 2 * TQ))

        def gather(ref):
            if q_lo == 0:
                return ref[...]
            return jnp.concatenate([ref[:, c] for c in col_sets], axis=1)

        def scatter(ref, val):
            if q_lo == 0:
                ref[...] = val
            else:
                for n, c in enumerate(col_sets):
                    ref[:, c] = val[:, n * width:(n + 1) * width]

        s = gather(s_ref)
        if diag is not None:
            key = q_lo + lax.broadcasted_iota(jnp.int32, (TK, 2 * width), 0)
            col = lax.broadcasted_iota(jnp.int32, (TK, 2 * width), 1)
            query = q_lo + jnp.where(col >= width, col - width, col)
            s = jnp.where((key // CHUNK) <= (query // CHUNK), s, -jnp.inf)
        m_prev = gather(m_ref)
        m_new = jnp.maximum(m_prev, jnp.max(s, axis=0, keepdims=True))
        alpha = jnp.exp2(m_prev - m_new)
        p = jnp.exp2(s - m_new)
        scatter(l_ref, alpha * gather(l_ref) + jnp.sum(p, axis=0, keepdims=True))
        scatter(acc_ref, alpha * gather(acc_ref) + jnp.dot(vt_ref[t], p.astype(BF16),
                                                           preferred_element_type=F32))
        scatter(m_ref, m_new)

    scores(0, sa_ref)

    def pair(p, carry):
        t = 2 * p
        scores(t + 1, sb_ref)
        consume(t, sa_ref)
        scores(t + 2, sa_ref)
        consume(t + 1, sb_ref)
        return carry

    lax.fori_loop(0, (DIAG_TILES // 2) * i, pair, 0)
    first_diag = DIAG_TILES * i
    buffers = (sa_ref, sb_ref)
    for d in range(DIAG_TILES):
        if d + 1 < DIAG_TILES:
            scores(first_diag + d + 1, buffers[(d + 1) % 2])
        consume(first_diag + d, buffers[d % 2], diag=d)

    o = acc_ref[...] / l_ref[...]
    y = _sub_ln(o[:, 0:TQ], o[:, TQ:], _lambda(lam_ref, lam_init), g_ref[...], lam_init, axis=0)
    o_ref[...] = y.T.astype(BF16)


def _prompt_attn(st, layer, q, k, v, lam, subln_g, lam_init):
    assert DIAG_TILES % 2 == 0
    nq = st.seq // TQ
    return pl.pallas_call(
        functools.partial(_prompt_attn_kernel, lam_init=lam_init),
        grid=(st.batch, N_HEADS, nq),
        in_specs=[
            pl.BlockSpec((TQ, V_DIM), lambda b, h, i: (b * nq + i, h)),
            pl.BlockSpec((st.seq, V_DIM), lambda b, h, i: (b, h)),
            pl.BlockSpec((st.seq, V_DIM), lambda b, h, i: (b, h)),
            pl.BlockSpec((None, 4, HEAD_DIM), lambda b, h, i: (layer, 0, 0)),
            pl.BlockSpec((None, V_DIM, 1), lambda b, h, i: (layer, 0, 0)),
        ],
        out_specs=pl.BlockSpec((TQ, V_DIM), lambda b, h, i: (b * nq + i, h)),
        out_shape=jax.ShapeDtypeStruct((st.rows, ATTN_WIDTH), BF16),
        scratch_shapes=[
            pltpu.VMEM((2 * TQ, V_DIM), BF16),
            pltpu.VMEM((st.seq // TK, V_DIM, TK), BF16),
            pltpu.VMEM((TK, 2 * TQ), F32),
            pltpu.VMEM((TK, 2 * TQ), F32),
            pltpu.VMEM((1, 2 * TQ), F32),
            pltpu.VMEM((1, 2 * TQ), F32),
            pltpu.VMEM((V_DIM, 2 * TQ), F32),
        ],
        compiler_params=_params("arbitrary", "arbitrary", "arbitrary"),
        name="prompt_attn",
    )(q, k, v, lam, subln_g.reshape(DEPTH, V_DIM, 1))


def _decode_attn_kernel(q_ref, kc_ref, vc_ref, kn_ref, vn_ref, lam_ref, g_ref, o_ref,
                        m_ref, l_ref, acc_ref, *, lam_init):
    t = pl.program_id(1)
    n_score_heads = 2 * N_HEADS
    pair_rows = 2 * DEC_SEQ

    @pl.when(t == 0)
    def _():
        m_ref[...] = jnp.full_like(m_ref, -jnp.inf)
        l_ref[...] = jnp.zeros_like(l_ref)
        acc_ref[...] = jnp.zeros_like(acc_ref)

    q = q_ref[...]
    q_heads = [q[:, h * HEAD_DIM:(h + 1) * HEAD_DIM] for h in range(n_score_heads)]

    def update(score_of_head, value_of_head):
        s = jnp.concatenate([score_of_head(h) for h in range(n_score_heads)], axis=0)
        m_prev = m_ref[...]
        m_new = jnp.maximum(m_prev, jnp.max(s, axis=-1, keepdims=True))
        alpha = jnp.exp2(m_prev - m_new)
        p = jnp.exp2(s - m_new)
        l_ref[...] = alpha * l_ref[...] + jnp.sum(p, axis=-1, keepdims=True)
        pb = p.astype(BF16)
        pv = jnp.concatenate(
            [jnp.dot(pb[j * pair_rows:(j + 1) * pair_rows], value_of_head(j), preferred_element_type=F32)
             for j in range(N_HEADS)], axis=0)
        acc_ref[...] = alpha * acc_ref[...] + pv
        m_ref[...] = m_new

    update(lambda h: jnp.dot(q_heads[h], kc_ref[h * HEAD_DIM:(h + 1) * HEAD_DIM, :].astype(BF16),
                             preferred_element_type=F32),
           lambda j: vc_ref[pl.ds(j, TK_DEC, stride=N_HEADS), :].astype(BF16))

    @pl.when(t == pl.num_programs(1) - 1)
    def _():
        update(lambda h: lax.dot_general(q_heads[h], kn_ref[:, h * HEAD_DIM:(h + 1) * HEAD_DIM], _NT,
                                         preferred_element_type=F32),
               lambda j: vn_ref[:, j * V_DIM:(j + 1) * V_DIM])
        o = acc_ref[...] / l_ref[...]
        lam = _lambda(lam_ref, lam_init)
        g = g_ref[...]
        for j in range(N_HEADS):
            r0 = j * pair_rows
            y = _sub_ln(o[r0:r0 + DEC_SEQ], o[r0 + DEC_SEQ:r0 + pair_rows], lam, g, lam_init)
            o_ref[:, j * V_DIM:(j + 1) * V_DIM] = y.astype(BF16)


def _decode_attn(st, layer, q, k_new, v_new, cache_k, cache_v, lam, subln_g, lam_init):
    rows_bd = 2 * N_HEADS * DEC_SEQ
    tok = pl.BlockSpec((DEC_SEQ, ATTN_WIDTH), lambda b, t: (b, 0))
    tiles = PAST_LEN // TK_DEC
    cache_k_spec = pl.BlockSpec((None, ATTN_WIDTH, TK_DEC), lambda b, t: (layer * st.batch + b, 0, t))
    cache_v_spec = pl.BlockSpec((TK_DEC * N_HEADS, V_DIM),
                                lambda b, t: ((layer * st.batch + b) * tiles + t, 0))
    return pl.pallas_call(
        functools.partial(_decode_attn_kernel, lam_init=lam_init),
        grid=(st.batch, tiles),
        in_specs=[
            tok, cache_k_spec, cache_v_spec, tok, tok,
            pl.BlockSpec((None, 4, HEAD_DIM), lambda b, t: (layer, 0, 0)),
            pl.BlockSpec((None, 1, V_DIM), lambda b, t: (layer, 0, 0)),
        ],
        out_specs=tok,
        out_shape=jax.ShapeDtypeStruct((st.rows, ATTN_WIDTH), BF16),
        scratch_shapes=[
            pltpu.VMEM((rows_bd, 1), F32),
            pltpu.VMEM((rows_bd, 1), F32),
            pltpu.VMEM((rows_bd, V_DIM), F32),
        ],
        compiler_params=_params("arbitrary", "arbitrary"),
        name="decode_attn",
    )(q, cache_k, cache_v, k_new, v_new, lam, subln_g)


def _out_proj_kernel(x_ref, pool_ref, att_ref, w_ref, gate_ref, g_ref, sc_ref, sh_ref,
                     xo_ref, h_ref):
    mixed = (jnp.dot(pool_ref[...], w_ref[0:POOL_WIDTH, :], preferred_element_type=F32)
             + jnp.dot(att_ref[...], w_ref[POOL_WIDTH:, :], preferred_element_type=F32))
    x = x_ref[...] + gate_ref[...] * mixed
    xo_ref[...] = x
    y = _rms(x, NORM_EPS) * g_ref[...]
    h_ref[...] = (y * (1.0 + sc_ref[...]) + sh_ref[...]).astype(BF16)


def _out_proj(st, layer, x, pool_out, att, mod, w_out, norm_g):
    m = st.rows
    tm = TM
    row = lambda i, j: (i, 0)
    return pl.pallas_call(
        _out_proj_kernel,
        grid=(m // tm, 1),
        in_specs=[
            pl.BlockSpec((tm, D_MODEL), row),
            pl.BlockSpec((tm, POOL_WIDTH), row),
            pl.BlockSpec((tm, ATTN_WIDTH), row),
            pl.BlockSpec((None, D_MODEL, D_MODEL), lambda i, j: (layer, 0, 0),
                         pipeline_mode=pl.Buffered(1)),
            _mod_spec(st, layer, MOD_GATE_A, tm),
            pl.BlockSpec((None, 1, D_MODEL), lambda i, j: (layer, 0, 0)),
            _mod_spec(st, layer, MOD_SCALE_F, tm),
            _mod_spec(st, layer, MOD_SHIFT_F, tm),
        ],
        out_specs=[pl.BlockSpec((tm, D_MODEL), row), pl.BlockSpec((tm, D_MODEL), row)],
        out_shape=[jax.ShapeDtypeStruct((m, D_MODEL), F32), jax.ShapeDtypeStruct((m, D_MODEL), BF16)],
        compiler_params=_params("arbitrary", "arbitrary"),
        name="out_proj",
    )(x, pool_out, att, w_out, mod, norm_g, mod, mod)


def _gate_up_kernel(h_ref, wg_ref, wu_ref, o_ref):
    h = h_ref[...]
    g = jnp.dot(h, wg_ref[...], preferred_element_type=F32)
    u = jnp.dot(h, wu_ref[...], preferred_element_type=F32)
    o_ref[...] = (g * jax.nn.sigmoid(g) * u).astype(BF16)


def _gate_up(st, layer, h, w_gate, w_up):
    m = st.rows
    tm = min(2 * TM_FFN, m)
    w_spec = pl.BlockSpec((None, D_MODEL, TF), lambda i, j: (layer, 0, j))
    return pl.pallas_call(
        _gate_up_kernel,
        grid=(m // tm, D_FF // TF),
        in_specs=[pl.BlockSpec((tm, D_MODEL), lambda i, j: (i, 0)), w_spec, w_spec],
        out_specs=pl.BlockSpec((tm, TF), lambda i, j: (i, j)),
        out_shape=jax.ShapeDtypeStruct((m, D_FF), BF16),
        compiler_params=_params("arbitrary", "arbitrary"),
        name="ffn_gate_up",
    )(h, w_gate, w_up)


def _down_kernel(a_ref, w_ref, x_ref, gate_ref, o_ref):
    y = jnp.dot(a_ref[...], w_ref[...], preferred_element_type=F32)
    o_ref[...] = x_ref[...] + gate_ref[...] * y


def _down(st, layer, act, x, mod, w_down):
    m = st.rows
    tm = min(TM_FFN, m)
    return pl.pallas_call(
        _down_kernel,
        grid=(m // tm, D_MODEL // TN_DOWN),
        in_specs=[
            pl.BlockSpec((tm, D_FF), lambda i, j: (i, 0)),
            pl.BlockSpec((None, D_FF, TN_DOWN), lambda i, j: (layer, 0, j)),
            pl.BlockSpec((tm, TN_DOWN), lambda i, j: (i, j)),
            _mod_spec(st, layer, MOD_GATE_F, tm, TN_DOWN),
        ],
        out_specs=pl.BlockSpec((tm, TN_DOWN), lambda i, j: (i, j)),
        out_shape=jax.ShapeDtypeStruct((m, D_MODEL), F32),
        compiler_params=_params("arbitrary", "arbitrary"),
        name="ffn_down",
    )(act, w_down, x, mod)


def _final_norm_kernel(x_ref, g_ref, o_ref):
    o_ref[...] = _rms(x_ref[...], NORM_EPS) * g_ref[...]


def _final_norm(x, g):
    m = x.shape[0]
    return pl.pallas_call(
        _final_norm_kernel,
        grid=(m // TM,),
        in_specs=[pl.BlockSpec((TM, D_MODEL), lambda i: (i, 0)), pl.BlockSpec((1, D_MODEL), lambda i: (0, 0))],
        out_specs=pl.BlockSpec((TM, D_MODEL), lambda i: (i, 0)),
        out_shape=jax.ShapeDtypeStruct((m, D_MODEL), F32),
        compiler_params=_params("arbitrary"),
        name="final_norm",
    )(x, g)


def _rope_tables(st):
    half = HEAD_DIM // 2
    inv = 1.0 / (ROPE_THETA ** (jnp.arange(half, dtype=F32) * (2.0 / HEAD_DIM)))
    pos = st.pos0 + jnp.arange(st.seq)
    ang = pos.astype(F32)[:, None] * inv[None, :]
    cos, sin = jnp.cos(ang), jnp.sin(ang)
    reps = V7X_LANES // HEAD_DIM
    cos_t = jnp.tile(jnp.concatenate([cos, cos], axis=-1), (1, reps))
    sin_t = jnp.tile(jnp.concatenate([-sin, sin], axis=-1), (1, reps))
    if st.seq < TM:
        cos_t = jnp.tile(cos_t, (TM // st.seq, 1))
        sin_t = jnp.tile(sin_t, (TM // st.seq, 1))
    return cos_t, sin_t


def _run_stream(st, x, mod, hist, caches, p):
    m = st.rows
    x = x.reshape(m, D_MODEL)
    cos_tab, sin_tab = _rope_tables(st)
    k_shape = ((DEPTH * st.batch, ATTN_WIDTH, st.seq) if st.k_transposed else (DEPTH * m, ATTN_WIDTH))
    kv_stacks = (jnp.zeros(k_shape, F32), jnp.zeros((DEPTH * m * N_HEADS, V_DIM), F32))
    tails = []
    for l in range(DEPTH):
        lam_init = 0.8 - 0.6 * math.exp(-0.3 * l)
        u, q, k_stack, kb, v_stack, vb = _in_proj(st, l, x, mod, p["norm_mix_g"], p["w_in"],
                                                  cos_tab, sin_tab, kv_stacks)
        kv_stacks = (k_stack, v_stack)
        pool_out, tail = _pool(st, l, u, hist[l], p["w_pool"], p["pool_scale"])
        tails.append(tail)
        if caches is None:
            att = _prompt_attn(st, l, q, kb, vb, p["lam"], p["subln_g"], lam_init)
        else:
            att = _decode_attn(st, l, q, kb, vb, caches[0], caches[1], p["lam"], p["subln_g"], lam_init)
        x, h = _out_proj(st, l, x, pool_out, att, mod, p["w_out"], p["norm_ffn_g"])
        act = _gate_up(st, l, h, p["w_gate"], p["w_up"])
        x = _down(st, l, act, x, mod, p["w_down"])
    y = _final_norm(x, p["final_g"]).reshape(st.batch, st.seq, D_MODEL)
    k_stack, v_stack = kv_stacks
    if st.k_transposed:
        k_out = k_stack.reshape(DEPTH, st.batch, 2 * N_HEADS, HEAD_DIM, st.seq)
        k_out = jnp.transpose(k_out, (0, 1, 4, 2, 3))
    else:
        k_out = k_stack.reshape(DEPTH, st.batch, st.seq, 2 * N_HEADS, HEAD_DIM)
    v_out = v_stack.reshape(DEPTH, st.batch, st.seq, N_HEADS, V_DIM)
    return y, k_out, v_out, jnp.stack(tails)[:, :, HALO - POOL_HIST:]


def kernel(x_prompt, x_sample, c_prompt, c_sample, cache_k, cache_v, state_pool, w_mod, b_mod,
           norm_mix_g, w_in, w_pool, pool_scale, lam_q1, lam_k1, lam_q2, lam_k2, subln_g, w_out,
           norm_ffn_g, w_gate, w_up, w_down, final_g):
    p = {
        "norm_mix_g": norm_mix_g.reshape(DEPTH, 1, D_MODEL),
        "norm_ffn_g": norm_ffn_g.reshape(DEPTH, 1, D_MODEL),
        "final_g": final_g.reshape(1, D_MODEL),
        "w_in": w_in.astype(BF16),
        "w_pool": w_pool.astype(BF16),
        "pool_scale": pool_scale.reshape(DEPTH, 1, POOL_WIDTH),
        "lam": jnp.stack([lam_q1, lam_k1, lam_q2, lam_k2], axis=1),
        "subln_g": subln_g.reshape(DEPTH, 1, V_DIM),
        "w_out": w_out.astype(BF16),
        "w_gate": w_gate.astype(BF16),
        "w_up": w_up.astype(BF16),
        "w_down": w_down.astype(BF16),
    }
    c_all = jnp.concatenate([c_prompt, c_sample], axis=0)
    c_all = jnp.pad(c_all, ((0, -c_all.shape[0] % 8), (0, 0)))
    mod = _modulation(c_all, w_mod, b_mod)[:, :BATCH + DEC_BATCH]
    mod_prompt = mod[:, :BATCH].reshape(DEPTH * BATCH, 1, N_MOD * D_MODEL)
    mod_sample = jnp.repeat(mod[:, BATCH:], DEC_SEQ, axis=1)

    hist_prompt = jnp.zeros((DEPTH, BATCH, HALO, POOL_WIDTH), F32)
    hist_sample = jnp.pad(state_pool, ((0, 0), (0, 0), (HALO - POOL_HIST, 0), (0, 0)))
    caches = (jnp.transpose(cache_k, (0, 1, 3, 4, 2)).reshape(DEPTH * DEC_BATCH, ATTN_WIDTH, PAST_LEN),
              cache_v.reshape(-1, V_DIM))

    y_p, k_p, v_p, pool_p = _run_stream(PROMPT, x_prompt, mod_prompt, hist_prompt, None, p)
    y_s, k_s, v_s, pool_s = _run_stream(SAMPLE, x_sample, mod_sample, hist_sample, caches, p)
    return (y_p, y_s, k_p, v_p, pool_p, k_s, v_s, pool_s)
```
